```python
import jax, jax.numpy as jnp
from jax import lax
import numpy as np

D_MODEL = 1024
BATCH = 2
SEQ = 8192
DEPTH = 1
DEC_BATCH = 32
DEC_SEQ = 8
PAST_LEN = 16384
PAGE_SIZE = 128

A_HEADS = 8
A_HEAD_DIM = D_MODEL // 16
A_WIDTH = A_HEADS * A_HEAD_DIM
MOBA_BLOCK = 256
MOBA_TOPK = 3
Q_BLOCK = 64
B_HEADS = 4
B_VAL_WIDTH = D_MODEL // 2
B_KEY_WIDTH = B_VAL_WIDTH // 2
B_KEY_DIM = B_KEY_WIDTH // B_HEADS
B_VAL_DIM = B_VAL_WIDTH // B_HEADS
GATE_RANK = 16
GATE_TAU = 16.0
GLA_CHUNK = 64
D_MIX = A_WIDTH + B_VAL_WIDTH
IN_SPLITS = (A_WIDTH, A_WIDTH, A_WIDTH, A_WIDTH,
             B_KEY_WIDTH, B_KEY_WIDTH, B_VAL_WIDTH, B_VAL_WIDTH, GATE_RANK)
N_IN = sum(IN_SPLITS)
EPS = 1e-6
NEG = -1e30

kernel_name = 'hymba_moba_gla_step'


def rms_norm(x, g):
    xf = x.astype(jnp.float32)
    y = xf * lax.rsqrt(jnp.mean(xf * xf, axis=-1, keepdims=True) + EPS)
    return (y * g.astype(jnp.float32)).astype(x.dtype)


def alibi_slopes(n):
    return jnp.asarray([2.0 ** (-8.0 * (h + 1) / n) for h in range(n)], jnp.float32)


def branch_inputs(x, g_pre, w_in, g_q, g_k, w_a2, b_a):
    bsz, t, _ = x.shape
    h = rms_norm(x, g_pre)
    proj = h @ w_in
    offs = np.cumsum(IN_SPLITS)[:-1].tolist()
    qa, ka, va, za, qb, kb, vb, zb, ab = jnp.split(proj, offs, axis=-1)
    qa = rms_norm(qa.reshape(bsz, t, A_HEADS, A_HEAD_DIM), g_q)
    ka = rms_norm(ka.reshape(bsz, t, A_HEADS, A_HEAD_DIM), g_k)
    va = va.reshape(bsz, t, A_HEADS, A_HEAD_DIM)
    qb = qb.reshape(bsz, t, B_HEADS, B_KEY_DIM)
    kb = kb.reshape(bsz, t, B_HEADS, B_KEY_DIM)
    vb = vb.reshape(bsz, t, B_HEADS, B_VAL_DIM)
    loga = jax.nn.log_sigmoid((ab @ w_a2 + b_a).astype(jnp.float32)) / GATE_TAU
    loga = loga.reshape(bsz, t, B_HEADS, B_KEY_DIM)
    return qa, ka, va, za, qb, kb, vb, zb, loga


def merge_out(x, oa, za, ob, zb, g_gla, w_out):
    bsz, t, _ = x.shape
    ya = oa.reshape(bsz, t, A_WIDTH).astype(x.dtype) * jax.nn.silu(za)
    yb = rms_norm(ob.astype(x.dtype), g_gla).reshape(bsz, t, B_VAL_WIDTH) * jax.nn.silu(zb)
    return x + jnp.concatenate([ya, yb], axis=-1) @ w_out


def to_blocks(k):
    b, length, h, d = k.shape
    n_blocks = max(-(-length // MOBA_BLOCK), MOBA_TOPK)
    k = jnp.pad(k, ((0, 0), (0, n_blocks * MOBA_BLOCK - length), (0, 0), (0, 0)))
    return k.reshape(b, n_blocks, MOBA_BLOCK, h, d).transpose(0, 3, 1, 2, 4)


def moba_core(q, qpos, kb, vb, kmean, slopes):
    bsz, nh, _, d = q.shape
    n_blocks = kb.shape[2]
    qblk = qpos // MOBA_BLOCK
    gate = jnp.einsum('bhqd,bhnd->bhqn', q.astype(jnp.float32), kmean)
    fully_past = jnp.arange(n_blocks)[None, :] < qblk[:, None]
    gate = jnp.where(fully_past, gate, NEG)
    _, top = lax.top_k(gate, MOBA_TOPK)
    top = top.astype(jnp.int32)
    sel_ok = top < qblk[:, None]
    own = jnp.broadcast_to(qblk[:, None], top.shape[:-1] + (1,)).astype(jnp.int32)
    idx = jnp.concatenate([top, own], axis=-1)
    ok = jnp.concatenate([sel_ok, jnp.ones(own.shape, bool)], axis=-1)
    bi = jnp.arange(bsz)[:, None, None, None]
    hi = jnp.arange(nh)[None, :, None, None]
    k_sel = kb[bi, hi, idx]
    v_sel = vb[bi, hi, idx]
    s = jnp.einsum('bhqd,bhqnkd->bhqnk', q, k_sel).astype(jnp.float32) * (d ** -0.5)
    kpos = idx[..., None] * MOBA_BLOCK + jnp.arange(MOBA_BLOCK, dtype=jnp.int32)
    dist = qpos[:, None, None] - kpos
    s = s - slopes[:, None, None, None] * dist.astype(jnp.float32)
    s = jnp.where(ok[..., None] & (dist >= 0), s, NEG)
    p = jax.nn.softmax(s.reshape(s.shape[:3] + (-1,)), axis=-1).reshape(s.shape)
    return jnp.einsum('bhqnk,bhqnkd->bhqd', p.astype(v_sel.dtype), v_sel)


def moba_prompt(qa, ka, va, slopes):
    bsz, t, nh, d = qa.shape
    kb, vb = to_blocks(ka), to_blocks(va)
    kmean = kb.astype(jnp.float32).mean(axis=3)
    n_chunks = t // Q_BLOCK
    q_chunks = qa.reshape(bsz, n_chunks, Q_BLOCK, nh, d).transpose(1, 0, 3, 2, 4)
    starts = jnp.arange(n_chunks, dtype=jnp.int32) * Q_BLOCK

    def one(args):
        qc, s0 = args
        qpos = s0 + jnp.arange(Q_BLOCK, dtype=jnp.int32)
        return moba_core(qc, qpos, kb, vb, kmean, slopes)

    o = lax.map(one, (q_chunks, starts))
    return o.transpose(1, 0, 3, 2, 4).reshape(bsz, t, nh, d)


def moba_sample(qa, ka, va, cache_k, cache_v, layer, page_table, slopes):
    n_pages = page_table.shape[1]
    past = n_pages * PAGE_SIZE
    t_new = qa.shape[1]
    qpos = past + jnp.arange(t_new, dtype=jnp.int32)

    def one(args):
        pt, q, kn, vn = args
        kp = cache_k[layer, pt].reshape(past, A_HEADS, A_HEAD_DIM)
        vp = cache_v[layer, pt].reshape(past, A_HEADS, A_HEAD_DIM)
        k_all = jnp.concatenate([kp, kn.astype(kp.dtype)], axis=0)[None]
        v_all = jnp.concatenate([vp, vn.astype(vp.dtype)], axis=0)[None]
        kb, vb = to_blocks(k_all), to_blocks(v_all)
        kmean = kb.astype(jnp.float32).mean(axis=3)
        return moba_core(q.transpose(1, 0, 2)[None], qpos, kb, vb, kmean, slopes)[0]

    o = lax.map(one, (page_table, qa, ka, va))
    return o.transpose(0, 2, 1, 3)


def gla_chunked(q, k, v, loga, s0, chunk):
    bsz, t, nh, dk = q.shape
    n_chunks = t // chunk

    def split(a):
        return a.astype(jnp.float32).reshape(bsz, n_chunks, chunk, nh, a.shape[-1]).transpose(1, 0, 3, 2, 4)

    qc, kc, vc, gc = split(q) * (dk ** -0.5), split(k), split(v), split(loga)
    causal = jnp.tril(jnp.ones((chunk, chunk), bool))

    def step(state, xs):
        qi, ki, vi, gi = xs
        b = jnp.cumsum(gi, axis=2)
        decay = jnp.exp(jnp.where(causal[:, :, None], b[:, :, :, None, :] - b[:, :, None, :, :], NEG))
        attn = jnp.einsum('bhtd,bhsd,bhtsd->bhts', qi, ki, decay)
        out = attn @ vi + jnp.einsum('bhtd,bhdv->bhtv', qi * jnp.exp(b), state)
        b_last = b[:, :, -1:, :]
        state = jnp.exp(b_last[:, :, 0, :])[..., None] * state + jnp.einsum('bhsd,bhsv->bhdv', ki * jnp.exp(b_last - b), vi)
        return state, out

    s_fin, o = lax.scan(step, s0.astype(jnp.float32), (qc, kc, vc, gc))
    return o.transpose(1, 0, 3, 2, 4).reshape(bsz, t, nh, v.shape[-1]), s_fin


def setup_inputs(seed: int = 0) -> dict:
    key = jax.random.key(seed)
    ks = jax.random.split(key, 16)
    f32 = jnp.float32
    n_pages = PAST_LEN // PAGE_SIZE
    n_used = DEC_BATCH * n_pages
    n_pool = n_used + max(1, n_used // 4)
    x_prompt = jax.random.normal(ks[0], (BATCH, SEQ, D_MODEL), f32)
    x_sample = jax.random.normal(ks[1], (DEC_BATCH, DEC_SEQ, D_MODEL), f32)
    cache_k = jax.random.normal(ks[2], (DEPTH, n_pool, PAGE_SIZE, A_HEADS, A_HEAD_DIM), f32)
    cache_v = jax.random.normal(ks[3], (DEPTH, n_pool, PAGE_SIZE, A_HEADS, A_HEAD_DIM), f32)
    state_gla = 0.5 * jax.random.normal(ks[4], (DEPTH, DEC_BATCH, B_HEADS, B_KEY_DIM, B_VAL_DIM), f32)
    page_table = jax.random.permutation(ks[5], n_pool)[:n_used].reshape(DEC_BATCH, n_pages).astype(jnp.int32)
    g_pre = 1.0 + 0.1 * jax.random.normal(ks[6], (DEPTH, D_MODEL), f32)
    w_in = jax.random.normal(ks[7], (DEPTH, D_MODEL, N_IN), f32) * D_MODEL ** -0.5
    g_q = 1.0 + 0.1 * jax.random.normal(ks[8], (DEPTH, A_HEAD_DIM), f32)
    g_k = 1.0 + 0.1 * jax.random.normal(ks[9], (DEPTH, A_HEAD_DIM), f32)
    w_a2 = jax.random.normal(ks[10], (DEPTH, GATE_RANK, B_KEY_WIDTH), f32) * GATE_RANK ** -0.5
    b_a = 0.1 * jax.random.normal(ks[11], (DEPTH, B_KEY_WIDTH), f32)
    g_gla = 1.0 + 0.1 * jax.random.normal(ks[12], (DEPTH, B_HEADS, B_VAL_DIM), f32)
    w_out = jax.random.normal(ks[13], (DEPTH, D_MIX, D_MODEL), f32) * D_MIX ** -0.5
    return {'x_prompt': x_prompt, 'x_sample': x_sample, 'cache_k': cache_k, 'cache_v': cache_v,
            'state_gla': state_gla, 'page_table': page_table, 'g_pre': g_pre, 'w_in': w_in,
            'g_q': g_q, 'g_k': g_k, 'w_a2': w_a2, 'b_a': b_a, 'g_gla': g_gla, 'w_out': w_out}


def reference(x_prompt, x_sample, cache_k, cache_v, state_gla, page_table,
              g_pre, w_in, g_q, g_k, w_a2, b_a, g_gla, w_out):
    slopes = alibi_slopes(A_HEADS)
    yp, ys = x_prompt, x_sample
    bsz = x_prompt.shape[0]
    kp_l, vp_l, sp_l, ks_l, vs_l, ss_l = [], [], [], [], [], []
    for l in range(DEPTH):
        qa, ka, va, za, qb, kb, vb, zb, ga = branch_inputs(yp, g_pre[l], w_in[l], g_q[l], g_k[l], w_a2[l], b_a[l])
        oa = moba_prompt(qa, ka, va, slopes)
        s0 = jnp.zeros((bsz, B_HEADS, B_KEY_DIM, B_VAL_DIM), jnp.float32)
        ob, s_fin = gla_chunked(qb, kb, vb, ga, s0, GLA_CHUNK)
        kp_l.append(ka)
        vp_l.append(va)
        sp_l.append(s_fin.astype(yp.dtype))
        yp = merge_out(yp, oa, za, ob, zb, g_gla[l], w_out[l])
        qa, ka, va, za, qb, kb, vb, zb, ga = branch_inputs(ys, g_pre[l], w_in[l], g_q[l], g_k[l], w_a2[l], b_a[l])
        oa = moba_sample(qa, ka, va, cache_k, cache_v, l, page_table, slopes)
        ob, s_new = gla_chunked(qb, kb, vb, ga, state_gla[l], ys.shape[1])
        ks_l.append(ka)
        vs_l.append(va)
        ss_l.append(s_new.astype(state_gla.dtype))
        ys = merge_out(ys, oa, za, ob, zb, g_gla[l], w_out[l])
    k_prompt = jnp.stack(kp_l)
    v_prompt = jnp.stack(vp_l)
    s_prompt = jnp.stack(sp_l)
    k_sample = jnp.stack(ks_l)
    v_sample = jnp.stack(vs_l)
    s_sample = jnp.stack(ss_l)
    return (yp, ys, k_prompt, v_prompt, s_prompt, k_sample, v_sample, s_sample)
```

```python
import functools

import jax
import jax.numpy as jnp
import numpy as np
from jax import lax
from jax.experimental import pallas as pl
from jax.experimental.pallas import tpu as pltpu

F32 = jnp.float32
BF16 = jnp.bfloat16

D_MODEL = 1024
A_HEADS = 8
A_HEAD_DIM = 64
A_WIDTH = A_HEADS * A_HEAD_DIM
MOBA_BLOCK = 256
MOBA_TOPK = 3
B_HEADS = 4
B_KEY_DIM = 64
B_VAL_DIM = 128
B_KEY_WIDTH = B_HEADS * B_KEY_DIM
B_VAL_WIDTH = B_HEADS * B_VAL_DIM
GATE_RANK = 16
GATE_TAU = 16.0
PAGE_SIZE = 128
EPS = 1e-6
NEG = -1e30

LANE = 128
VMEM_LIMIT = 56 * 1024 * 1024

_OFF = np.cumsum([0, A_WIDTH, A_WIDTH, A_WIDTH, A_WIDTH, B_KEY_WIDTH, B_KEY_WIDTH,
                  B_VAL_WIDTH, B_VAL_WIDTH, GATE_RANK]).tolist()
_ROW_W = 5 * 512 + LANE
_AUG0 = 128
_MASK0 = 136


def _dot(a, b, dims=(((1,), (0,)), ((), ())), precision=None):
    return lax.dot_general(a, b, dims, precision=precision, preferred_element_type=F32)


_NT = (((1,), (1,)), ((), ()))
_TN = (((0,), (0,)), ((), ()))


def _silu(x):
    return x / (1.0 + jnp.exp(-x))


def _log_sigmoid(x):
    return jnp.minimum(x, 0.0) - jnp.log1p(jnp.exp(-jnp.abs(x)))


def _proj_kernel(x_ref, gpre_ref, wt_ref, wrow_ref, wa2_ref, ba_ref, gq_ref, gk_ref, *out_refs,
                 tm, sample):
    x = x_ref[...]
    ms = jnp.mean(x * x, axis=-1, keepdims=True)
    h = ((x * lax.rsqrt(ms + EPS)) * gpre_ref[...]).astype(BF16)

    pt = _dot(wt_ref[...], h, _NT)

    def head_norm(t, g):
        t3 = t.reshape(A_HEADS, A_HEAD_DIM, tm)
        ss = jnp.mean(t3 * t3, axis=1, keepdims=True)
        return (t3 * lax.rsqrt(ss + EPS)).reshape(A_WIDTH, tm) * g

    q_t = head_norm(pt[0:A_WIDTH], gq_ref[...]) * (A_HEAD_DIM ** -0.5)
    k_t = head_norm(pt[A_WIDTH:2 * A_WIDTH], gk_ref[...])
    v_t = pt[2 * A_WIDTH:3 * A_WIDTH]

    def seg(i0, i1):
        return _dot(h, wrow_ref[:, i0:i1])

    sza = _silu(seg(0, 512)).astype(BF16)
    szb = _silu(seg(512, 1024)).astype(BF16)
    qb = seg(1024, 1536) * (B_KEY_DIM ** -0.5)
    kb = seg(1536, 2048)
    vb = seg(2048, 2560)
    ab = seg(2560, 2688).astype(BF16)
    pre = _dot(ab, wa2_ref[...]) + ba_ref[...]
    lg = _log_sigmoid(pre) * (1.0 / GATE_TAU)

    if sample:
        (q_ref, k_ref, v_ref, sza_ref, szb_ref, qb_ref, kb_ref, vb_ref, lg_ref) = out_refs
        q_ref[...] = q_t.T
        k_ref[...] = k_t.T
        v_ref[...] = v_t.T
    else:
        (qt_ref, kt_ref, vt_ref, krow_ref, kmean_ref,
         sza_ref, szb_ref, qb_ref, kb_ref, vb_ref, lg_ref) = out_refs
        qt_ref[0] = q_t
        kt_ref[0] = k_t
        vt_ref[0] = v_t
        k_row = k_t.T
        krow_ref[0] = k_row.astype(BF16)
        kmean_ref[0] = jnp.mean(k_row, axis=0, keepdims=True)
    sza_ref[...] = sza
    szb_ref[...] = szb
    qb_ref[...] = qb
    kb_ref[...] = kb
    vb_ref[...] = vb
    lg_ref[...] = lg


def _proj(x2d, w, batch, sample):
    n = x2d.shape[0]
    tm = MOBA_BLOCK
    nt = n // tm
    const = lambda i: (0, 0)
    row = lambda i: (i, 0)
    in_specs = [
        pl.BlockSpec((tm, D_MODEL), row),
        pl.BlockSpec((1, D_MODEL), const),
        pl.BlockSpec((3 * A_WIDTH, D_MODEL), const),
        pl.BlockSpec((D_MODEL, _ROW_W), const),
        pl.BlockSpec((LANE, 512), const),
        pl.BlockSpec((1, 512), const),
        pl.BlockSpec((A_WIDTH, tm), const),
        pl.BlockSpec((A_WIDTH, tm), const),
    ]
    row_specs = [pl.BlockSpec((tm, 512), row)] * 6
    row_shapes = [jax.ShapeDtypeStruct((n, 512), BF16)] * 2 + [jax.ShapeDtypeStruct((n, 512), F32)] * 4
    if sample:
        out_specs = [pl.BlockSpec((tm, 512), row)] * 3 + row_specs
        out_shape = [jax.ShapeDtypeStruct((n, 512), F32)] * 3 + row_shapes
    else:
        t = n // batch
        tpb = t // tm
        feat = lambda i: (i // tpb, 0, i % tpb)
        out_specs = ([pl.BlockSpec((1, A_WIDTH, tm), feat)] * 3
                     + [pl.BlockSpec((1, tm, 512), lambda i: (i // tpb, i % tpb, 0)),
                        pl.BlockSpec((1, 1, 512), lambda i: (i, 0, 0))]
                     + row_specs)
        out_shape = ([jax.ShapeDtypeStruct((batch, A_WIDTH, t), F32)] * 3
                     + [jax.ShapeDtypeStruct((batch, t, 512), BF16),
                        jax.ShapeDtypeStruct((nt, 1, 512), F32)]
                     + row_shapes)
    return pl.pallas_call(
        functools.partial(_proj_kernel, tm=tm, sample=sample),
        grid=(nt,),
        in_specs=in_specs,
        out_specs=out_specs,
        out_shape=out_shape,
        compiler_params=pltpu.CompilerParams(dimension_semantics=("arbitrary",),
                                             vmem_limit_bytes=VMEM_LIMIT),
        name="proj_sample" if sample else "proj_prompt",
    )(x2d, w["g_pre"], w["w_t"], w["w_row"], w["w_a2p"], w["b_ap"], w["gq_t"], w["gk_t"])


def _top3_rows(g, n):
    idx = lax.broadcasted_iota(jnp.int32, g.shape, 0)
    sel = jnp.zeros(g.shape, F32)
    for _ in range(MOBA_TOPK):
        m = jnp.max(g, axis=0, keepdims=True)
        first = jnp.min(jnp.where(g == m, idx, n), axis=0, keepdims=True)
        pick = idx == first
        sel = jnp.where(pick, 1.0, sel)
        g = jnp.where(pick, -jnp.inf, g)
    return sel


def _moba_prompt_kernel(qt_ref, krow_ref, vt_ref, kmean_ref, qaug_ref, kaug_ref, o_ref,
                        qop_ref, acc_ref, m_ref, *, nb):
    blk = MOBA_BLOCK
    i = pl.program_id(2)
    qt = qt_ref[0]
    row128 = lax.broadcasted_iota(jnp.int32, (2 * A_HEAD_DIM, blk), 0)
    lane128 = lax.broadcasted_iota(jnp.int32, (nb, LANE), 1)
    bidx = lax.broadcasted_iota(jnp.int32, (nb, blk), 0)
    kmean = kmean_ref[0]

    for hh in range(2):
        in_head = (row128 >= hh * A_HEAD_DIM) & (row128 < (hh + 1) * A_HEAD_DIM)
        q_h = jnp.where(in_head, qt, 0.0)
        km_h = jnp.where((lane128 >= hh * A_HEAD_DIM) & (lane128 < (hh + 1) * A_HEAD_DIM), kmean, 0.0)
        gate = _dot(km_h, qt, precision=lax.Precision.HIGHEST)
        past = bidx < i
        sel = _top3_rows(jnp.where(past, gate, NEG), nb)
        keep = ((sel > 0.5) & past) | (bidx == i)
        maskbias = jnp.where(keep, 0.0, NEG)
        qop = jnp.concatenate(
            [q_h, qaug_ref[0, hh], maskbias,
             jnp.zeros((2 * LANE - _MASK0 - nb, blk), F32)], axis=0)
        qop_ref[hh] = qop.astype(BF16)
        acc_ref[hh] = jnp.zeros(acc_ref.shape[1:], F32)
        m_ref[hh] = jnp.full(m_ref.shape[1:], -jnp.inf, F32)

    lane_k = lax.broadcasted_iota(jnp.int32, (blk, LANE), 1)
    kaug_base = kaug_ref[...]
    ones_rows = jnp.ones((8, blk), BF16)
    kk = lax.broadcasted_iota(jnp.int32, (blk, blk), 0)
    qq = lax.broadcasted_iota(jnp.int32, (blk, blk), 1)

    def tile(j, causal):
        off = pl.multiple_of(j * blk, blk)
        shift = ((j - i) * blk).astype(F32)
        kaug = jnp.where(lane_k == 2, shift.astype(BF16),
                         jnp.where(lane_k == 8 + j, jnp.ones((), BF16), kaug_base))
        kop = jnp.concatenate([krow_ref[0, pl.ds(off, blk), :], kaug], axis=1)
        for hh in range(2):
            s = _dot(kop, qop_ref[hh])
            if causal:
                s = jnp.where(kk <= qq, s, NEG)
            m_prev = m_ref[hh][0:1]
            m_new = jnp.maximum(m_prev, jnp.max(s, axis=0, keepdims=True))
            alpha = jnp.exp(m_prev - m_new)
            p = jnp.exp(s - m_new).astype(BF16)
            v_t = vt_ref[0, hh * A_HEAD_DIM:(hh + 1) * A_HEAD_DIM, pl.ds(off, blk)].astype(BF16)
            vop = jnp.concatenate([v_t, ones_rows], axis=0)
            acc_ref[hh] = alpha * acc_ref[hh] + _dot(vop, p)
            m_ref[hh] = jnp.broadcast_to(m_new, m_ref.shape[1:])

    tile(i, True)

    def body(j, carry):
        tile(j, False)
        return carry

    lax.fori_loop(0, i, body, 0)

    outs = []
    for hh in range(2):
        acc = acc_ref[hh]
        outs.append(acc[0:A_HEAD_DIM] / acc[A_HEAD_DIM:A_HEAD_DIM + 1])
    o_ref[0] = jnp.concatenate(outs, axis=0).T.astype(o_ref.dtype)


def _moba_prompt(qt, krow, vt, kmean, w):
    batch, _, t = qt.shape
    nb = t // MOBA_BLOCK
    blk = MOBA_BLOCK
    return pl.pallas_call(
        functools.partial(_moba_prompt_kernel, nb=nb),
        grid=(batch, A_HEADS // 2, nb),
        in_specs=[
            pl.BlockSpec((1, 2 * A_HEAD_DIM, blk), lambda b, hp, i: (b, hp, i)),
            pl.BlockSpec((1, t, LANE), lambda b, hp, i: (b, 0, hp)),
            pl.BlockSpec((1, 2 * A_HEAD_DIM, t), lambda b, hp, i: (b, hp, 0)),
            pl.BlockSpec((1, nb, LANE), lambda b, hp, i: (b, 0, hp)),
            pl.BlockSpec((1, 2, 8, blk), lambda b, hp, i: (hp, 0, 0, 0)),
            pl.BlockSpec((blk, LANE), lambda b, hp, i: (0, 0)),
        ],
        out_specs=pl.BlockSpec((1, blk, LANE), lambda b, hp, i: (b, i, hp)),
        out_shape=jax.ShapeDtypeStruct((batch, t, A_WIDTH), BF16),
        scratch_shapes=[
            pltpu.VMEM((2, 2 * LANE, blk), BF16),
            pltpu.VMEM((2, A_HEAD_DIM + 8, blk), F32),
            pltpu.VMEM((2, 8, blk), F32),
        ],
        compiler_params=pltpu.CompilerParams(
            dimension_semantics=("arbitrary", "arbitrary", "arbitrary"),
            vmem_limit_bytes=VMEM_LIMIT),
        name="moba_prompt",
    )(qt, krow, vt, kmean.reshape(batch, nb, A_WIDTH), w["qaug"], w["kaug"])


def _moba_sample_kernel(pt_ref, q_ref, kn_ref, vn_ref, slope_ref, ck_hbm, cv_hbm, o_ref,
                        kbuf, sall, gate_ref, idx_v, idx_s, vbuf, oacc, sem_k, sem_v, sem_i,
                        *, n_pages, npg):
    b = pl.program_id(0)
    n_chunks = n_pages // npg
    cw = npg * PAGE_SIZE
    bpc = cw // MOBA_BLOCK
    past = n_pages * PAGE_SIZE
    n_blocks = past // MOBA_BLOCK
    t_new = q_ref.shape[2]
    rows = A_HEADS * t_new

    def k_copy(c, p, slot):
        page = pt_ref[b, c * npg + p]
        return pltpu.make_async_copy(
            ck_hbm.at[0, page], kbuf.at[slot, :, :, pl.ds(p * PAGE_SIZE, PAGE_SIZE)], sem_k.at[slot])

    def start_chunk(c, slot):
        for p in range(npg):
            k_copy(c, p, slot).start()

    def wait_chunk(c, slot):
        for p in range(npg):
            k_copy(c, p, slot).wait()

    q_ops = []
    for h in range(A_HEADS):
        qh = q_ref[0, h]
        hi = qh.astype(BF16)
        lo = (qh - hi.astype(F32)).astype(BF16)
        q_ops.append(jnp.concatenate([hi, lo], axis=0))

    gate_ref[...] = jnp.zeros(gate_ref.shape, F32)
    lane_g = lax.broadcasted_iota(jnp.int32, (t_new, LANE), 1)
    start_chunk(0, 0)

    def chunk_body(c, carry):
        slot = c % 2

        @pl.when(c + 1 < n_chunks)
        def _():
            start_chunk(c + 1, 1 - slot)

        wait_chunk(c, slot)
        off = pl.multiple_of(c * cw, cw)
        for h in range(A_HEADS):
            kt = kbuf[slot, h].astype(BF16)
            s2 = _dot(q_ops[h], kt)
            s = s2[0:t_new] + s2[t_new:2 * t_new]
            sall[h * t_new:(h + 1) * t_new, pl.ds(off, cw)] = s
            g = gate_ref[h * t_new:(h + 1) * t_new, :]
            for jb in range(bpc):
                t2 = (s[:, jb * MOBA_BLOCK:jb * MOBA_BLOCK + LANE]
                      + s[:, jb * MOBA_BLOCK + LANE:(jb + 1) * MOBA_BLOCK])
                r = jnp.sum(t2, axis=1, keepdims=True)
                g = jnp.where(lane_g == c * bpc + jb, r, g)
            gate_ref[h * t_new:(h + 1) * t_new, :] = g
        return carry

    lax.fori_loop(0, n_chunks, chunk_body, 0)

    lane_r = lax.broadcasted_iota(jnp.int32, (rows, LANE), 1)
    g = jnp.where(lane_r < n_blocks, gate_ref[...], -jnp.inf)
    sel = jnp.zeros((rows, LANE), F32)
    idx = jnp.zeros((rows, LANE), jnp.int32)
    for r in range(MOBA_TOPK):
        m = jnp.max(g, axis=1, keepdims=True)
        first = jnp.min(jnp.where(g == m, lane_r, LANE), axis=1, keepdims=True)
        pick = lane_r == first
        sel = jnp.where(pick, 1.0, sel)
        g = jnp.where(pick, -jnp.inf, g)
        idx = jnp.where(lane_r == r, first, idx)
    idx_v[...] = idx
    cp_i = pltpu.make_async_copy(idx_v, idx_s, sem_i)
    cp_i.start()
    cp_i.wait()

    def v_copy(row, r, pg):
        bsel = idx_s[row, r]
        page = pt_ref[b, 2 * bsel + pg]
        return pltpu.make_async_copy(
            cv_hbm.at[0, page, row // t_new],
            vbuf.at[row * MOBA_TOPK + r, :, pl.ds(pg * PAGE_SIZE, PAGE_SIZE)], sem_v)

    def v_start(row, carry):
        for r in range(MOBA_TOPK):
            for pg in range(2):
                v_copy(row, r, pg).start()
        return carry

    lax.fori_loop(0, rows, v_start, 0)

    slope = slope_ref[...][:, 0:1]
    trow = lax.broadcasted_iota(jnp.int32, (rows, 1), 0) % t_new
    qpos = (past + trow).astype(F32)
    sel_b = sel.astype(BF16)
    erow = lax.broadcasted_iota(jnp.int32, (LANE, cw), 0)
    elane = lax.broadcasted_iota(jnp.int32, (LANE, cw), 1)
    klane = lax.broadcasted_iota(jnp.int32, (rows, cw), 1)

    def bias_body(c, m):
        off = pl.multiple_of(c * cw, cw)
        expand = jnp.where(erow == c * bpc + elane // MOBA_BLOCK, 1.0, 0.0).astype(BF16)
        keep = _dot(sel_b, expand)
        kpos = (klane + c * cw).astype(F32)
        s = sall[:, pl.ds(off, cw)] - slope * (qpos - kpos)
        s = jnp.where(keep > 0.5, s, NEG)
        sall[:, pl.ds(off, cw)] = s
        return jnp.maximum(m, jnp.max(s, axis=1, keepdims=True))

    m = lax.fori_loop(0, n_chunks, bias_body, jnp.full((rows, 1), -jnp.inf, F32))

    s_new = []
    for h in range(A_HEADS):
        s_new.append(_dot(q_ref[0, h].astype(BF16), kn_ref[0, h].astype(BF16), _NT))
    s_new = jnp.concatenate(s_new, axis=0)
    snew_col = lax.broadcasted_iota(jnp.int32, (rows, t_new), 1)
    dist_new = trow - snew_col
    s_new = s_new - slope * dist_new.astype(F32)
    s_new = jnp.where(dist_new >= 0, s_new, NEG)
    m = jnp.maximum(m, jnp.max(s_new, axis=1, keepdims=True))
    p_new = jnp.exp(s_new - m)
    l0 = jnp.sum(p_new, axis=1, keepdims=True)

    def exp_body(c, l):
        off = pl.multiple_of(c * cw, cw)
        p = jnp.exp(sall[:, pl.ds(off, cw)] - m)
        sall[:, pl.ds(off, cw)] = p
        return l + jnp.sum(p, axis=1, keepdims=True)

    l = lax.fori_loop(0, n_chunks, exp_body, l0)

    for h in range(A_HEADS):
        oacc[h] = _dot(p_new[h * t_new:(h + 1) * t_new].astype(BF16), vn_ref[0, h].astype(BF16))

    def v_wait(row, carry):
        for r in range(MOBA_TOPK):
            for pg in range(2):
                v_copy(row, r, pg).wait()
        return carry

    lax.fori_loop(0, rows, v_wait, 0)

    sub = lax.broadcasted_iota(jnp.int32, (t_new, MOBA_BLOCK), 0)

    def pv_body(row, carry):
        h = row // t_new
        t = row % t_new
        r0 = pl.multiple_of(h * t_new, t_new)
        acc = oacc[h]
        for r in range(MOBA_TOPK):
            bsel = idx_s[row, r]
            off = pl.multiple_of(bsel * MOBA_BLOCK, MOBA_BLOCK)
            p = sall[pl.ds(r0, t_new), pl.ds(off, MOBA_BLOCK)]
            p = jnp.where(sub == t, p, 0.0).astype(BF16)
            acc = acc + _dot(p, vbuf[row * MOBA_TOPK + r].astype(BF16), _NT)
        oacc[h] = acc
        return carry

    lax.fori_loop(0, rows, pv_body, 0)

    for h in range(A_HEADS):
        o_ref[0, h] = oacc[h] / l[h * t_new:(h + 1) * t_new]


def _moba_sample(q4, kn4, vn4, cache_kt, cache_vt, page_table, slope_rows):
    nbatch, _, t_new, _ = q4.shape
    n_pages = page_table.shape[1]
    npg = 16
    rows = A_HEADS * t_new
    past = n_pages * PAGE_SIZE
    blk4 = pl.BlockSpec((1, A_HEADS, t_new, A_HEAD_DIM), lambda b, pt: (b, 0, 0, 0))
    return pl.pallas_call(
        functools.partial(_moba_sample_kernel, n_pages=n_pages, npg=npg),
        grid_spec=pltpu.PrefetchScalarGridSpec(
            num_scalar_prefetch=1,
            grid=(nbatch,),
            in_specs=[blk4, blk4, blk4,
                      pl.BlockSpec((rows, LANE), lambda b, pt: (0, 0)),
                      pl.BlockSpec(memory_space=pl.ANY),
                      pl.BlockSpec(memory_space=pl.ANY)],
            out_specs=blk4,
            scratch_shapes=[
                pltpu.VMEM((2, A_HEADS, A_HEAD_DIM, npg * PAGE_SIZE), F32),
                pltpu.VMEM((rows, past), F32),
                pltpu.VMEM((rows, LANE), F32),
                pltpu.VMEM((rows, LANE), jnp.int32),
                pltpu.SMEM((rows, LANE), jnp.int32),
                pltpu.VMEM((rows * MOBA_TOPK, A_HEAD_DIM, MOBA_BLOCK), F32),
                pltpu.VMEM((A_HEADS, t_new, A_HEAD_DIM), F32),
                pltpu.SemaphoreType.DMA((2,)),
                pltpu.SemaphoreType.DMA,
                pltpu.SemaphoreType.DMA,
            ]),
        out_shape=jax.ShapeDtypeStruct((nbatch, A_HEADS, t_new, A_HEAD_DIM), F32),
        compiler_params=pltpu.CompilerParams(dimension_semantics=("arbitrary",),
                                             vmem_limit_bytes=VMEM_LIMIT),
        name="moba_sample",
    )(page_table, q4, kn4, vn4, slope_rows, cache_kt, cache_vt)


def _gla_kernel(q_ref, k_ref, g_ref, v_ref, s0_ref, o_ref, sfin_ref, state, *, chunk, n_chunks):
    c = chunk
    mx = BF16 if c >= 16 else F32

    @pl.when(pl.program_id(2) == 0)
    def _():
        state[...] = jnp.concatenate([s0_ref[0, 0], jnp.zeros((LANE - B_KEY_DIM, B_VAL_DIM), F32)],
                                     axis=0)

    ri = lax.broadcasted_iota(jnp.int32, (c, LANE), 0)
    ti = lax.broadcasted_iota(jnp.int32, (c, c), 0)
    si = lax.broadcasted_iota(jnp.int32, (c, c), 1)
    tril = jnp.where(si <= ti, 1.0, 0.0).astype(mx)
    ones = jnp.ones((c, LANE), mx)
    levels = int(np.log2(c))

    for ci in range(n_chunks):
        rows = pl.ds(ci * c, c)
        q = q_ref[0, rows, :]
        k = k_ref[0, rows, :]
        g = g_ref[0, rows, :]
        v = v_ref[0, rows, :].astype(mx)
        g_hi = g.astype(BF16)
        g_lo = (g - g_hi.astype(F32)).astype(BF16)
        g_hi, g_lo = g_hi.astype(mx), g_lo.astype(mx)
        b = _dot(tril, g_hi) + _dot(tril, g_lo)
        b_tot = _dot(g_hi, ones, _TN) + _dot(g_lo, ones, _TN)
        b_last = b[c - 1:c, :]

        attn = jnp.where(ti == si, _dot(q.astype(mx), k.astype(mx), _NT), 0.0)
        for lv in range(levels):
            half = c >> (lv + 1)
            width = 2 * half
            pos = ri & (width - 1)
            second = pos >= half
            if half >= 8:
                ref = jnp.concatenate(
                    [jnp.broadcast_to(b[m * width + half - 1:m * width + half, :], (width, LANE))
                     for m in range(c // width)], axis=0)
            else:
                ref = jnp.zeros((c, LANE), F32)
                for p in range(width):
                    sh = (p - (half - 1)) % c
                    rolled = b if sh == 0 else pltpu.roll(b, sh, 0)
                    ref = jnp.where(pos == p, rolled, ref)
            x = jnp.where(second, b - ref, ref - b)
            z = (jnp.where(second, q, k) * jnp.exp(x)).astype(mx)
            gram = _dot(z, z, _NT)
            sb = int(np.log2(width))
            pair = ((ti >> sb) == (si >> sb)) & ((ti & (width - 1)) >= half) & ((si & (width - 1)) < half)
            attn = attn + jnp.where(pair, gram, 0.0)

        s_prev = state[...]
        qd = (q * jnp.exp(b)).astype(mx)
        o = _dot(attn.astype(mx), v) + _dot(qd, s_prev.astype(mx))
        o_ref[0, rows, :] = o
        kd = (k * jnp.exp(b_last - b)).astype(mx)
        state[...] = jnp.exp(b_tot) * s_prev + _dot(kd, v, _TN)

    @pl.when(pl.program_id(2) == pl.num_programs(2) - 1)
    def _():
        sfin_ref[0, 0] = state[0:B_KEY_DIM, :]


def _gla(qb, kb, lg, vb, s0, chunk):
    batch, t, _ = qb.shape
    tc = min(t, 512)
    n_chunks = tc // chunk
    seq = pl.BlockSpec((1, tc, LANE), lambda b, h, c: (b, c, h))
    st = pl.BlockSpec((1, 1, B_KEY_DIM, B_VAL_DIM), lambda b, h, c: (b, h, 0, 0))
    return pl.pallas_call(
        functools.partial(_gla_kernel, chunk=chunk, n_chunks=n_chunks),
        grid=(batch, B_HEADS, t // tc),
        in_specs=[seq, seq, seq, seq, st],
        out_specs=[seq, st],
        out_shape=[jax.ShapeDtypeStruct((batch, t, B_VAL_WIDTH), F32),
                   jax.ShapeDtypeStruct((batch, B_HEADS, B_KEY_DIM, B_VAL_DIM), F32)],
        scratch_shapes=[pltpu.VMEM((LANE, B_VAL_DIM), F32)],
        compiler_params=pltpu.CompilerParams(
            dimension_semantics=("arbitrary", "arbitrary", "arbitrary"),
            vmem_limit_bytes=VMEM_LIMIT),
        name="gla_c%d" % chunk,
    )(qb, kb, lg, vb, s0)


def _merge_kernel(x_ref, oa_ref, sza_ref, ob_ref, szb_ref, gg_ref, wo_ref, y_ref):
    ya = oa_ref[...].astype(F32) * sza_ref[...].astype(F32)
    ob = ob_ref[...]
    parts = []
    for h in range(B_HEADS):
        oh = ob[:, h * B_VAL_DIM:(h + 1) * B_VAL_DIM]
        ms = jnp.mean(oh * oh, axis=-1, keepdims=True)
        parts.append(oh * lax.rsqrt(ms + EPS))
    yb = (jnp.concatenate(parts, axis=1) * gg_ref[...]) * szb_ref[...].astype(F32)
    cat = jnp.concatenate([ya, yb], axis=1).astype(BF16)
    y_ref[...] = x_ref[...] + _dot(cat, wo_ref[...])


def _merge(x2d, oa, sza, ob, szb, w):
    n = x2d.shape[0]
    tm = min(n, 512)
    row = lambda i: (i, 0)
    const = lambda i: (0, 0)
    return pl.pallas_call(
        _merge_kernel,
        grid=(n // tm,),
        in_specs=[pl.BlockSpec((tm, D_MODEL), row), pl.BlockSpec((tm, 512), row),
                  pl.BlockSpec((tm, 512), row), pl.BlockSpec((tm, 512), row),
                  pl.BlockSpec((tm, 512), row), pl.BlockSpec((1, 512), const),
                  pl.BlockSpec((D_MODEL, D_MODEL), const)],
        out_specs=pl.BlockSpec((tm, D_MODEL), row),
        out_shape=jax.ShapeDtypeStruct((n, D_MODEL), F32),
        compiler_params=pltpu.CompilerParams(dimension_semantics=("arbitrary",),
                                             vmem_limit_bytes=VMEM_LIMIT),
        name="merge_out",
    )(x2d, oa, sza, ob, szb, w["g_gla"], w["w_out"])


def _pad_heads(a, n_heads, dim):
    lead = a.shape[:-1]
    a = a.reshape(lead + (n_heads, dim))
    a = jnp.pad(a, [(0, 0)] * len(lead) + [(0, 0), (0, LANE - dim)])
    return a.reshape(lead + (n_heads * LANE,))


def _layer_weights(g_pre, w_in, g_q, g_k, w_a2, b_a, g_gla, w_out):
    o = _OFF
    w_t = w_in[:, o[0]:o[3]].T.astype(BF16)
    w_row = jnp.concatenate([
        w_in[:, o[3]:o[4]], w_in[:, o[7]:o[8]],
        _pad_heads(w_in[:, o[4]:o[5]], B_HEADS, B_KEY_DIM),
        _pad_heads(w_in[:, o[5]:o[6]], B_HEADS, B_KEY_DIM),
        w_in[:, o[6]:o[7]],
        jnp.pad(w_in[:, o[8]:o[9]], ((0, 0), (0, LANE - GATE_RANK)))], axis=1).astype(BF16)
    w_a2p = jnp.pad(_pad_heads(w_a2, B_HEADS, B_KEY_DIM), ((0, LANE - GATE_RANK), (0, 0))).astype(BF16)
    b_ap = _pad_heads(b_a, B_HEADS, B_KEY_DIM).reshape(1, 512)
    gq_t = jnp.broadcast_to(jnp.tile(g_q, A_HEADS)[:, None], (A_WIDTH, MOBA_BLOCK))
    gk_t = jnp.broadcast_to(jnp.tile(g_k, A_HEADS)[:, None], (A_WIDTH, MOBA_BLOCK))
    slopes = jnp.asarray([2.0 ** (-8.0 * (h + 1) / A_HEADS) for h in range(A_HEADS)], F32)
    qq = jnp.arange(MOBA_BLOCK, dtype=F32)
    qaug = jnp.zeros((A_HEADS, 8, MOBA_BLOCK), F32)
    qaug = qaug.at[:, 0, :].set(slopes[:, None])
    qaug = qaug.at[:, 1, :].set(-slopes[:, None] * qq[None, :])
    qaug = qaug.at[:, 2, :].set(slopes[:, None])
    qaug = qaug.reshape(A_HEADS // 2, 2, 8, MOBA_BLOCK)
    kaug = jnp.zeros((MOBA_BLOCK, LANE), F32)
    kaug = kaug.at[:, 0].set(qq).at[:, 1].set(1.0).astype(BF16)
    return {
        "g_pre": g_pre.reshape(1, D_MODEL), "w_t": w_t, "w_row": w_row, "w_a2p": w_a2p, "b_ap": b_ap,
        "gq_t": gq_t, "gk_t": gk_t, "g_gla": g_gla.reshape(1, B_VAL_WIDTH),
        "w_out": w_out.astype(BF16), "qaug": qaug, "kaug": kaug, "slopes": slopes,
    }


def kernel(x_prompt, x_sample, cache_k, cache_v, state_gla, page_table, g_pre, w_in, g_q, g_k, w_a2, b_a, g_gla, w_out):
    depth = g_pre.shape[0]
    batch, t, _ = x_prompt.shape
    nb_s, t_new, _ = x_sample.shape
    cache_kt = jnp.transpose(cache_k, (0, 1, 3, 4, 2))
    cache_vt = jnp.transpose(cache_v, (0, 1, 3, 4, 2))
    yp = x_prompt.reshape(batch * t, D_MODEL)
    ys = x_sample.reshape(nb_s * t_new, D_MODEL)
    kp_l, vp_l, sp_l, ks_l, vs_l, ss_l = [], [], [], [], [], []
    for l in range(depth):
        w = _layer_weights(g_pre[l], w_in[l], g_q[l], g_k[l], w_a2[l], b_a[l], g_gla[l], w_out[l])
        qt, kt, vt, krow, kmean, sza, szb, qb, kb, vb, lg = _proj(yp, w, batch, sample=False)
        oa = _moba_prompt(qt, krow, vt, kmean, w)
        r3 = lambda a: a.reshape(batch, t, 512)
        s0 = jnp.zeros((batch, B_HEADS, B_KEY_DIM, B_VAL_DIM), F32)
        ob, s_fin = _gla(r3(qb), r3(kb), r3(lg), r3(vb), s0, chunk=64)
        yp = _merge(yp, oa.reshape(batch * t, A_WIDTH), sza, ob.reshape(batch * t, 512), szb, w)
        kp_l.append(jnp.transpose(kt.reshape(batch, A_HEADS, A_HEAD_DIM, t), (0, 3, 1, 2)))
        vp_l.append(jnp.transpose(vt.reshape(batch, A_HEADS, A_HEAD_DIM, t), (0, 3, 1, 2)))
        sp_l.append(s_fin)
        qs, ks, vs, sza, szb, qb, kb, vb, lg = _proj(ys, w, nb_s, sample=True)
        h4 = lambda a: jnp.transpose(a.reshape(nb_s, t_new, A_HEADS, A_HEAD_DIM), (0, 2, 1, 3))
        slope_rows = jnp.broadcast_to(jnp.repeat(w["slopes"], t_new)[:, None], (A_HEADS * t_new, LANE))
        oa4 = _moba_sample(h4(qs), h4(ks), h4(vs), cache_kt[l:l + 1], cache_vt[l:l + 1],
                           page_table, slope_rows)
        oa = jnp.transpose(oa4, (0, 2, 1, 3)).reshape(nb_s * t_new, A_WIDTH)
        r3 = lambda a: a.reshape(nb_s, t_new, 512)
        ob, s_new = _gla(r3(qb), r3(kb), r3(lg), r3(vb), state_gla[l], chunk=t_new)
        ys = _merge(ys, oa, sza, ob.reshape(nb_s * t_new, 512), szb, w)
        ks_l.append(ks.reshape(nb_s, t_new, A_HEADS, A_HEAD_DIM))
        vs_l.append(vs.reshape(nb_s, t_new, A_HEADS, A_HEAD_DIM))
        ss_l.append(s_new)
    return (yp.reshape(batch, t, D_MODEL), ys.reshape(nb_s, t_new, D_MODEL),
            jnp.stack(kp_l), jnp.stack(vp_l), jnp.stack(sp_l),
            jnp.stack(ks_l), jnp.stack(vs_l), jnp.stack(ss_l))
```

```python
import functools

import jax
import jax.numpy as jnp
import numpy as np
from jax import lax
from jax.experimental import pallas as pl
from jax.experimental.pallas import tpu as pltpu

F32 = jnp.float32
BF16 = jnp.bfloat16

D_MODEL = 1024
A_HEADS = 8
A_HEAD_DIM = 64
A_WIDTH = A_HEADS * A_HEAD_DIM
MOBA_BLOCK = 256
MOBA_TOPK = 3
B_HEADS = 4
B_KEY_DIM = 64
B_VAL_DIM = 128
B_KEY_WIDTH = B_HEADS * B_KEY_DIM
B_VAL_WIDTH = B_HEADS * B_VAL_DIM
GATE_RANK = 16
GATE_TAU = 16.0
PAGE_SIZE = 128
EPS = 1e-6
NEG = -1e30

LANE = 128
VMEM_LIMIT = 56 * 1024 * 1024

_OFF = np.cumsum([0, A_WIDTH, A_WIDTH, A_WIDTH, A_WIDTH, B_KEY_WIDTH, B_KEY_WIDTH,
                  B_VAL_WIDTH, B_VAL_WIDTH, GATE_RANK]).tolist()
_ROW_W = 5 * 512 + LANE
_AUG0 = 128
_MASK0 = 136


def _dot(a, b, dims=(((1,), (0,)), ((), ())), precision=None):
    return lax.dot_general(a, b, dims, precision=precision, preferred_element_type=F32)


_NT = (((1,), (1,)), ((), ()))
_TN = (((0,), (0,)), ((), ()))


def _silu(x):
    return x / (1.0 + jnp.exp(-x))


def _log_sigmoid(x):
    return jnp.minimum(x, 0.0) - jnp.log1p(jnp.exp(-jnp.abs(x)))


def _proj_kernel(x_ref, gpre_ref, wt_ref, wrow_ref, wa2_ref, ba_ref, gq_ref, gk_ref, *out_refs,
                 tm, sample):
    x = x_ref[...]
    ms = jnp.mean(x * x, axis=-1, keepdims=True)
    h = ((x * lax.rsqrt(ms + EPS)) * gpre_ref[...]).astype(BF16)

    pt = _dot(wt_ref[...], h, _NT)

    def head_norm(t, g):
        t3 = t.reshape(A_HEADS, A_HEAD_DIM, tm)
        ss = jnp.mean(t3 * t3, axis=1, keepdims=True)
        return (t3 * lax.rsqrt(ss + EPS)).reshape(A_WIDTH, tm) * g

    q_t = head_norm(pt[0:A_WIDTH], gq_ref[...]) * (A_HEAD_DIM ** -0.5)
    k_t = head_norm(pt[A_WIDTH:2 * A_WIDTH], gk_ref[...])
    v_t = pt[2 * A_WIDTH:3 * A_WIDTH]

    def seg(i0, i1):
        return _dot(h, wrow_ref[:, i0:i1])

    sza = _silu(seg(0, 512)).astype(BF16)
    szb = _silu(seg(512, 1024)).astype(BF16)
    qb = seg(1024, 1536) * (B_KEY_DIM ** -0.5)
    kb = seg(1536, 2048)
    vb = seg(2048, 2560)
    ab = seg(2560, 2688).astype(BF16)
    pre = _dot(ab, wa2_ref[...]) + ba_ref[...]
    lg = _log_sigmoid(pre) * (1.0 / GATE_TAU)

    if sample:
        (q_ref, k_ref, v_ref, sza_ref, szb_ref, qb_ref, kb_ref, vb_ref, lg_ref) = out_refs
        q_ref[...] = q_t.T
        k_ref[...] = k_t.T
        v_ref[...] = v_t.T
    else:
        (qt_ref, kt_ref, vt_ref, krow_ref, kmean_ref,
         sza_ref, szb_ref, qb_ref, kb_ref, vb_ref, lg_ref) = out_refs
        qt_ref[0] = q_t
        kt_ref[0] = k_t
        vt_ref[0] = v_t
        k_row = k_t.T
        krow_ref[0] = k_row.astype(BF16)
        kmean_ref[0] = jnp.mean(k_row, axis=0, keepdims=True)
    sza_ref[...] = sza
    szb_ref[...] = szb
    qb_ref[...] = qb
    kb_ref[...] = kb
    vb_ref[...] = vb
    lg_ref[...] = lg


def _proj(x2d, w, batch, sample):
    n = x2d.shape[0]
    tm = MOBA_BLOCK
    nt = n // tm
    const = lambda i: (0, 0)
    row = lambda i: (i, 0)
    in_specs = [
        pl.BlockSpec((tm, D_MODEL), row),
        pl.BlockSpec((1, D_MODEL), const),
        pl.BlockSpec((3 * A_WIDTH, D_MODEL), const),
        pl.BlockSpec((D_MODEL, _ROW_W), const),
        pl.BlockSpec((LANE, 512), const),
        pl.BlockSpec((1, 512), const),
        pl.BlockSpec((A_WIDTH, tm), const),
        pl.BlockSpec((A_WIDTH, tm), const),
    ]
    row_specs = [pl.BlockSpec((tm, 512), row)] * 6
    row_shapes = [jax.ShapeDtypeStruct((n, 512), BF16)] * 2 + [jax.ShapeDtypeStruct((n, 512), F32)] * 4
    if sample:
        out_specs = [pl.BlockSpec((tm, 512), row)] * 3 + row_specs
        out_shape = [jax.ShapeDtypeStruct((n, 512), F32)] * 3 + row_shapes
    else:
        t = n // batch
        tpb = t // tm
        feat = lambda i: (i // tpb, 0, i % tpb)
        out_specs = ([pl.BlockSpec((1, A_WIDTH, tm), feat)] * 3
                     + [pl.BlockSpec((1, tm, 512), lambda i: (i // tpb, i % tpb, 0)),
                        pl.BlockSpec((1, 1, 512), lambda i: (i, 0, 0))]
                     + row_specs)
        out_shape = ([jax.ShapeDtypeStruct((batch, A_WIDTH, t), F32)] * 3
                     + [jax.ShapeDtypeStruct((batch, t, 512), BF16),
                        jax.ShapeDtypeStruct((nt, 1, 512), F32)]
                     + row_shapes)
    return pl.pallas_call(
        functools.partial(_proj_kernel, tm=tm, sample=sample),
        grid=(nt,),
        in_specs=in_specs,
        out_specs=out_specs,
        out_shape=out_shape,
        compiler_params=pltpu.CompilerParams(dimension_semantics=("arbitrary",),
                                             vmem_limit_bytes=VMEM_LIMIT),
        name="proj_sample" if sample else "proj_prompt",
    )(x2d, w["g_pre"], w["w_t"], w["w_row"], w["w_a2p"], w["b_ap"], w["gq_t"], w["gk_t"])


def _top3_rows(g, n):
    idx = lax.broadcasted_iota(jnp.int32, g.shape, 0)
    sel = jnp.zeros(g.shape, F32)
    for _ in range(MOBA_TOPK):
        m = jnp.max(g, axis=0, keepdims=True)
        first = jnp.min(jnp.where(g == m, idx, n), axis=0, keepdims=True)
        pick = idx == first
        sel = jnp.where(pick, 1.0, sel)
        g = jnp.where(pick, -jnp.inf, g)
    return sel


_QB = 2


def _moba_prompt_kernel(qt_ref, krow_ref, vt_ref, kmean_ref, qaug_ref, kaug_ref, o_ref,
                        qop_ref, acc_ref, m_ref, s_ref, *, nb):
    blk = MOBA_BLOCK
    qw = _QB * blk
    i0 = pl.program_id(2) * _QB
    qt = qt_ref[0]
    row128 = lax.broadcasted_iota(jnp.int32, (2 * A_HEAD_DIM, qw), 0)
    lane128 = lax.broadcasted_iota(jnp.int32, (nb, LANE), 1)
    bidx = lax.broadcasted_iota(jnp.int32, (nb, qw), 0)
    iq = i0 + lax.broadcasted_iota(jnp.int32, (nb, qw), 1) // blk
    kmean = kmean_ref[0]

    for hh in range(2):
        in_head = (row128 >= hh * A_HEAD_DIM) & (row128 < (hh + 1) * A_HEAD_DIM)
        q_h = jnp.where(in_head, qt, 0.0)
        km_h = jnp.where((lane128 >= hh * A_HEAD_DIM) & (lane128 < (hh + 1) * A_HEAD_DIM), kmean, 0.0)
        gate = _dot(km_h, qt, precision=lax.Precision.HIGHEST)
        past = bidx < iq
        sel = _top3_rows(jnp.where(past, gate, NEG), nb)
        keep = ((sel > 0.5) & past) | (bidx == iq)
        maskbias = jnp.where(keep, 0.0, NEG)
        for qb in range(_QB):
            c = hh * _QB + qb
            cols = slice(qb * blk, (qb + 1) * blk)
            qop = jnp.concatenate(
                [q_h[:, cols], qaug_ref[0, c], maskbias[:, cols],
                 jnp.zeros((2 * LANE - _MASK0 - nb, blk), F32)], axis=0)
            qop_ref[c] = qop.astype(BF16)
            acc_ref[c] = jnp.zeros(acc_ref.shape[1:], F32)
            m_ref[c] = jnp.full(m_ref.shape[1:], -jnp.inf, F32)

    lane_k = lax.broadcasted_iota(jnp.int32, (blk, LANE), 1)
    kaug_base = kaug_ref[...]
    ones_rows = jnp.ones((8, blk), BF16)
    kk = lax.broadcasted_iota(jnp.int32, (blk, blk), 0)
    qq = lax.broadcasted_iota(jnp.int32, (blk, blk), 1)

    def scores(j, slot, chains):
        off = pl.multiple_of(j * blk, blk)
        shift = ((j - i0) * blk).astype(F32)
        kaug = jnp.where(lane_k == 2, shift.astype(BF16),
                         jnp.where(lane_k == 8 + j, jnp.ones((), BF16), kaug_base))
        kop = jnp.concatenate([krow_ref[0, pl.ds(off, blk), :], kaug], axis=1)
        for c in chains:
            s_ref[slot, c] = _dot(kop, qop_ref[c])

    def absorb(j, slot, chains):
        off = pl.multiple_of(j * blk, blk)
        vops = []
        for hh in range(2):
            v_t = vt_ref[0, hh * A_HEAD_DIM:(hh + 1) * A_HEAD_DIM, pl.ds(off, blk)].astype(BF16)
            vops.append(jnp.concatenate([v_t, ones_rows], axis=0))
        for c, causal in chains:
            s = s_ref[slot, c]
            if causal:
                s = jnp.where(kk <= qq, s, NEG)
            m_prev = m_ref[c][0:1]
            m_new = jnp.maximum(m_prev, jnp.max(s, axis=0, keepdims=True))
            alpha = jnp.exp(m_prev - m_new)
            p = jnp.exp(s - m_new).astype(BF16)
            acc_ref[c] = alpha * acc_ref[c] + _dot(vops[c // _QB], p)
            m_ref[c] = jnp.broadcast_to(m_new, m_ref.shape[1:])

    all_c = list(range(2 * _QB))
    plain = [(c, False) for c in all_c]
    own1 = [hh * _QB + 1 for hh in range(2)]
    scores(0, 0, all_c)

    def body(jj, carry):
        j = 2 * jj
        scores(j + 1, 1, all_c)
        absorb(j, 0, plain)
        scores(j + 2, 0, all_c)
        absorb(j + 1, 1, plain)
        return carry

    lax.fori_loop(0, i0 // 2, body, 0)
    scores(i0 + 1, 1, own1)
    absorb(i0, 0, [(hh * _QB + qb, qb == 0) for hh in range(2) for qb in range(_QB)])
    absorb(i0 + 1, 1, [(c, True) for c in own1])

    for qb in range(_QB):
        outs = []
        for hh in range(2):
            acc = acc_ref[hh * _QB + qb]
            outs.append(acc[0:A_HEAD_DIM] / acc[A_HEAD_DIM:A_HEAD_DIM + 1])
        o_ref[0, qb * blk:(qb + 1) * blk, :] = jnp.concatenate(outs, axis=0).T.astype(o_ref.dtype)


def _moba_prompt(qt, krow, vt, kmean, w):
    batch, _, t = qt.shape
    nb = t // MOBA_BLOCK
    blk = MOBA_BLOCK
    qw = _QB * blk
    return pl.pallas_call(
        functools.partial(_moba_prompt_kernel, nb=nb),
        grid=(batch, A_HEADS // 2, nb // _QB),
        in_specs=[
            pl.BlockSpec((1, 2 * A_HEAD_DIM, qw), lambda b, hp, i: (b, hp, i)),
            pl.BlockSpec((1, t, LANE), lambda b, hp, i: (b, 0, hp)),
            pl.BlockSpec((1, 2 * A_HEAD_DIM, t), lambda b, hp, i: (b, hp, 0)),
            pl.BlockSpec((1, nb, LANE), lambda b, hp, i: (b, 0, hp)),
            pl.BlockSpec((1, 2 * _QB, 8, blk), lambda b, hp, i: (hp, 0, 0, 0)),
            pl.BlockSpec((blk, LANE), lambda b, hp, i: (0, 0)),
        ],
        out_specs=pl.BlockSpec((1, qw, LANE), lambda b, hp, i: (b, i, hp)),
        out_shape=jax.ShapeDtypeStruct((batch, t, A_WIDTH), BF16),
        scratch_shapes=[
            pltpu.VMEM((2 * _QB, 2 * LANE, blk), BF16),
            pltpu.VMEM((2 * _QB, A_HEAD_DIM + 8, blk), F32),
            pltpu.VMEM((2 * _QB, 8, blk), F32),
            pltpu.VMEM((2, 2 * _QB, blk, blk), F32),
        ],
        compiler_params=pltpu.CompilerParams(
            dimension_semantics=("arbitrary", "arbitrary", "arbitrary"),
            vmem_limit_bytes=VMEM_LIMIT),
        name="moba_prompt",
    )(qt, krow, vt, kmean.reshape(batch, nb, A_WIDTH), w["qaug"], w["kaug"])


def _moba_sample_kernel(pt_ref, q_ref, kn_ref, vn_ref, slope_ref, ck_hbm, cv_hbm, o_ref,
                        kbuf, sall, gate_ref, idx_v, idx_s, vbuf, sem_k, sem_v, sem_i,
                        *, n_pages, npg):
    b = pl.program_id(0)
    n_chunks = n_pages // npg
    cw = npg * PAGE_SIZE
    bpc = cw // MOBA_BLOCK
    past = n_pages * PAGE_SIZE
    n_blocks = past // MOBA_BLOCK
    t_new = q_ref.shape[2]
    rows = A_HEADS * t_new

    def k_copy(c, p, slot):
        page = pt_ref[b, c * npg + p]
        return pltpu.make_async_copy(
            ck_hbm.at[0, page], kbuf.at[slot, :, :, pl.ds(p * PAGE_SIZE, PAGE_SIZE)], sem_k.at[slot])

    def start_chunk(c, slot):
        for p in range(npg):
            k_copy(c, p, slot).start()

    def wait_chunk(c, slot):
        for p in range(npg):
            k_copy(c, p, slot).wait()

    q_ops = []
    for h in range(A_HEADS):
        qh = q_ref[0, h]
        hi = qh.astype(BF16)
        lo = (qh - hi.astype(F32)).astype(BF16)
        q_ops.append(jnp.concatenate([hi, lo], axis=0))

    gate_ref[...] = jnp.zeros(gate_ref.shape, F32)
    lane_g = lax.broadcasted_iota(jnp.int32, (t_new, LANE), 1)
    start_chunk(0, 0)

    def chunk_body(c, carry):
        slot = c % 2

        @pl.when(c + 1 < n_chunks)
        def _():
            start_chunk(c + 1, 1 - slot)

        wait_chunk(c, slot)
        off = pl.multiple_of(c * cw, cw)
        for h in range(A_HEADS):
            kt = kbuf[slot, h].astype(BF16)
            s2 = _dot(q_ops[h], kt)
            s = s2[0:t_new] + s2[t_new:2 * t_new]
            sall[h * t_new:(h + 1) * t_new, pl.ds(off, cw)] = s
            g = gate_ref[h * t_new:(h + 1) * t_new, :]
            for jb in range(bpc):
                t2 = (s[:, jb * MOBA_BLOCK:jb * MOBA_BLOCK + LANE]
                      + s[:, jb * MOBA_BLOCK + LANE:(jb + 1) * MOBA_BLOCK])
                r = jnp.sum(t2, axis=1, keepdims=True)
                g = jnp.where(lane_g == c * bpc + jb, r, g)
            gate_ref[h * t_new:(h + 1) * t_new, :] = g
        return carry

    lax.fori_loop(0, n_chunks, chunk_body, 0)

    lane_r = lax.broadcasted_iota(jnp.int32, (rows, LANE), 1)
    g = jnp.where(lane_r < n_blocks, gate_ref[...], -jnp.inf)
    idx = jnp.zeros((rows, LANE), jnp.int32)
    for r in range(MOBA_TOPK):
        m = jnp.max(g, axis=1, keepdims=True)
        first = jnp.min(jnp.where(g == m, lane_r, LANE), axis=1, keepdims=True)
        g = jnp.where(lane_r == first, -jnp.inf, g)
        idx = jnp.where(lane_r == r, first, idx)
    idx_v[...] = idx
    cp_i = pltpu.make_async_copy(idx_v, idx_s, sem_i)
    cp_i.start()
    cp_i.wait()

    picks = [(t, r) for t in range(t_new) for r in range(MOBA_TOPK)]

    def v_copy(h, t, r, pg):
        bsel = idx_s[h * t_new + t, r]
        page = pt_ref[b, 2 * bsel + pg]
        col = ((t * MOBA_TOPK + r) * 2 + pg) * PAGE_SIZE
        return pltpu.make_async_copy(
            cv_hbm.at[0, page, h], vbuf.at[h, :, pl.ds(col, PAGE_SIZE)], sem_v.at[h % 2])

    def v_start(h):
        for t, r in picks:
            for pg in range(2):
                v_copy(h, t, r, pg).start()

    def v_wait(h):
        for t, r in picks:
            for pg in range(2):
                v_copy(h, t, r, pg).wait()

    sub = lax.broadcasted_iota(jnp.int32, (t_new, MOBA_BLOCK), 0)
    klane = lax.broadcasted_iota(jnp.int32, (t_new, MOBA_BLOCK), 1)
    trow = lax.broadcasted_iota(jnp.int32, (t_new, t_new), 0)
    tcol = lax.broadcasted_iota(jnp.int32, (t_new, t_new), 1)
    v_start(0)

    def head_step(h, prefetch):
        if prefetch:
            v_start(h + 1)
        r0 = h * t_new if isinstance(h, int) else pl.multiple_of(h * t_new, t_new)
        slope = slope_ref[pl.ds(r0, t_new), :][:, 0:1]
        pieces = []
        for t, r in picks:
            bsel = idx_s[r0 + t, r]
            off = pl.multiple_of(bsel * MOBA_BLOCK, MOBA_BLOCK)
            dist = ((past + t - bsel * MOBA_BLOCK) - klane).astype(F32)
            sc = sall[pl.ds(r0, t_new), pl.ds(off, MOBA_BLOCK)] - slope * dist
            pieces.append(jnp.where(sub == t, sc, NEG))
        s_sel = jnp.concatenate(pieces, axis=1)
        qh = q_ref[0, h]
        s_new = _dot(qh, kn_ref[0, h], _NT) - slope * (trow - tcol).astype(F32)
        s_new = jnp.where(tcol <= trow, s_new, NEG)
        m = jnp.maximum(jnp.max(s_sel, axis=1, keepdims=True), jnp.max(s_new, axis=1, keepdims=True))
        p_sel = jnp.exp(s_sel - m)
        p_new = jnp.exp(s_new - m)
        l = jnp.sum(p_sel, axis=1, keepdims=True) + jnp.sum(p_new, axis=1, keepdims=True)
        v_wait(h)
        acc = _dot(p_sel.astype(BF16), vbuf[h].astype(BF16), _NT) + _dot(p_new, vn_ref[0, h])
        o_ref[0, h] = acc / l

    def head_body(h, carry):
        head_step(h, True)
        return carry

    lax.fori_loop(0, A_HEADS - 1, head_body, 0)
    head_step(A_HEADS - 1, False)


def _moba_sample(q4, kn4, vn4, cache_kt, cache_vt, page_table, slope_rows):
    nbatch, _, t_new, _ = q4.shape
    n_pages = page_table.shape[1]
    npg = 16
    rows = A_HEADS * t_new
    past = n_pages * PAGE_SIZE
    blk4 = pl.BlockSpec((1, A_HEADS, t_new, A_HEAD_DIM), lambda b, pt: (b, 0, 0, 0))
    return pl.pallas_call(
        functools.partial(_moba_sample_kernel, n_pages=n_pages, npg=npg),
        grid_spec=pltpu.PrefetchScalarGridSpec(
            num_scalar_prefetch=1,
            grid=(nbatch,),
            in_specs=[blk4, blk4, blk4,
                      pl.BlockSpec((rows, LANE), lambda b, pt: (0, 0)),
                      pl.BlockSpec(memory_space=pl.ANY),
                      pl.BlockSpec(memory_space=pl.ANY)],
            out_specs=blk4,
            scratch_shapes=[
                pltpu.VMEM((2, A_HEADS, A_HEAD_DIM, npg * PAGE_SIZE), F32),
                pltpu.VMEM((rows, past), F32),
                pltpu.VMEM((rows, LANE), F32),
                pltpu.VMEM((rows, LANE), jnp.int32),
                pltpu.SMEM((rows, LANE), jnp.int32),
                pltpu.VMEM((A_HEADS, A_HEAD_DIM, t_new * MOBA_TOPK * MOBA_BLOCK), F32),
                pltpu.SemaphoreType.DMA((2,)),
                pltpu.SemaphoreType.DMA((2,)),
                pltpu.SemaphoreType.DMA,
            ]),
        out_shape=jax.ShapeDtypeStruct((nbatch, A_HEADS, t_new, A_HEAD_DIM), F32),
        compiler_params=pltpu.CompilerParams(dimension_semantics=("arbitrary",),
                                             vmem_limit_bytes=VMEM_LIMIT),
        name="moba_sample",
    )(page_table, q4, kn4, vn4, slope_rows, cache_kt, cache_vt)


def _gla_kernel(q_ref, k_ref, g_ref, v_ref, s0_ref, o_ref, sfin_ref, state, *, chunk, n_chunks):
    c = chunk
    mx = BF16 if c >= 16 else F32

    @pl.when(pl.program_id(2) == 0)
    def _():
        state[...] = jnp.concatenate([s0_ref[0, 0], jnp.zeros((LANE - B_KEY_DIM, B_VAL_DIM), F32)],
                                     axis=0)

    ri = lax.broadcasted_iota(jnp.int32, (c, LANE), 0)
    ti = lax.broadcasted_iota(jnp.int32, (c, c), 0)
    si = lax.broadcasted_iota(jnp.int32, (c, c), 1)
    tril = jnp.where(si <= ti, 1.0, 0.0).astype(mx)
    ones = jnp.ones((c, LANE), mx)
    levels = int(np.log2(c))

    for ci in range(n_chunks):
        rows = pl.ds(ci * c, c)
        q = q_ref[0, rows, :]
        k = k_ref[0, rows, :]
        g = g_ref[0, rows, :]
        v = v_ref[0, rows, :].astype(mx)
        g_hi = g.astype(BF16)
        g_lo = (g - g_hi.astype(F32)).astype(BF16)
        g_hi, g_lo = g_hi.astype(mx), g_lo.astype(mx)
        b = _dot(tril, g_hi) + _dot(tril, g_lo)
        b_tot = _dot(g_hi, ones, _TN) + _dot(g_lo, ones, _TN)
        b_last = b[c - 1:c, :]

        attn = jnp.where(ti == si, _dot(q.astype(mx), k.astype(mx), _NT), 0.0)
        for lv in range(levels):
            half = c >> (lv + 1)
            width = 2 * half
            pos = ri & (width - 1)
            second = pos >= half
            if half >= 8:
                ref = jnp.concatenate(
                    [jnp.broadcast_to(b[m * width + half - 1:m * width + half, :], (width, LANE))
                     for m in range(c // width)], axis=0)
            else:
                ref = jnp.zeros((c, LANE), F32)
                for p in range(width):
                    sh = (p - (half - 1)) % c
                    rolled = b if sh == 0 else pltpu.roll(b, sh, 0)
                    ref = jnp.where(pos == p, rolled, ref)
            x = jnp.where(second, b - ref, ref - b)
            z = (jnp.where(second, q, k) * jnp.exp(x)).astype(mx)
            gram = _dot(z, z, _NT)
            sb = int(np.log2(width))
            pair = ((ti >> sb) == (si >> sb)) & ((ti & (width - 1)) >= half) & ((si & (width - 1)) < half)
            attn = attn + jnp.where(pair, gram, 0.0)

        s_prev = state[...]
        qd = (q * jnp.exp(b)).astype(mx)
        o = _dot(attn.astype(mx), v) + _dot(qd, s_prev.astype(mx))
        o_ref[0, rows, :] = o
        kd = (k * jnp.exp(b_last - b)).astype(mx)
        state[...] = jnp.exp(b_tot) * s_prev + _dot(kd, v, _TN)

    @pl.when(pl.program_id(2) == pl.num_programs(2) - 1)
    def _():
        sfin_ref[0, 0] = state[0:B_KEY_DIM, :]


def _gla(qb, kb, lg, vb, s0, chunk):
    batch, t, _ = qb.shape
    tc = min(t, 512)
    n_chunks = tc // chunk
    seq = pl.BlockSpec((1, tc, LANE), lambda b, h, c: (b, c, h))
    st = pl.BlockSpec((1, 1, B_KEY_DIM, B_VAL_DIM), lambda b, h, c: (b, h, 0, 0))
    return pl.pallas_call(
        functools.partial(_gla_kernel, chunk=chunk, n_chunks=n_chunks),
        grid=(batch, B_HEADS, t // tc),
        in_specs=[seq, seq, seq, seq, st],
        out_specs=[seq, st],
        out_shape=[jax.ShapeDtypeStruct((batch, t, B_VAL_WIDTH), F32),
                   jax.ShapeDtypeStruct((batch, B_HEADS, B_KEY_DIM, B_VAL_DIM), F32)],
        scratch_shapes=[pltpu.VMEM((LANE, B_VAL_DIM), F32)],
        compiler_params=pltpu.CompilerParams(
            dimension_semantics=("arbitrary", "arbitrary", "arbitrary"),
            vmem_limit_bytes=VMEM_LIMIT),
        name="gla_c%d" % chunk,
    )(qb, kb, lg, vb, s0)


def _merge_kernel(x_ref, oa_ref, sza_ref, ob_ref, szb_ref, gg_ref, wo_ref, y_ref):
    ya = oa_ref[...].astype(F32) * sza_ref[...].astype(F32)
    ob = ob_ref[...]
    parts = []
    for h in range(B_HEADS):
        oh = ob[:, h * B_VAL_DIM:(h + 1) * B_VAL_DIM]
        ms = jnp.mean(oh * oh, axis=-1, keepdims=True)
        parts.append(oh * lax.rsqrt(ms + EPS))
    yb = (jnp.concatenate(parts, axis=1) * gg_ref[...]) * szb_ref[...].astype(F32)
    cat = jnp.concatenate([ya, yb], axis=1).astype(BF16)
    y_ref[...] = x_ref[...] + _dot(cat, wo_ref[...])


def _merge(x2d, oa, sza, ob, szb, w):
    n = x2d.shape[0]
    tm = min(n, 512)
    row = lambda i: (i, 0)
    const = lambda i: (0, 0)
    return pl.pallas_call(
        _merge_kernel,
        grid=(n // tm,),
        in_specs=[pl.BlockSpec((tm, D_MODEL), row), pl.BlockSpec((tm, 512), row),
                  pl.BlockSpec((tm, 512), row), pl.BlockSpec((tm, 512), row),
                  pl.BlockSpec((tm, 512), row), pl.BlockSpec((1, 512), const),
                  pl.BlockSpec((D_MODEL, D_MODEL), const)],
        out_specs=pl.BlockSpec((tm, D_MODEL), row),
        out_shape=jax.ShapeDtypeStruct((n, D_MODEL), F32),
        compiler_params=pltpu.CompilerParams(dimension_semantics=("arbitrary",),
                                             vmem_limit_bytes=VMEM_LIMIT),
        name="merge_out",
    )(x2d, oa, sza, ob, szb, w["g_gla"], w["w_out"])


def _pad_heads(a, n_heads, dim):
    lead = a.shape[:-1]
    a = a.reshape(lead + (n_heads, dim))
    a = jnp.pad(a, [(0, 0)] * len(lead) + [(0, 0), (0, LANE - dim)])
    return a.reshape(lead + (n_heads * LANE,))


def _layer_weights(g_pre, w_in, g_q, g_k, w_a2, b_a, g_gla, w_out):
    o = _OFF
    w_t = w_in[:, o[0]:o[3]].T.astype(BF16)
    w_row = jnp.concatenate([
        w_in[:, o[3]:o[4]], w_in[:, o[7]:o[8]],
        _pad_heads(w_in[:, o[4]:o[5]], B_HEADS, B_KEY_DIM),
        _pad_heads(w_in[:, o[5]:o[6]], B_HEADS, B_KEY_DIM),
        w_in[:, o[6]:o[7]],
        jnp.pad(w_in[:, o[8]:o[9]], ((0, 0), (0, LANE - GATE_RANK)))], axis=1).astype(BF16)
    w_a2p = jnp.pad(_pad_heads(w_a2, B_HEADS, B_KEY_DIM), ((0, LANE - GATE_RANK), (0, 0))).astype(BF16)
    b_ap = _pad_heads(b_a, B_HEADS, B_KEY_DIM).reshape(1, 512)
    gq_t = jnp.broadcast_to(jnp.tile(g_q, A_HEADS)[:, None], (A_WIDTH, MOBA_BLOCK))
    gk_t = jnp.broadcast_to(jnp.tile(g_k, A_HEADS)[:, None], (A_WIDTH, MOBA_BLOCK))
    slopes = jnp.asarray([2.0 ** (-8.0 * (h + 1) / A_HEADS) for h in range(A_HEADS)], F32)
    qq = jnp.arange(MOBA_BLOCK, dtype=F32)
    qaug = jnp.zeros((A_HEADS, _QB, 8, MOBA_BLOCK), F32)
    qaug = qaug.at[:, :, 0, :].set(slopes[:, None, None])
    qaug = qaug.at[:, :, 1, :].set(-slopes[:, None, None] * qq[None, None, :])
    qaug = qaug.at[:, :, 2, :].set(slopes[:, None, None])
    qaug = qaug.at[:, :, 3, :].set(-slopes[:, None, None] * (MOBA_BLOCK * jnp.arange(_QB, dtype=F32))[None, :, None])
    qaug = qaug.reshape(A_HEADS // 2, 2 * _QB, 8, MOBA_BLOCK)
    kaug = jnp.zeros((MOBA_BLOCK, LANE), F32)
    kaug = kaug.at[:, 0].set(qq).at[:, 1].set(1.0).at[:, 3].set(1.0).astype(BF16)
    return {
        "g_pre": g_pre.reshape(1, D_MODEL), "w_t": w_t, "w_row": w_row, "w_a2p": w_a2p, "b_ap": b_ap,
        "gq_t": gq_t, "gk_t": gk_t, "g_gla": g_gla.reshape(1, B_VAL_WIDTH),
        "w_out": w_out.astype(BF16), "qaug": qaug, "kaug": kaug, "slopes": slopes,
    }


def kernel(x_prompt, x_sample, cache_k, cache_v, state_gla, page_table, g_pre, w_in, g_q, g_k, w_a2, b_a, g_gla, w_out):
    depth = g_pre.shape[0]
    batch, t, _ = x_prompt.shape
    nb_s, t_new, _ = x_sample.shape
    cache_kt = jnp.transpose(cache_k, (0, 1, 3, 4, 2))
    cache_vt = jnp.transpose(cache_v, (0, 1, 3, 4, 2))
    yp = x_prompt.reshape(batch * t, D_MODEL)
    ys = x_sample.reshape(nb_s * t_new, D_MODEL)
    kp_l, vp_l, sp_l, ks_l, vs_l, ss_l = [], [], [], [], [], []
    for l in range(depth):
        w = _layer_weights(g_pre[l], w_in[l], g_q[l], g_k[l], w_a2[l], b_a[l], g_gla[l], w_out[l])
        qt, kt, vt, krow, kmean, sza, szb, qb, kb, vb, lg = _proj(yp, w, batch, sample=False)
        oa = _moba_prompt(qt, krow, vt, kmean, w)
        r3 = lambda a: a.reshape(batch, t, 512)
        s0 = jnp.zeros((batch, B_HEADS, B_KEY_DIM, B_VAL_DIM), F32)
        ob, s_fin = _gla(r3(qb), r3(kb), r3(lg), r3(vb), s0, chunk=64)
        yp = _merge(yp, oa.reshape(batch * t, A_WIDTH), sza, ob.reshape(batch * t, 512), szb, w)
        kp_l.append(jnp.transpose(kt.reshape(batch, A_HEADS, A_HEAD_DIM, t), (0, 3, 1, 2)))
        vp_l.append(jnp.transpose(vt.reshape(batch, A_HEADS, A_HEAD_DIM, t), (0, 3, 1, 2)))
        sp_l.append(s_fin)
        qs, ks, vs, sza, szb, qb, kb, vb, lg = _proj(ys, w, nb_s, sample=True)
        h4 = lambda a: jnp.transpose(a.reshape(nb_s, t_new, A_HEADS, A_HEAD_DIM), (0, 2, 1, 3))
        slope_rows = jnp.broadcast_to(jnp.repeat(w["slopes"], t_new)[:, None], (A_HEADS * t_new, LANE))
        oa4 = _moba_sample(h4(qs), h4(ks), h4(vs), cache_kt[l:l + 1], cache_vt[l:l + 1],
                           page_table, slope_rows)
        oa = jnp.transpose(oa4, (0, 2, 1, 3)).reshape(nb_s * t_new, A_WIDTH)
        r3 = lambda a: a.reshape(nb_s, t_new, 512)
        ob, s_new = _gla(r3(qb), r3(kb), r3(lg), r3(vb), state_gla[l], chunk=t_new)
        ys = _merge(ys, oa, sza, ob.reshape(nb_s * t_new, 512), szb, w)
        ks_l.append(ks.reshape(nb_s, t_new, A_HEADS, A_HEAD_DIM))
        vs_l.append(vs.reshape(nb_s, t_new, A_HEADS, A_HEAD_DIM))
        ss_l.append(s_new)
    return (yp.reshape(batch, t, D_MODEL), ys.reshape(nb_s, t_new, D_MODEL),
            jnp.stack(kp_l), jnp.stack(vp_l), jnp.stack(sp_l),
            jnp.stack(ks_l), jnp.stack(vs_l), jnp.stack(ss_l))
```

```python
import functools

import jax
import jax.numpy as jnp
import numpy as np
from jax import lax
from jax.experimental import pallas as pl
from jax.experimental.pallas import tpu as pltpu

F32 = jnp.float32
BF16 = jnp.bfloat16

D_MODEL = 1024
A_HEADS = 8
A_HEAD_DIM = 64
A_WIDTH = A_HEADS * A_HEAD_DIM
MOBA_BLOCK = 256
MOBA_TOPK = 3
B_HEADS = 4
B_KEY_DIM = 64
B_VAL_DIM = 128
B_KEY_WIDTH = B_HEADS * B_KEY_DIM
B_VAL_WIDTH = B_HEADS * B_VAL_DIM
GATE_RANK = 16
GATE_TAU = 16.0
PAGE_SIZE = 128
EPS = 1e-6
NEG = -1e30

LANE = 128
VMEM_LIMIT = 56 * 1024 * 1024

_OFF = np.cumsum([0, A_WIDTH, A_WIDTH, A_WIDTH, A_WIDTH, B_KEY_WIDTH, B_KEY_WIDTH,
                  B_VAL_WIDTH, B_VAL_WIDTH, GATE_RANK]).tolist()
_ROW_W = 5 * 512 + LANE
_AUG0 = 128
_MASK0 = 136
_LOG2E = 1.4426950408889634


def _dot(a, b, dims=(((1,), (0,)), ((), ())), precision=None):
    return lax.dot_general(a, b, dims, precision=precision, preferred_element_type=F32)


_NT = (((1,), (1,)), ((), ()))
_TN = (((0,), (0,)), ((), ()))


def _silu(x):
    return x / (1.0 + jnp.exp(-x))


def _log_sigmoid(x):
    return jnp.minimum(x, 0.0) - jnp.log1p(jnp.exp(-jnp.abs(x)))


def _proj_kernel(x_ref, gpre_ref, wt_ref, wrow_ref, wa2_ref, ba_ref, gq_ref, gk_ref, *out_refs,
                 tm, sample):
    x = x_ref[...]
    ms = jnp.mean(x * x, axis=-1, keepdims=True)
    h = ((x * lax.rsqrt(ms + EPS)) * gpre_ref[...]).astype(BF16)

    pt = _dot(wt_ref[...], h, _NT)

    def head_norm(t, g):
        t3 = t.reshape(A_HEADS, A_HEAD_DIM, tm)
        ss = jnp.mean(t3 * t3, axis=1, keepdims=True)
        return (t3 * lax.rsqrt(ss + EPS)).reshape(A_WIDTH, tm) * g

    q_t = head_norm(pt[0:A_WIDTH], gq_ref[...]) * (A_HEAD_DIM ** -0.5)
    k_t = head_norm(pt[A_WIDTH:2 * A_WIDTH], gk_ref[...])
    v_t = pt[2 * A_WIDTH:3 * A_WIDTH]

    def seg(i0, i1):
        return _dot(h, wrow_ref[:, i0:i1])

    sza = _silu(seg(0, 512)).astype(BF16)
    szb = _silu(seg(512, 1024)).astype(BF16)
    qb = seg(1024, 1536) * (B_KEY_DIM ** -0.5)
    kb = seg(1536, 2048)
    vb = seg(2048, 2560)
    ab = seg(2560, 2688).astype(BF16)
    pre = _dot(ab, wa2_ref[...]) + ba_ref[...]
    lg = _log_sigmoid(pre) * (_LOG2E / GATE_TAU)

    if sample:
        (q_ref, k_ref, v_ref, sza_ref, szb_ref, qb_ref, kb_ref, vb_ref, lg_ref) = out_refs
        q_ref[...] = q_t.T
        k_ref[...] = k_t.T
        v_ref[...] = v_t.T
    else:
        (qt_ref, kt_ref, vt_ref, krow_ref, kmean_ref,
         sza_ref, szb_ref, qb_ref, kb_ref, vb_ref, lg_ref) = out_refs
        qt_ref[0] = q_t
        kt_ref[0] = k_t
        vt_ref[0] = v_t
        k_row = k_t.T
        krow_ref[0] = k_row.astype(BF16)
        kmean_ref[0] = jnp.mean(k_row, axis=0, keepdims=True)
    sza_ref[...] = sza
    szb_ref[...] = szb
    qb_ref[...] = qb
    kb_ref[...] = kb
    vb_ref[...] = vb
    lg_ref[...] = lg


def _proj(x2d, w, batch, sample):
    n = x2d.shape[0]
    tm = MOBA_BLOCK
    nt = n // tm
    const = lambda i: (0, 0)
    row = lambda i: (i, 0)
    in_specs = [
        pl.BlockSpec((tm, D_MODEL), row),
        pl.BlockSpec((1, D_MODEL), const),
        pl.BlockSpec((3 * A_WIDTH, D_MODEL), const),
        pl.BlockSpec((D_MODEL, _ROW_W), const),
        pl.BlockSpec((LANE, 512), const),
        pl.BlockSpec((1, 512), const),
        pl.BlockSpec((A_WIDTH, tm), const),
        pl.BlockSpec((A_WIDTH, tm), const),
    ]
    row_specs = [pl.BlockSpec((tm, 512), row)] * 6
    row_shapes = [jax.ShapeDtypeStruct((n, 512), BF16)] * 2 + [jax.ShapeDtypeStruct((n, 512), F32)] * 4
    if sample:
        out_specs = [pl.BlockSpec((tm, 512), row)] * 3 + row_specs
        out_shape = [jax.ShapeDtypeStruct((n, 512), F32)] * 3 + row_shapes
    else:
        t = n // batch
        tpb = t // tm
        feat = lambda i: (i // tpb, 0, i % tpb)
        out_specs = ([pl.BlockSpec((1, A_WIDTH, tm), feat)] * 3
                     + [pl.BlockSpec((1, tm, 512), lambda i: (i // tpb, i % tpb, 0)),
                        pl.BlockSpec((1, 1, 512), lambda i: (i, 0, 0))]
                     + row_specs)
        out_shape = ([jax.ShapeDtypeStruct((batch, A_WIDTH, t), F32)] * 3
                     + [jax.ShapeDtypeStruct((batch, t, 512), BF16),
                        jax.ShapeDtypeStruct((nt, 1, 512), F32)]
                     + row_shapes)
    return pl.pallas_call(
        functools.partial(_proj_kernel, tm=tm, sample=sample),
        grid=(nt,),
        in_specs=in_specs,
        out_specs=out_specs,
        out_shape=out_shape,
        compiler_params=pltpu.CompilerParams(dimension_semantics=("arbitrary",),
                                             vmem_limit_bytes=VMEM_LIMIT),
        name="proj_sample" if sample else "proj_prompt",
    )(x2d, w["g_pre"], w["w_t"], w["w_row"], w["w_a2p"], w["b_ap"], w["gq_t"], w["gk_t"])


def _top3_rows(g, n):
    idx = lax.broadcasted_iota(jnp.int32, g.shape, 0)
    sel = jnp.zeros(g.shape, F32)
    for _ in range(MOBA_TOPK):
        m = jnp.max(g, axis=0, keepdims=True)
        first = jnp.min(jnp.where(g == m, idx, n), axis=0, keepdims=True)
        pick = idx == first
        sel = jnp.where(pick, 1.0, sel)
        g = jnp.where(pick, -jnp.inf, g)
    return sel


_QB = 4


def _moba_prompt_kernel(qt_ref, krow_ref, vt_ref, kmean_ref, qaug_ref, kaug_ref, o_ref,
                        qop_ref, acc_ref, m_ref, s_ref, *, nb):
    blk = MOBA_BLOCK
    qw = _QB * blk
    i0 = pl.program_id(2) * _QB
    qt = qt_ref[0]
    row128 = lax.broadcasted_iota(jnp.int32, (2 * A_HEAD_DIM, qw), 0)
    lane128 = lax.broadcasted_iota(jnp.int32, (nb, LANE), 1)
    bidx = lax.broadcasted_iota(jnp.int32, (nb, qw), 0)
    iq = i0 + lax.broadcasted_iota(jnp.int32, (nb, qw), 1) // blk
    kmean = kmean_ref[0]

    for hh in range(2):
        in_head = (row128 >= hh * A_HEAD_DIM) & (row128 < (hh + 1) * A_HEAD_DIM)
        q_h = jnp.where(in_head, qt * _LOG2E, 0.0)
        km_h = jnp.where((lane128 >= hh * A_HEAD_DIM) & (lane128 < (hh + 1) * A_HEAD_DIM), kmean, 0.0)
        gate = _dot(km_h, qt, precision=lax.Precision.HIGHEST)
        past = bidx < iq
        sel = _top3_rows(jnp.where(past, gate, NEG), nb)
        keep = ((sel > 0.5) & past) | (bidx == iq)
        maskbias = jnp.where(keep, 0.0, NEG)
        for qb in range(_QB):
            c = hh * _QB + qb
            cols = slice(qb * blk, (qb + 1) * blk)
            qop = jnp.concatenate(
                [q_h[:, cols], qaug_ref[0, c], maskbias[:, cols],
                 jnp.zeros((2 * LANE - _MASK0 - nb, blk), F32)], axis=0)
            qop_ref[c] = qop.astype(BF16)
            acc_ref[c] = jnp.zeros(acc_ref.shape[1:], F32)
            m_ref[c] = jnp.full(m_ref.shape[1:], -jnp.inf, F32)

    lane_k = lax.broadcasted_iota(jnp.int32, (blk, LANE), 1)
    kaug_base = kaug_ref[...]
    ones_rows = jnp.ones((8, blk), BF16)
    kk = lax.broadcasted_iota(jnp.int32, (blk, blk), 0)
    qq = lax.broadcasted_iota(jnp.int32, (blk, blk), 1)

    def scores(j, slot, chains):
        off = pl.multiple_of(j * blk, blk)
        shift = ((j - i0) * blk).astype(F32)
        kaug = jnp.where((lane_k >= 3) & (lane_k < 6), shift.astype(BF16),
                         jnp.where(lane_k == 8 + j, jnp.ones((), BF16), kaug_base))
        kop = jnp.concatenate([krow_ref[0, pl.ds(off, blk), :], kaug], axis=1)
        for c in chains:
            s_ref[slot, c] = _dot(kop, qop_ref[c])

    def absorb(j, slot, chains):
        off = pl.multiple_of(j * blk, blk)
        vops = []
        for hh in range(2):
            v_t = vt_ref[0, hh * A_HEAD_DIM:(hh + 1) * A_HEAD_DIM, pl.ds(off, blk)].astype(BF16)
            vops.append(jnp.concatenate([v_t, ones_rows], axis=0))
        for c, causal in chains:
            s = s_ref[slot, c]
            if causal:
                s = jnp.where(kk <= qq, s, NEG)
            m_prev = m_ref[c][0:1]
            m_new = jnp.maximum(m_prev, jnp.max(s, axis=0, keepdims=True))
            alpha = jnp.exp2(m_prev - m_new)
            p = jnp.exp2(s - m_new).astype(BF16)
            acc_ref[c] = alpha * acc_ref[c] + _dot(vops[c // _QB], p)
            m_ref[c] = jnp.broadcast_to(m_new, m_ref.shape[1:])

    all_c = list(range(2 * _QB))
    plain = [(c, False) for c in all_c]
    scores(0, 0, all_c)

    def body(jj, carry):
        j = 2 * jj
        scores(j + 1, 1, all_c)
        absorb(j, 0, plain)
        scores(j + 2, 0, all_c)
        absorb(j + 1, 1, plain)
        return carry

    lax.fori_loop(0, i0 // 2, body, 0)
    for d in range(_QB):
        if d + 1 < _QB:
            scores(i0 + d + 1, (d + 1) % 2, [c for c in all_c if c % _QB >= d + 1])
        absorb(i0 + d, d % 2, [(c, c % _QB == d) for c in all_c if c % _QB >= d])

    for qb in range(_QB):
        outs = []
        for hh in range(2):
            acc = acc_ref[hh * _QB + qb]
            outs.append(acc[0:A_HEAD_DIM] / acc[A_HEAD_DIM:A_HEAD_DIM + 1])
        o_ref[0, qb * blk:(qb + 1) * blk, :] = jnp.concatenate(outs, axis=0).T.astype(o_ref.dtype)


def _moba_prompt(qt, krow, vt, kmean, w):
    batch, _, t = qt.shape
    nb = t // MOBA_BLOCK
    blk = MOBA_BLOCK
    qw = _QB * blk
    return pl.pallas_call(
        functools.partial(_moba_prompt_kernel, nb=nb),
        grid=(batch, A_HEADS // 2, nb // _QB),
        in_specs=[
            pl.BlockSpec((1, 2 * A_HEAD_DIM, qw), lambda b, hp, i: (b, hp, i)),
            pl.BlockSpec((1, t, LANE), lambda b, hp, i: (b, 0, hp)),
            pl.BlockSpec((1, 2 * A_HEAD_DIM, t), lambda b, hp, i: (b, hp, 0)),
            pl.BlockSpec((1, nb, LANE), lambda b, hp, i: (b, 0, hp)),
            pl.BlockSpec((1, 2 * _QB, 8, blk), lambda b, hp, i: (hp, 0, 0, 0)),
            pl.BlockSpec((blk, LANE), lambda b, hp, i: (0, 0)),
        ],
        out_specs=pl.BlockSpec((1, qw, LANE), lambda b, hp, i: (b, i, hp)),
        out_shape=jax.ShapeDtypeStruct((batch, t, A_WIDTH), BF16),
        scratch_shapes=[
            pltpu.VMEM((2 * _QB, 2 * LANE, blk), BF16),
            pltpu.VMEM((2 * _QB, A_HEAD_DIM + 8, blk), F32),
            pltpu.VMEM((2 * _QB, 8, blk), F32),
            pltpu.VMEM((2, 2 * _QB, blk, blk), F32),
        ],
        compiler_params=pltpu.CompilerParams(
            dimension_semantics=("arbitrary", "arbitrary", "arbitrary"),
            vmem_limit_bytes=VMEM_LIMIT),
        name="moba_prompt",
    )(qt, krow, vt, kmean.reshape(batch, nb, A_WIDTH), w["qaug"], w["kaug"])


def _moba_sample_kernel(pt_ref, q_ref, kn_ref, vn_ref, slope_ref, ck_hbm, cv_hbm, o_ref,
                        kbuf, sall, gate_ref, idx_v, idx_s, vbuf, sem_k, sem_v, sem_i,
                        *, n_pages, npg):
    b = pl.program_id(0)
    n_chunks = n_pages // npg
    cw = npg * PAGE_SIZE
    bpc = cw // MOBA_BLOCK
    past = n_pages * PAGE_SIZE
    n_blocks = past // MOBA_BLOCK
    t_new = q_ref.shape[2]
    rows = A_HEADS * t_new

    def k_copy(bb, c, p, slot):
        page = pt_ref[bb, c * npg + p]
        return pltpu.make_async_copy(
            ck_hbm.at[0, page], kbuf.at[slot, :, :, pl.ds(p * PAGE_SIZE, PAGE_SIZE)], sem_k.at[slot])

    def start_chunk(bb, c, slot):
        for p in range(npg):
            k_copy(bb, c, p, slot).start()

    def wait_chunk(c, slot):
        for p in range(npg):
            k_copy(b, c, p, slot).wait()

    @pl.when(b == 0)
    def _():
        start_chunk(0, 0, 0)
        start_chunk(0, 1, 1)

    q_ops = []
    for h in range(A_HEADS):
        qh = q_ref[0, h]
        hi = qh.astype(BF16)
        lo = (qh - hi.astype(F32)).astype(BF16)
        q_ops.append(jnp.concatenate([hi, lo], axis=0))

    gate_ref[...] = jnp.zeros(gate_ref.shape, F32)
    lane_g = lax.broadcasted_iota(jnp.int32, (t_new, LANE), 1)

    def chunk_step(c, refill):
        slot = c % 2
        wait_chunk(c, slot)
        off = c * cw if isinstance(c, int) else pl.multiple_of(c * cw, cw)
        for h in range(A_HEADS):
            kt = kbuf[slot, h].astype(BF16)
            s2 = _dot(q_ops[h], kt)
            s = s2[0:t_new] + s2[t_new:2 * t_new]
            sall[h * t_new:(h + 1) * t_new, pl.ds(off, cw)] = s
            g = gate_ref[h * t_new:(h + 1) * t_new, :]
            for jb in range(bpc):
                t2 = (s[:, jb * MOBA_BLOCK:jb * MOBA_BLOCK + LANE]
                      + s[:, jb * MOBA_BLOCK + LANE:(jb + 1) * MOBA_BLOCK])
                r = jnp.sum(t2, axis=1, keepdims=True)
                g = jnp.where(lane_g == c * bpc + jb, r, g)
            gate_ref[h * t_new:(h + 1) * t_new, :] = g
        if refill:
            start_chunk(b, c + 2, slot)

    def chunk_body(c, carry):
        chunk_step(c, True)
        return carry

    lax.fori_loop(0, n_chunks - 2, chunk_body, 0)
    chunk_step(n_chunks - 2, False)
    chunk_step(n_chunks - 1, False)

    @pl.when(b + 1 < pl.num_programs(0))
    def _():
        start_chunk(b + 1, 0, 0)
        start_chunk(b + 1, 1, 1)

    lane_r = lax.broadcasted_iota(jnp.int32, (rows, LANE), 1)
    g = jnp.where(lane_r < n_blocks, gate_ref[...], -jnp.inf)
    idx = jnp.zeros((rows, LANE), jnp.int32)
    for r in range(MOBA_TOPK):
        m = jnp.max(g, axis=1, keepdims=True)
        first = jnp.min(jnp.where(g == m, lane_r, LANE), axis=1, keepdims=True)
        g = jnp.where(lane_r == first, -jnp.inf, g)
        idx = jnp.where(lane_r == r, first, idx)
    idx_v[...] = idx
    cp_i = pltpu.make_async_copy(idx_v, idx_s, sem_i)
    cp_i.start()
    cp_i.wait()

    picks = [(t, r) for t in range(t_new) for r in range(MOBA_TOPK)]

    def v_copy(h, t, r, pg):
        bsel = idx_s[h * t_new + t, r]
        page = pt_ref[b, 2 * bsel + pg]
        col = ((t * MOBA_TOPK + r) * 2 + pg) * PAGE_SIZE
        return pltpu.make_async_copy(
            cv_hbm.at[0, page, h], vbuf.at[h, :, pl.ds(col, PAGE_SIZE)], sem_v.at[h % 2])

    def v_start(h):
        for t, r in picks:
            for pg in range(2):
                v_copy(h, t, r, pg).start()

    def v_wait(h):
        for t, r in picks:
            for pg in range(2):
                v_copy(h, t, r, pg).wait()

    sub = lax.broadcasted_iota(jnp.int32, (t_new, MOBA_BLOCK), 0)
    klane = lax.broadcasted_iota(jnp.int32, (t_new, MOBA_BLOCK), 1)
    trow = lax.broadcasted_iota(jnp.int32, (t_new, t_new), 0)
    tcol = lax.broadcasted_iota(jnp.int32, (t_new, t_new), 1)
    v_start(0)

    def head_step(h, prefetch):
        if prefetch:
            v_start(h + 1)
        r0 = h * t_new if isinstance(h, int) else pl.multiple_of(h * t_new, t_new)
        slope = slope_ref[pl.ds(r0, t_new), :][:, 0:1]
        pieces = []
        for t, r in picks:
            bsel = idx_s[r0 + t, r]
            off = pl.multiple_of(bsel * MOBA_BLOCK, MOBA_BLOCK)
            dist = ((past + t - bsel * MOBA_BLOCK) - klane).astype(F32)
            sc = sall[pl.ds(r0, t_new), pl.ds(off, MOBA_BLOCK)] - slope * dist
            pieces.append(jnp.where(sub == t, sc, NEG))
        s_sel = jnp.concatenate(pieces, axis=1)
        qh = q_ref[0, h]
        s_new = _dot(qh, kn_ref[0, h], _NT) - slope * (trow - tcol).astype(F32)
        s_new = jnp.where(tcol <= trow, s_new, NEG)
        m = jnp.maximum(jnp.max(s_sel, axis=1, keepdims=True), jnp.max(s_new, axis=1, keepdims=True))
        p_sel = jnp.exp(s_sel - m)
        p_new = jnp.exp(s_new - m)
        l = jnp.sum(p_sel, axis=1, keepdims=True) + jnp.sum(p_new, axis=1, keepdims=True)
        v_wait(h)
        acc = _dot(p_sel.astype(BF16), vbuf[h].astype(BF16), _NT) + _dot(p_new, vn_ref[0, h])
        o_ref[0, h] = acc / l

    def head_body(h, carry):
        head_step(h, True)
        return carry

    lax.fori_loop(0, A_HEADS - 1, head_body, 0)
    head_step(A_HEADS - 1, False)


def _moba_sample(q4, kn4, vn4, cache_kt, cache_vt, page_table, slope_rows):
    nbatch, _, t_new, _ = q4.shape
    n_pages = page_table.shape[1]
    npg = 16
    rows = A_HEADS * t_new
    past = n_pages * PAGE_SIZE
    blk4 = pl.BlockSpec((1, A_HEADS, t_new, A_HEAD_DIM), lambda b, pt: (b, 0, 0, 0))
    return pl.pallas_call(
        functools.partial(_moba_sample_kernel, n_pages=n_pages, npg=npg),
        grid_spec=pltpu.PrefetchScalarGridSpec(
            num_scalar_prefetch=1,
            grid=(nbatch,),
            in_specs=[blk4, blk4, blk4,
                      pl.BlockSpec((rows, LANE), lambda b, pt: (0, 0)),
                      pl.BlockSpec(memory_space=pl.ANY),
                      pl.BlockSpec(memory_space=pl.ANY)],
            out_specs=blk4,
            scratch_shapes=[
                pltpu.VMEM((2, A_HEADS, A_HEAD_DIM, npg * PAGE_SIZE), F32),
                pltpu.VMEM((rows, past), F32),
                pltpu.VMEM((rows, LANE), F32),
                pltpu.VMEM((rows, LANE), jnp.int32),
                pltpu.SMEM((rows, LANE), jnp.int32),
                pltpu.VMEM((A_HEADS, A_HEAD_DIM, t_new * MOBA_TOPK * MOBA_BLOCK), F32),
                pltpu.SemaphoreType.DMA((2,)),
                pltpu.SemaphoreType.DMA((2,)),
                pltpu.SemaphoreType.DMA,
            ]),
        out_shape=jax.ShapeDtypeStruct((nbatch, A_HEADS, t_new, A_HEAD_DIM), F32),
        compiler_params=pltpu.CompilerParams(dimension_semantics=("arbitrary",),
                                             vmem_limit_bytes=VMEM_LIMIT),
        name="moba_sample",
    )(page_table, q4, kn4, vn4, slope_rows, cache_kt, cache_vt)


def _gla_kernel(q_ref, k_ref, g_ref, v_ref, s0_ref, o_ref, sfin_ref, state, *, chunk, n_chunks, nbb):
    c = chunk
    mx = BF16 if c >= 16 else F32

    @pl.when(pl.program_id(2) == 0)
    def _():
        for bi in range(nbb):
            state[bi] = jnp.concatenate([s0_ref[bi, 0], jnp.zeros((LANE - B_KEY_DIM, B_VAL_DIM), F32)],
                                        axis=0)

    ri = lax.broadcasted_iota(jnp.int32, (c, LANE), 0)
    ti = lax.broadcasted_iota(jnp.int32, (c, c), 0)
    si = lax.broadcasted_iota(jnp.int32, (c, c), 1)
    tril = jnp.where(si <= ti, 1.0, 0.0).astype(mx)
    ones = jnp.ones((c, LANE), mx)
    sub3 = lax.broadcasted_iota(jnp.int32, (c // 8, 8, LANE), 1)
    diag = ti == si
    levels = []
    half = c // 2
    while half >= 1:
        width = 2 * half
        sb = int(np.log2(width))
        upper = (ri & (width - 1)) >= half
        pair = ((ti >> sb) == (si >> sb)) & ((ti & (width - 1)) >= half) & ((si & (width - 1)) < half)
        levels.append((half, width, upper, pair))
        half //= 2

    for bi, ci in [(bi, ci) for bi in range(nbb) for ci in range(n_chunks)]:
        rows = pl.ds(ci * c, c)
        q = q_ref[bi, rows, :]
        k = k_ref[bi, rows, :]
        g = g_ref[bi, rows, :]
        v = v_ref[bi, rows, :].astype(mx)
        g_hi = g.astype(BF16)
        g_lo = (g - g_hi.astype(F32)).astype(BF16)
        g_hi, g_lo = g_hi.astype(mx), g_lo.astype(mx)
        b = _dot(tril, g_hi) + _dot(tril, g_lo)
        b_tot = _dot(g_hi, ones, _TN) + _dot(g_lo, ones, _TN)
        b_last = b[c - 1:c, :]
        b3 = b.reshape(c // 8, 8, LANE)

        attn = jnp.where(diag, _dot(q.astype(mx), k.astype(mx), _NT), 0.0)
        for half, width, upper, pair in levels:
            if half == 1:
                x = jnp.where(upper, g, 0.0)
            else:
                if half >= 8:
                    mid = jnp.concatenate(
                        [jnp.broadcast_to(b[m * width + half - 1:m * width + half, :], (width, LANE))
                         for m in range(c // width)], axis=0)
                else:
                    mid3 = jnp.broadcast_to(b3[:, half - 1:half, :], b3.shape)
                    for m in range(1, 8 // width):
                        r = m * width + half - 1
                        mid3 = jnp.where(sub3 >= m * width,
                                         jnp.broadcast_to(b3[:, r:r + 1, :], b3.shape), mid3)
                    mid = mid3.reshape(c, LANE)
                x = jnp.where(upper, b - mid, mid - b)
            z = (jnp.where(upper, q, k) * jnp.exp2(x)).astype(mx)
            attn = attn + jnp.where(pair, _dot(z, z, _NT), 0.0)

        s_prev = state[bi]
        qd = (q * jnp.exp2(b)).astype(mx)
        o = _dot(attn.astype(mx), v) + _dot(qd, s_prev.astype(mx))
        o_ref[bi, rows, :] = o
        kd = (k * jnp.exp2(b_last - b)).astype(mx)
        state[bi] = jnp.exp2(b_tot) * s_prev + _dot(kd, v, _TN)

    @pl.when(pl.program_id(2) == pl.num_programs(2) - 1)
    def _():
        for bi in range(nbb):
            sfin_ref[bi, 0] = state[bi, 0:B_KEY_DIM, :]


def _gla(qb, kb, lg, vb, s0, chunk):
    batch, t, _ = qb.shape
    tc = min(t, 512)
    n_chunks = tc // chunk
    nbb = 8 if (t == tc and batch % 8 == 0) else 1
    seq = pl.BlockSpec((nbb, tc, LANE), lambda b, h, c: (b, c, h))
    st = pl.BlockSpec((nbb, 1, B_KEY_DIM, B_VAL_DIM), lambda b, h, c: (b, h, 0, 0))
    return pl.pallas_call(
        functools.partial(_gla_kernel, chunk=chunk, n_chunks=n_chunks, nbb=nbb),
        grid=(batch // nbb, B_HEADS, t // tc),
        in_specs=[seq, seq, seq, seq, st],
        out_specs=[seq, st],
        out_shape=[jax.ShapeDtypeStruct((batch, t, B_VAL_WIDTH), F32),
                   jax.ShapeDtypeStruct((batch, B_HEADS, B_KEY_DIM, B_VAL_DIM), F32)],
        scratch_shapes=[pltpu.VMEM((nbb, LANE, B_VAL_DIM), F32)],
        compiler_params=pltpu.CompilerParams(
            dimension_semantics=("arbitrary", "arbitrary", "arbitrary"),
            vmem_limit_bytes=VMEM_LIMIT),
        name="gla_c%d" % chunk,
    )(qb, kb, lg, vb, s0)


def _merge_kernel(x_ref, oa_ref, sza_ref, ob_ref, szb_ref, gg_ref, wo_ref, y_ref):
    ya = oa_ref[...].astype(F32) * sza_ref[...].astype(F32)
    ob = ob_ref[...]
    parts = []
    for h in range(B_HEADS):
        oh = ob[:, h * B_VAL_DIM:(h + 1) * B_VAL_DIM]
        ms = jnp.mean(oh * oh, axis=-1, keepdims=True)
        parts.append(oh * lax.rsqrt(ms + EPS))
    yb = (jnp.concatenate(parts, axis=1) * gg_ref[...]) * szb_ref[...].astype(F32)
    cat = jnp.concatenate([ya, yb], axis=1).astype(BF16)
    y_ref[...] = x_ref[...] + _dot(cat, wo_ref[...])


def _merge(x2d, oa, sza, ob, szb, w):
    n = x2d.shape[0]
    tm = min(n, 512)
    row = lambda i: (i, 0)
    const = lambda i: (0, 0)
    return pl.pallas_call(
        _merge_kernel,
        grid=(n // tm,),
        in_specs=[pl.BlockSpec((tm, D_MODEL), row), pl.BlockSpec((tm, 512), row),
                  pl.BlockSpec((tm, 512), row), pl.BlockSpec((tm, 512), row),
                  pl.BlockSpec((tm, 512), row), pl.BlockSpec((1, 512), const),
                  pl.BlockSpec((D_MODEL, D_MODEL), const)],
        out_specs=pl.BlockSpec((tm, D_MODEL), row),
        out_shape=jax.ShapeDtypeStruct((n, D_MODEL), F32),
        compiler_params=pltpu.CompilerParams(dimension_semantics=("arbitrary",),
                                             vmem_limit_bytes=VMEM_LIMIT),
        name="merge_out",
    )(x2d, oa, sza, ob, szb, w["g_gla"], w["w_out"])


def _pad_heads(a, n_heads, dim):
    lead = a.shape[:-1]
    a = a.reshape(lead + (n_heads, dim))
    a = jnp.pad(a, [(0, 0)] * len(lead) + [(0, 0), (0, LANE - dim)])
    return a.reshape(lead + (n_heads * LANE,))


def _layer_weights(g_pre, w_in, g_q, g_k, w_a2, b_a, g_gla, w_out):
    o = _OFF
    w_t = w_in[:, o[0]:o[3]].T.astype(BF16)
    w_row = jnp.concatenate([
        w_in[:, o[3]:o[4]], w_in[:, o[7]:o[8]],
        _pad_heads(w_in[:, o[4]:o[5]], B_HEADS, B_KEY_DIM),
        _pad_heads(w_in[:, o[5]:o[6]], B_HEADS, B_KEY_DIM),
        w_in[:, o[6]:o[7]],
        jnp.pad(w_in[:, o[8]:o[9]], ((0, 0), (0, LANE - GATE_RANK)))], axis=1).astype(BF16)
    w_a2p = jnp.pad(_pad_heads(w_a2, B_HEADS, B_KEY_DIM), ((0, LANE - GATE_RANK), (0, 0))).astype(BF16)
    b_ap = _pad_heads(b_a, B_HEADS, B_KEY_DIM).reshape(1, 512)
    gq_t = jnp.broadcast_to(jnp.tile(g_q, A_HEADS)[:, None], (A_WIDTH, MOBA_BLOCK))
    gk_t = jnp.broadcast_to(jnp.tile(g_k, A_HEADS)[:, None], (A_WIDTH, MOBA_BLOCK))
    slopes = jnp.asarray([2.0 ** (-8.0 * (h + 1) / A_HEADS) for h in range(A_HEADS)], F32)
    c = slopes * _LOG2E
    pieces = []
    rem = c
    for _ in range(3):
        pc = rem.astype(BF16).astype(F32)
        pieces.append(pc)
        rem = rem - pc
    qq = jnp.arange(MOBA_BLOCK, dtype=F32)
    qaug = jnp.zeros((A_HEADS, _QB, 8, MOBA_BLOCK), F32)
    for p_i, pc in enumerate(pieces):
        qaug = qaug.at[:, :, p_i, :].set(pc[:, None, None])
        qaug = qaug.at[:, :, 3 + p_i, :].set(pc[:, None, None])
    qpos = qq[None, None, :] + (MOBA_BLOCK * jnp.arange(_QB, dtype=F32))[None, :, None]
    qaug = qaug.at[:, :, 6, :].set(-c[:, None, None] * qpos)
    qaug = qaug.reshape(A_HEADS // 2, 2 * _QB, 8, MOBA_BLOCK)
    kaug = jnp.zeros((MOBA_BLOCK, LANE), F32)
    kaug = kaug.at[:, 0:3].set(qq[:, None]).at[:, 6].set(1.0).astype(BF16)
    return {
        "g_pre": g_pre.reshape(1, D_MODEL), "w_t": w_t, "w_row": w_row, "w_a2p": w_a2p, "b_ap": b_ap,
        "gq_t": gq_t, "gk_t": gk_t, "g_gla": g_gla.reshape(1, B_VAL_WIDTH),
        "w_out": w_out.astype(BF16), "qaug": qaug, "kaug": kaug, "slopes": slopes,
    }


def kernel(x_prompt, x_sample, cache_k, cache_v, state_gla, page_table, g_pre, w_in, g_q, g_k, w_a2, b_a, g_gla, w_out):
    depth = g_pre.shape[0]
    batch, t, _ = x_prompt.shape
    nb_s, t_new, _ = x_sample.shape
    cache_kt = jnp.transpose(cache_k, (0, 1, 3, 4, 2))
    cache_vt = jnp.transpose(cache_v, (0, 1, 3, 4, 2))
    yp = x_prompt.reshape(batch * t, D_MODEL)
    ys = x_sample.reshape(nb_s * t_new, D_MODEL)
    kp_l, vp_l, sp_l, ks_l, vs_l, ss_l = [], [], [], [], [], []
    for l in range(depth):
        w = _layer_weights(g_pre[l], w_in[l], g_q[l], g_k[l], w_a2[l], b_a[l], g_gla[l], w_out[l])
        qt, kt, vt, krow, kmean, sza, szb, qb, kb, vb, lg = _proj(yp, w, batch, sample=False)
        oa = _moba_prompt(qt, krow, vt, kmean, w)
        r3 = lambda a: a.reshape(batch, t, 512)
        s0 = jnp.zeros((batch, B_HEADS, B_KEY_DIM, B_VAL_DIM), F32)
        ob, s_fin = _gla(r3(qb), r3(kb), r3(lg), r3(vb), s0, chunk=64)
        yp = _merge(yp, oa.reshape(batch * t, A_WIDTH), sza, ob.reshape(batch * t, 512), szb, w)
        kp_l.append(jnp.transpose(kt.reshape(batch, A_HEADS, A_HEAD_DIM, t), (0, 3, 1, 2)))
        vp_l.append(jnp.transpose(vt.reshape(batch, A_HEADS, A_HEAD_DIM, t), (0, 3, 1, 2)))
        sp_l.append(s_fin)
        qs, ks, vs, sza, szb, qb, kb, vb, lg = _proj(ys, w, nb_s, sample=True)
        h4 = lambda a: jnp.transpose(a.reshape(nb_s, t_new, A_HEADS, A_HEAD_DIM), (0, 2, 1, 3))
        slope_rows = jnp.broadcast_to(jnp.repeat(w["slopes"], t_new)[:, None], (A_HEADS * t_new, LANE))
        oa4 = _moba_sample(h4(qs), h4(ks), h4(vs), cache_kt[l:l + 1], cache_vt[l:l + 1],
                           page_table, slope_rows)
        oa = jnp.transpose(oa4, (0, 2, 1, 3)).reshape(nb_s * t_new, A_WIDTH)
        r3 = lambda a: a.reshape(nb_s, t_new, 512)
        ob, s_new = _gla(r3(qb), r3(kb), r3(lg), r3(vb), state_gla[l], chunk=t_new)
        ys = _merge(ys, oa, sza, ob.reshape(nb_s * t_new, 512), szb, w)
        ks_l.append(ks.reshape(nb_s, t_new, A_HEADS, A_HEAD_DIM))
        vs_l.append(vs.reshape(nb_s, t_new, A_HEADS, A_HEAD_DIM))
        ss_l.append(s_new)
    return (yp.reshape(batch, t, D_MODEL), ys.reshape(nb_s, t_new, D_MODEL),
            jnp.stack(kp_l), jnp.stack(vp_l), jnp.stack(sp_l),
            jnp.stack(ks_l), jnp.stack(vs_l), jnp.stack(ss_l))
```

```python
import functools

import jax
import jax.numpy as jnp
import numpy as np
from jax import lax
from jax.experimental import pallas as pl
from jax.experimental.pallas import tpu as pltpu

F32 = jnp.float32
BF16 = jnp.bfloat16

D_MODEL = 1024
A_HEADS = 8
A_HEAD_DIM = 64
A_WIDTH = A_HEADS * A_HEAD_DIM
MOBA_BLOCK = 256
MOBA_TOPK = 3
B_HEADS = 4
B_KEY_DIM = 64
B_VAL_DIM = 128
B_KEY_WIDTH = B_HEADS * B_KEY_DIM
B_VAL_WIDTH = B_HEADS * B_VAL_DIM
GATE_RANK = 16
GATE_TAU = 16.0
PAGE_SIZE = 128
EPS = 1e-6
NEG = -1e30

LANE = 128
VMEM_LIMIT = 56 * 1024 * 1024

_OFF = np.cumsum([0, A_WIDTH, A_WIDTH, A_WIDTH, A_WIDTH, B_KEY_WIDTH, B_KEY_WIDTH,
                  B_VAL_WIDTH, B_VAL_WIDTH, GATE_RANK]).tolist()
_ROW_W = 5 * 512 + LANE
_AUG0 = 128
_MASK0 = 136
_LOG2E = 1.4426950408889634


def _dot(a, b, dims=(((1,), (0,)), ((), ())), precision=None):
    return lax.dot_general(a, b, dims, precision=precision, preferred_element_type=F32)


_NT = (((1,), (1,)), ((), ()))
_TN = (((0,), (0,)), ((), ()))


def _silu(x):
    return x / (1.0 + jnp.exp(-x))


def _log_sigmoid(x):
    return jnp.minimum(x, 0.0) - jnp.log1p(jnp.exp(-jnp.abs(x)))


def _proj_kernel(x_ref, gpre_ref, wt_ref, wrow_ref, wa2_ref, ba_ref, gq_ref, gk_ref, *out_refs,
                 tm, sample):
    x = x_ref[...]
    ms = jnp.mean(x * x, axis=-1, keepdims=True)
    h = ((x * lax.rsqrt(ms + EPS)) * gpre_ref[...]).astype(BF16)

    pt = _dot(wt_ref[...], h, _NT)

    def head_norm(t, g):
        t3 = t.reshape(A_HEADS, A_HEAD_DIM, tm)
        ss = jnp.mean(t3 * t3, axis=1, keepdims=True)
        return (t3 * lax.rsqrt(ss + EPS)).reshape(A_WIDTH, tm) * g

    q_t = head_norm(pt[0:A_WIDTH], gq_ref[...]) * (A_HEAD_DIM ** -0.5)
    k_t = head_norm(pt[A_WIDTH:2 * A_WIDTH], gk_ref[...])
    v_t = pt[2 * A_WIDTH:3 * A_WIDTH]

    def seg(i0, i1):
        return _dot(h, wrow_ref[:, i0:i1])

    sza = _silu(seg(0, 512)).astype(BF16)
    szb = _silu(seg(512, 1024)).astype(BF16)
    qb = seg(1024, 1536) * (B_KEY_DIM ** -0.5)
    kb = seg(1536, 2048)
    vb = seg(2048, 2560)
    ab = seg(2560, 2688).astype(BF16)
    pre = _dot(ab, wa2_ref[...]) + ba_ref[...]
    lg = _log_sigmoid(pre) * (_LOG2E / GATE_TAU)

    if sample:
        (q_ref, k_ref, v_ref, sza_ref, szb_ref, qb_ref, kb_ref, vb_ref, lg_ref) = out_refs
        q_ref[...] = q_t.T
        k_ref[...] = k_t.T
        v_ref[...] = v_t.T
    else:
        (qt_ref, kt_ref, vt_ref, krow_ref, kmean_ref,
         sza_ref, szb_ref, qb_ref, kb_ref, vb_ref, lg_ref) = out_refs
        qt_ref[0] = q_t
        kt_ref[0] = k_t
        vt_ref[0] = v_t
        k_row = k_t.T
        krow_ref[0] = k_row.astype(BF16)
        kmean_ref[0] = jnp.mean(k_row, axis=0, keepdims=True)
    sza_ref[...] = sza
    szb_ref[...] = szb
    qb_ref[...] = qb
    kb_ref[...] = kb
    vb_ref[...] = vb
    lg_ref[...] = lg


def _proj(x2d, w, batch, sample):
    n = x2d.shape[0]
    tm = MOBA_BLOCK
    nt = n // tm
    const = lambda i: (0, 0)
    row = lambda i: (i, 0)
    in_specs = [
        pl.BlockSpec((tm, D_MODEL), row),
        pl.BlockSpec((1, D_MODEL), const),
        pl.BlockSpec((3 * A_WIDTH, D_MODEL), const),
        pl.BlockSpec((D_MODEL, _ROW_W), const),
        pl.BlockSpec((LANE, 512), const),
        pl.BlockSpec((1, 512), const),
        pl.BlockSpec((A_WIDTH, tm), const),
        pl.BlockSpec((A_WIDTH, tm), const),
    ]
    row_specs = [pl.BlockSpec((tm, 512), row)] * 6
    row_shapes = [jax.ShapeDtypeStruct((n, 512), BF16)] * 2 + [jax.ShapeDtypeStruct((n, 512), F32)] * 4
    if sample:
        out_specs = [pl.BlockSpec((tm, 512), row)] * 3 + row_specs
        out_shape = [jax.ShapeDtypeStruct((n, 512), F32)] * 3 + row_shapes
    else:
        t = n // batch
        tpb = t // tm
        feat = lambda i: (i // tpb, 0, i % tpb)
        out_specs = ([pl.BlockSpec((1, A_WIDTH, tm), feat)] * 3
                     + [pl.BlockSpec((1, tm, 512), lambda i: (i // tpb, i % tpb, 0)),
                        pl.BlockSpec((1, 1, 512), lambda i: (i, 0, 0))]
                     + row_specs)
        out_shape = ([jax.ShapeDtypeStruct((batch, A_WIDTH, t), F32)] * 3
                     + [jax.ShapeDtypeStruct((batch, t, 512), BF16),
                        jax.ShapeDtypeStruct((nt, 1, 512), F32)]
                     + row_shapes)
    return pl.pallas_call(
        functools.partial(_proj_kernel, tm=tm, sample=sample),
        grid=(nt,),
        in_specs=in_specs,
        out_specs=out_specs,
        out_shape=out_shape,
        compiler_params=pltpu.CompilerParams(dimension_semantics=("arbitrary",),
                                             vmem_limit_bytes=VMEM_LIMIT),
        name="proj_sample" if sample else "proj_prompt",
    )(x2d, w["g_pre"], w["w_t"], w["w_row"], w["w_a2p"], w["b_ap"], w["gq_t"], w["gk_t"])


def _top3_rows(g, n):
    idx = lax.broadcasted_iota(jnp.int32, g.shape, 0)
    sel = jnp.zeros(g.shape, F32)
    for _ in range(MOBA_TOPK):
        m = jnp.max(g, axis=0, keepdims=True)
        first = jnp.min(jnp.where(g == m, idx, n), axis=0, keepdims=True)
        pick = idx == first
        sel = jnp.where(pick, 1.0, sel)
        g = jnp.where(pick, -jnp.inf, g)
    return sel


_QB = 4


def _moba_prompt_kernel(qt_ref, krow_ref, vt_ref, kmean_ref, qaug_ref, kaug_ref, o_ref,
                        qop_ref, acc_ref, m_ref, s_ref, *, nb):
    blk = MOBA_BLOCK
    qw = _QB * blk
    i0 = pl.program_id(2) * _QB
    qt = qt_ref[0]
    row128 = lax.broadcasted_iota(jnp.int32, (2 * A_HEAD_DIM, qw), 0)
    lane128 = lax.broadcasted_iota(jnp.int32, (nb, LANE), 1)
    bidx = lax.broadcasted_iota(jnp.int32, (nb, qw), 0)
    iq = i0 + lax.broadcasted_iota(jnp.int32, (nb, qw), 1) // blk
    kmean = kmean_ref[0]

    for hh in range(2):
        in_head = (row128 >= hh * A_HEAD_DIM) & (row128 < (hh + 1) * A_HEAD_DIM)
        q_h = jnp.where(in_head, qt * _LOG2E, 0.0)
        km_h = jnp.where((lane128 >= hh * A_HEAD_DIM) & (lane128 < (hh + 1) * A_HEAD_DIM), kmean, 0.0)
        gate = _dot(km_h, qt, precision=lax.Precision.HIGHEST)
        past = bidx < iq
        sel = _top3_rows(jnp.where(past, gate, NEG), nb)
        keep = ((sel > 0.5) & past) | (bidx == iq)
        maskbias = jnp.where(keep, 0.0, NEG)
        for qb in range(_QB):
            c = hh * _QB + qb
            cols = slice(qb * blk, (qb + 1) * blk)
            qop = jnp.concatenate(
                [q_h[:, cols], qaug_ref[0, c], maskbias[:, cols],
                 jnp.zeros((2 * LANE - _MASK0 - nb, blk), F32)], axis=0)
            qop_ref[c] = qop.astype(BF16)
            acc_ref[c] = jnp.zeros(acc_ref.shape[1:], F32)
            m_ref[c] = jnp.full(m_ref.shape[1:], -jnp.inf, F32)

    lane_k = lax.broadcasted_iota(jnp.int32, (blk, LANE), 1)
    kaug_base = kaug_ref[...]
    ones_rows = jnp.ones((8, blk), BF16)
    kk = lax.broadcasted_iota(jnp.int32, (blk, blk), 0)
    qq = lax.broadcasted_iota(jnp.int32, (blk, blk), 1)

    def scores(j, slot, chains):
        off = pl.multiple_of(j * blk, blk)
        shift = ((j - i0) * blk).astype(F32)
        kaug = jnp.where((lane_k >= 3) & (lane_k < 6), shift.astype(BF16),
                         jnp.where(lane_k == 8 + j, jnp.ones((), BF16), kaug_base))
        kop = jnp.concatenate([krow_ref[0, pl.ds(off, blk), :], kaug], axis=1)
        for c in chains:
            s_ref[slot, c] = _dot(kop, qop_ref[c])

    def absorb(j, slot, chains):
        off = pl.multiple_of(j * blk, blk)
        vops = []
        for hh in range(2):
            v_t = vt_ref[0, hh * A_HEAD_DIM:(hh + 1) * A_HEAD_DIM, pl.ds(off, blk)].astype(BF16)
            vops.append(jnp.concatenate([v_t, ones_rows], axis=0))
        for c, causal in chains:
            s = s_ref[slot, c]
            if causal:
                s = jnp.where(kk <= qq, s, NEG)
            m_prev = m_ref[c][0:1]
            m_new = jnp.maximum(m_prev, jnp.max(s, axis=0, keepdims=True))
            alpha = jnp.exp2(m_prev - m_new)
            p = jnp.exp2(s - m_new).astype(BF16)
            acc_ref[c] = alpha * acc_ref[c] + _dot(vops[c // _QB], p)
            m_ref[c] = jnp.broadcast_to(m_new, m_ref.shape[1:])

    all_c = list(range(2 * _QB))
    plain = [(c, False) for c in all_c]
    scores(0, 0, all_c)

    def body(jj, carry):
        j = 2 * jj
        scores(j + 1, 1, all_c)
        absorb(j, 0, plain)
        scores(j + 2, 0, all_c)
        absorb(j + 1, 1, plain)
        return carry

    lax.fori_loop(0, i0 // 2, body, 0)
    for d in range(_QB):
        if d + 1 < _QB:
            scores(i0 + d + 1, (d + 1) % 2, [c for c in all_c if c % _QB >= d + 1])
        absorb(i0 + d, d % 2, [(c, c % _QB == d) for c in all_c if c % _QB >= d])

    for qb in range(_QB):
        outs = []
        for hh in range(2):
            acc = acc_ref[hh * _QB + qb]
            outs.append(acc[0:A_HEAD_DIM] / acc[A_HEAD_DIM:A_HEAD_DIM + 1])
        o_ref[0, qb * blk:(qb + 1) * blk, :] = jnp.concatenate(outs, axis=0).T.astype(o_ref.dtype)


def _moba_prompt(qt, krow, vt, kmean, w):
    batch, _, t = qt.shape
    nb = t // MOBA_BLOCK
    blk = MOBA_BLOCK
    qw = _QB * blk
    return pl.pallas_call(
        functools.partial(_moba_prompt_kernel, nb=nb),
        grid=(batch, A_HEADS // 2, nb // _QB),
        in_specs=[
            pl.BlockSpec((1, 2 * A_HEAD_DIM, qw), lambda b, hp, i: (b, hp, i)),
            pl.BlockSpec((1, t, LANE), lambda b, hp, i: (b, 0, hp)),
            pl.BlockSpec((1, 2 * A_HEAD_DIM, t), lambda b, hp, i: (b, hp, 0)),
            pl.BlockSpec((1, nb, LANE), lambda b, hp, i: (b, 0, hp)),
            pl.BlockSpec((1, 2 * _QB, 8, blk), lambda b, hp, i: (hp, 0, 0, 0)),
            pl.BlockSpec((blk, LANE), lambda b, hp, i: (0, 0)),
        ],
        out_specs=pl.BlockSpec((1, qw, LANE), lambda b, hp, i: (b, i, hp)),
        out_shape=jax.ShapeDtypeStruct((batch, t, A_WIDTH), BF16),
        scratch_shapes=[
            pltpu.VMEM((2 * _QB, 2 * LANE, blk), BF16),
            pltpu.VMEM((2 * _QB, A_HEAD_DIM + 8, blk), F32),
            pltpu.VMEM((2 * _QB, 8, blk), F32),
            pltpu.VMEM((2, 2 * _QB, blk, blk), F32),
        ],
        compiler_params=pltpu.CompilerParams(
            dimension_semantics=("arbitrary", "arbitrary", "arbitrary"),
            vmem_limit_bytes=VMEM_LIMIT),
        name="moba_prompt",
    )(qt, krow, vt, kmean.reshape(batch, nb, A_WIDTH), w["qaug"], w["kaug"])


def _moba_sample_kernel(pt_ref, q_ref, kn_ref, vn_ref, slope_ref, ck_hbm, cv_hbm, o_ref,
                        kbuf, sall, gate_ref, idx_v, idx_s, vbuf, sem_k, sem_v, sem_i,
                        *, n_pages, npg, ring):
    b = pl.program_id(0)
    n_chunks = n_pages // npg
    cw = npg * PAGE_SIZE
    bpc = cw // MOBA_BLOCK
    past = n_pages * PAGE_SIZE
    n_blocks = past // MOBA_BLOCK
    t_new = q_ref.shape[2]
    rows = A_HEADS * t_new

    def k_copy(bb, c, p, slot):
        page = pt_ref[bb, c * npg + p]
        return pltpu.make_async_copy(
            ck_hbm.at[0, page], kbuf.at[slot, :, :, pl.ds(p * PAGE_SIZE, PAGE_SIZE)], sem_k.at[slot])

    def start_chunk(bb, c, slot):
        for p in range(npg):
            k_copy(bb, c, p, slot).start()

    def wait_chunk(c, slot):
        for p in range(npg):
            k_copy(b, c, p, slot).wait()

    @pl.when(b == 0)
    def _():
        for c0 in range(ring):
            start_chunk(0, c0, c0)

    q_ops = []
    for h in range(A_HEADS):
        qh = q_ref[0, h]
        hi = qh.astype(BF16)
        lo = (qh - hi.astype(F32)).astype(BF16)
        q_ops.append(jnp.concatenate([hi, lo], axis=0))

    gate_ref[...] = jnp.zeros(gate_ref.shape, F32)
    lane_g = lax.broadcasted_iota(jnp.int32, (t_new, LANE), 1)

    def chunk_step(c, refill):
        slot = c % ring
        wait_chunk(c, slot)
        off = c * cw if isinstance(c, int) else pl.multiple_of(c * cw, cw)
        for h in range(A_HEADS):
            kt = kbuf[slot, h].astype(BF16)
            s2 = _dot(q_ops[h], kt)
            s = s2[0:t_new] + s2[t_new:2 * t_new]
            sall[h * t_new:(h + 1) * t_new, pl.ds(off, cw)] = s
            g = gate_ref[h * t_new:(h + 1) * t_new, :]
            for jb in range(bpc):
                t2 = (s[:, jb * MOBA_BLOCK:jb * MOBA_BLOCK + LANE]
                      + s[:, jb * MOBA_BLOCK + LANE:(jb + 1) * MOBA_BLOCK])
                r = jnp.sum(t2, axis=1, keepdims=True)
                g = jnp.where(lane_g == c * bpc + jb, r, g)
            gate_ref[h * t_new:(h + 1) * t_new, :] = g
        if refill:
            start_chunk(b, c + ring, slot)

    def refill_body(c, carry):
        chunk_step(c, True)
        return carry

    def drain_body(c, carry):
        chunk_step(c, False)
        return carry

    lax.fori_loop(0, n_chunks - ring, refill_body, 0)
    lax.fori_loop(n_chunks - ring, n_chunks, drain_body, 0)

    @pl.when(b + 1 < pl.num_programs(0))
    def _():
        for c0 in range(ring):
            start_chunk(b + 1, c0, c0)

    lane_r = lax.broadcasted_iota(jnp.int32, (rows, LANE), 1)
    g = jnp.where(lane_r < n_blocks, gate_ref[...], -jnp.inf)
    idx = jnp.zeros((rows, LANE), jnp.int32)
    for r in range(MOBA_TOPK):
        m = jnp.max(g, axis=1, keepdims=True)
        first = jnp.min(jnp.where(g == m, lane_r, LANE), axis=1, keepdims=True)
        g = jnp.where(lane_r == first, -jnp.inf, g)
        idx = jnp.where(lane_r == r, first, idx)
    idx_v[...] = idx
    cp_i = pltpu.make_async_copy(idx_v, idx_s, sem_i)
    cp_i.start()
    cp_i.wait()

    picks = [(t, r) for t in range(t_new) for r in range(MOBA_TOPK)]

    def v_copy(h, t, r, pg):
        bsel = idx_s[h * t_new + t, r]
        page = pt_ref[b, 2 * bsel + pg]
        col = ((t * MOBA_TOPK + r) * 2 + pg) * PAGE_SIZE
        return pltpu.make_async_copy(
            cv_hbm.at[0, page, h], vbuf.at[h, :, pl.ds(col, PAGE_SIZE)], sem_v.at[h % 2])

    def v_start(h):
        for t, r in picks:
            for pg in range(2):
                v_copy(h, t, r, pg).start(priority=1)

    def v_wait(h):
        for t, r in picks:
            for pg in range(2):
                v_copy(h, t, r, pg).wait()

    sub = lax.broadcasted_iota(jnp.int32, (t_new, MOBA_BLOCK), 0)
    klane = lax.broadcasted_iota(jnp.int32, (t_new, MOBA_BLOCK), 1)
    trow = lax.broadcasted_iota(jnp.int32, (t_new, t_new), 0)
    tcol = lax.broadcasted_iota(jnp.int32, (t_new, t_new), 1)
    v_start(0)

    def head_step(h, prefetch):
        if prefetch:
            v_start(h + 1)
        r0 = h * t_new if isinstance(h, int) else pl.multiple_of(h * t_new, t_new)
        slope = slope_ref[pl.ds(r0, t_new), :][:, 0:1]
        pieces = []
        for t, r in picks:
            bsel = idx_s[r0 + t, r]
            off = pl.multiple_of(bsel * MOBA_BLOCK, MOBA_BLOCK)
            dist = ((past + t - bsel * MOBA_BLOCK) - klane).astype(F32)
            sc = sall[pl.ds(r0, t_new), pl.ds(off, MOBA_BLOCK)] - slope * dist
            pieces.append(jnp.where(sub == t, sc, NEG))
        s_sel = jnp.concatenate(pieces, axis=1)
        qh = q_ref[0, h]
        s_new = _dot(qh, kn_ref[0, h], _NT) - slope * (trow - tcol).astype(F32)
        s_new = jnp.where(tcol <= trow, s_new, NEG)
        m = jnp.maximum(jnp.max(s_sel, axis=1, keepdims=True), jnp.max(s_new, axis=1, keepdims=True))
        p_sel = jnp.exp(s_sel - m)
        p_new = jnp.exp(s_new - m)
        l = jnp.sum(p_sel, axis=1, keepdims=True) + jnp.sum(p_new, axis=1, keepdims=True)
        v_wait(h)
        acc = _dot(p_sel.astype(BF16), vbuf[h].astype(BF16), _NT) + _dot(p_new, vn_ref[0, h])
        o_ref[0, h] = acc / l

    def head_body(h, carry):
        head_step(h, True)
        return carry

    lax.fori_loop(0, A_HEADS - 1, head_body, 0)
    head_step(A_HEADS - 1, False)


def _moba_sample(q4, kn4, vn4, cache_kt, cache_vt, page_table, slope_rows):
    nbatch, _, t_new, _ = q4.shape
    n_pages = page_table.shape[1]
    npg = 16
    ring = min(4, n_pages // npg)
    rows = A_HEADS * t_new
    past = n_pages * PAGE_SIZE
    blk4 = pl.BlockSpec((1, A_HEADS, t_new, A_HEAD_DIM), lambda b, pt: (b, 0, 0, 0))
    return pl.pallas_call(
        functools.partial(_moba_sample_kernel, n_pages=n_pages, npg=npg, ring=ring),
        grid_spec=pltpu.PrefetchScalarGridSpec(
            num_scalar_prefetch=1,
            grid=(nbatch,),
            in_specs=[blk4, blk4, blk4,
                      pl.BlockSpec((rows, LANE), lambda b, pt: (0, 0)),
                      pl.BlockSpec(memory_space=pl.ANY),
                      pl.BlockSpec(memory_space=pl.ANY)],
            out_specs=blk4,
            scratch_shapes=[
                pltpu.VMEM((ring, A_HEADS, A_HEAD_DIM, npg * PAGE_SIZE), F32),
                pltpu.VMEM((rows, past), F32),
                pltpu.VMEM((rows, LANE), F32),
                pltpu.VMEM((rows, LANE), jnp.int32),
                pltpu.SMEM((rows, LANE), jnp.int32),
                pltpu.VMEM((A_HEADS, A_HEAD_DIM, t_new * MOBA_TOPK * MOBA_BLOCK), F32),
                pltpu.SemaphoreType.DMA((ring,)),
                pltpu.SemaphoreType.DMA((2,)),
                pltpu.SemaphoreType.DMA,
            ]),
        out_shape=jax.ShapeDtypeStruct((nbatch, A_HEADS, t_new, A_HEAD_DIM), F32),
        compiler_params=pltpu.CompilerParams(dimension_semantics=("arbitrary",),
                                             vmem_limit_bytes=VMEM_LIMIT),
        name="moba_sample",
    )(page_table, q4, kn4, vn4, slope_rows, cache_kt, cache_vt)


def _gla_kernel(q_ref, k_ref, g_ref, v_ref, s0_ref, o_ref, sfin_ref, state, *, chunk, n_chunks, nbb):
    c = chunk
    mx = BF16 if c >= 16 else F32

    @pl.when(pl.program_id(2) == 0)
    def _():
        for bi in range(nbb):
            state[bi] = jnp.concatenate([s0_ref[bi, 0], jnp.zeros((LANE - B_KEY_DIM, B_VAL_DIM), F32)],
                                        axis=0).T

    ri = lax.broadcasted_iota(jnp.int32, (c, LANE), 0)
    ti = lax.broadcasted_iota(jnp.int32, (c, c), 0)
    si = lax.broadcasted_iota(jnp.int32, (c, c), 1)
    tril = jnp.where(si <= ti, 1.0, 0.0).astype(mx)
    sub3 = lax.broadcasted_iota(jnp.int32, (c // 8, 8, LANE), 1)
    levels = []
    half = c // 2
    while half >= 1:
        width = 2 * half
        sb = int(np.log2(width))
        upper = (ri & (width - 1)) >= half
        pair = ((ti >> sb) == (si >> sb)) & ((ti & (width - 1)) >= half) & ((si & (width - 1)) < half)
        levels.append((half, width, upper, pair))
        half //= 2

    seq = [(bi, ci) for bi in range(nbb) for ci in range(n_chunks)]
    loaded = []
    for bi, ci in seq:
        g = g_ref[bi, pl.ds(ci * c, c), :]
        g_hi = g.astype(BF16)
        g_lo = (g - g_hi.astype(F32)).astype(BF16)
        loaded.append((g, _dot(tril, g_hi.astype(mx)) + _dot(tril, g_lo.astype(mx))))

    prepared = []
    for (bi, ci), (g, b) in zip(seq, loaded):
        rows = pl.ds(ci * c, c)
        q = q_ref[bi, rows, :]
        k = k_ref[bi, rows, :]
        v32 = v_ref[bi, rows, :]
        b_last = b[c - 1:c, :]
        b3 = b.reshape(c // 8, 8, LANE)
        zs = []
        for half, width, upper, _ in levels:
            if half == 1:
                x = jnp.where(upper, g, 0.0)
            else:
                if half >= 8:
                    mid = jnp.concatenate(
                        [jnp.broadcast_to(b[m * width + half - 1:m * width + half, :], (width, LANE))
                         for m in range(c // width)], axis=0)
                else:
                    mid3 = jnp.broadcast_to(b3[:, half - 1:half, :], b3.shape)
                    for m in range(1, 8 // width):
                        r = m * width + half - 1
                        mid3 = jnp.where(sub3 >= m * width,
                                         jnp.broadcast_to(b3[:, r:r + 1, :], b3.shape), mid3)
                    mid = mid3.reshape(c, LANE)
                x = jnp.where(upper, b - mid, mid - b)
            zs.append((jnp.where(upper, q, k) * jnp.exp2(x)).astype(mx))
        qd = (q * jnp.exp2(b)).astype(mx)
        kd = (k * jnp.exp2(b_last - b)).astype(mx)
        o_same = jnp.sum(q * k, axis=1, keepdims=True) * v32
        prepared.append((v32.astype(mx), zs, qd, kd, jnp.exp2(b_last), o_same))

    grams, updates = [], []
    for v, zs, qd, kd, a_last, o_same in prepared:
        grams.append([_dot(z, z, _NT) for z in zs])
        updates.append(_dot(v, kd, _TN))

    attns = []
    for gr in grams:
        attn = jnp.where(levels[0][3], gr[0], 0.0)
        for (_, _, _, pair), gm in zip(levels[1:], gr[1:]):
            attn = attn + jnp.where(pair, gm, 0.0)
        attns.append(attn.astype(mx))

    states = []
    for n, (bi, ci) in enumerate(seq):
        st = state[bi] if ci == 0 else states[-1][1]
        states.append((st, prepared[n][4] * st + updates[n]))
        if ci == n_chunks - 1:
            state[bi] = states[-1][1]

    for n, (bi, ci) in enumerate(seq):
        v, zs, qd, kd, a_last, o_same = prepared[n]
        o_ref[bi, pl.ds(ci * c, c), :] = (o_same + _dot(attns[n], v)
                                          + _dot(qd, states[n][0].astype(mx), _NT))

    @pl.when(pl.program_id(2) == pl.num_programs(2) - 1)
    def _():
        for bi in range(nbb):
            sfin_ref[bi, 0] = state[bi].T[0:B_KEY_DIM, :]


def _gla(qb, kb, lg, vb, s0, chunk):
    batch, t, _ = qb.shape
    tc = min(t, 1024)
    n_chunks = tc // chunk
    nbb = 8 if (t == tc and batch % 8 == 0) else 1
    seq = pl.BlockSpec((nbb, tc, LANE), lambda b, h, c: (b, c, h))
    st = pl.BlockSpec((nbb, 1, B_KEY_DIM, B_VAL_DIM), lambda b, h, c: (b, h, 0, 0))
    return pl.pallas_call(
        functools.partial(_gla_kernel, chunk=chunk, n_chunks=n_chunks, nbb=nbb),
        grid=(batch // nbb, B_HEADS, t // tc),
        in_specs=[seq, seq, seq, seq, st],
        out_specs=[seq, st],
        out_shape=[jax.ShapeDtypeStruct((batch, t, B_VAL_WIDTH), F32),
                   jax.ShapeDtypeStruct((batch, B_HEADS, B_KEY_DIM, B_VAL_DIM), F32)],
        scratch_shapes=[pltpu.VMEM((nbb, LANE, B_VAL_DIM), F32)],
        compiler_params=pltpu.CompilerParams(
            dimension_semantics=("arbitrary", "arbitrary", "arbitrary"),
            vmem_limit_bytes=VMEM_LIMIT),
        name="gla_c%d" % chunk,
    )(qb, kb, lg, vb, s0)


def _merge_kernel(x_ref, oa_ref, sza_ref, ob_ref, szb_ref, gg_ref, wo_ref, y_ref):
    ya = oa_ref[...].astype(F32) * sza_ref[...].astype(F32)
    ob = ob_ref[...]
    parts = []
    for h in range(B_HEADS):
        oh = ob[:, h * B_VAL_DIM:(h + 1) * B_VAL_DIM]
        ms = jnp.mean(oh * oh, axis=-1, keepdims=True)
        parts.append(oh * lax.rsqrt(ms + EPS))
    yb = (jnp.concatenate(parts, axis=1) * gg_ref[...]) * szb_ref[...].astype(F32)
    cat = jnp.concatenate([ya, yb], axis=1).astype(BF16)
    y_ref[...] = x_ref[...] + _dot(cat, wo_ref[...])


def _merge(x2d, oa, sza, ob, szb, w):
    n = x2d.shape[0]
    tm = min(n, 512)
    row = lambda i: (i, 0)
    const = lambda i: (0, 0)
    return pl.pallas_call(
        _merge_kernel,
        grid=(n // tm,),
        in_specs=[pl.BlockSpec((tm, D_MODEL), row), pl.BlockSpec((tm, 512), row),
                  pl.BlockSpec((tm, 512), row), pl.BlockSpec((tm, 512), row),
                  pl.BlockSpec((tm, 512), row), pl.BlockSpec((1, 512), const),
                  pl.BlockSpec((D_MODEL, D_MODEL), const)],
        out_specs=pl.BlockSpec((tm, D_MODEL), row),
        out_shape=jax.ShapeDtypeStruct((n, D_MODEL), F32),
        compiler_params=pltpu.CompilerParams(dimension_semantics=("arbitrary",),
                                             vmem_limit_bytes=VMEM_LIMIT),
        name="merge_out",
    )(x2d, oa, sza, ob, szb, w["g_gla"], w["w_out"])


def _pad_heads(a, n_heads, dim):
    lead = a.shape[:-1]
    a = a.reshape(lead + (n_heads, dim))
    a = jnp.pad(a, [(0, 0)] * len(lead) + [(0, 0), (0, LANE - dim)])
    return a.reshape(lead + (n_heads * LANE,))


def _layer_weights(g_pre, w_in, g_q, g_k, w_a2, b_a, g_gla, w_out):
    o = _OFF
    w_t = w_in[:, o[0]:o[3]].T.astype(BF16)
    w_row = jnp.concatenate([
        w_in[:, o[3]:o[4]], w_in[:, o[7]:o[8]],
        _pad_heads(w_in[:, o[4]:o[5]], B_HEADS, B_KEY_DIM),
        _pad_heads(w_in[:, o[5]:o[6]], B_HEADS, B_KEY_DIM),
        w_in[:, o[6]:o[7]],
        jnp.pad(w_in[:, o[8]:o[9]], ((0, 0), (0, LANE - GATE_RANK)))], axis=1).astype(BF16)
    w_a2p = jnp.pad(_pad_heads(w_a2, B_HEADS, B_KEY_DIM), ((0, LANE - GATE_RANK), (0, 0))).astype(BF16)
    b_ap = _pad_heads(b_a, B_HEADS, B_KEY_DIM).reshape(1, 512)
    gq_t = jnp.broadcast_to(jnp.tile(g_q, A_HEADS)[:, None], (A_WIDTH, MOBA_BLOCK))
    gk_t = jnp.broadcast_to(jnp.tile(g_k, A_HEADS)[:, None], (A_WIDTH, MOBA_BLOCK))
    slopes = jnp.asarray([2.0 ** (-8.0 * (h + 1) / A_HEADS) for h in range(A_HEADS)], F32)
    c = slopes * _LOG2E
    pieces = []
    rem = c
    for _ in range(3):
        pc = rem.astype(BF16).astype(F32)
        pieces.append(pc)
        rem = rem - pc
    qq = jnp.arange(MOBA_BLOCK, dtype=F32)
    qaug = jnp.zeros((A_HEADS, _QB, 8, MOBA_BLOCK), F32)
    for p_i, pc in enumerate(pieces):
        qaug = qaug.at[:, :, p_i, :].set(pc[:, None, None])
        qaug = qaug.at[:, :, 3 + p_i, :].set(pc[:, None, None])
    qpos = qq[None, None, :] + (MOBA_BLOCK * jnp.arange(_QB, dtype=F32))[None, :, None]
    qaug = qaug.at[:, :, 6, :].set(-c[:, None, None] * qpos)
    qaug = qaug.reshape(A_HEADS // 2, 2 * _QB, 8, MOBA_BLOCK)
    kaug = jnp.zeros((MOBA_BLOCK, LANE), F32)
    kaug = kaug.at[:, 0:3].set(qq[:, None]).at[:, 6].set(1.0).astype(BF16)
    return {
        "g_pre": g_pre.reshape(1, D_MODEL), "w_t": w_t, "w_row": w_row, "w_a2p": w_a2p, "b_ap": b_ap,
        "gq_t": gq_t, "gk_t": gk_t, "g_gla": g_gla.reshape(1, B_VAL_WIDTH),
        "w_out": w_out.astype(BF16), "qaug": qaug, "kaug": kaug, "slopes": slopes,
    }


def kernel(x_prompt, x_sample, cache_k, cache_v, state_gla, page_table, g_pre, w_in, g_q, g_k, w_a2, b_a, g_gla, w_out):
    depth = g_pre.shape[0]
    batch, t, _ = x_prompt.shape
    nb_s, t_new, _ = x_sample.shape
    cache_kt = jnp.transpose(cache_k, (0, 1, 3, 4, 2))
    cache_vt = jnp.transpose(cache_v, (0, 1, 3, 4, 2))
    yp = x_prompt.reshape(batch * t, D_MODEL)
    ys = x_sample.reshape(nb_s * t_new, D_MODEL)
    kp_l, vp_l, sp_l, ks_l, vs_l, ss_l = [], [], [], [], [], []
    for l in range(depth):
        w = _layer_weights(g_pre[l], w_in[l], g_q[l], g_k[l], w_a2[l], b_a[l], g_gla[l], w_out[l])
        qt, kt, vt, krow, kmean, sza, szb, qb, kb, vb, lg = _proj(yp, w, batch, sample=False)
        oa = _moba_prompt(qt, krow, vt, kmean, w)
        r3 = lambda a: a.reshape(batch, t, 512)
        s0 = jnp.zeros((batch, B_HEADS, B_KEY_DIM, B_VAL_DIM), F32)
        ob, s_fin = _gla(r3(qb), r3(kb), r3(lg), r3(vb), s0, chunk=64)
        yp = _merge(yp, oa.reshape(batch * t, A_WIDTH), sza, ob.reshape(batch * t, 512), szb, w)
        kp_l.append(jnp.transpose(kt.reshape(batch, A_HEADS, A_HEAD_DIM, t), (0, 3, 1, 2)))
        vp_l.append(jnp.transpose(vt.reshape(batch, A_HEADS, A_HEAD_DIM, t), (0, 3, 1, 2)))
        sp_l.append(s_fin)
        qs, ks, vs, sza, szb, qb, kb, vb, lg = _proj(ys, w, nb_s, sample=True)
        h4 = lambda a: jnp.transpose(a.reshape(nb_s, t_new, A_HEADS, A_HEAD_DIM), (0, 2, 1, 3))
        slope_rows = jnp.broadcast_to(jnp.repeat(w["slopes"], t_new)[:, None], (A_HEADS * t_new, LANE))
        oa4 = _moba_sample(h4(qs), h4(ks), h4(vs), cache_kt[l:l + 1], cache_vt[l:l + 1],
                           page_table, slope_rows)
        oa = jnp.transpose(oa4, (0, 2, 1, 3)).reshape(nb_s * t_new, A_WIDTH)
        r3 = lambda a: a.reshape(nb_s, t_new, 512)
        ob, s_new = _gla(r3(qb), r3(kb), r3(lg), r3(vb), state_gla[l], chunk=t_new)
        ys = _merge(ys, oa, sza, ob.reshape(nb_s * t_new, 512), szb, w)
        ks_l.append(ks.reshape(nb_s, t_new, A_HEADS, A_HEAD_DIM))
        vs_l.append(vs.reshape(nb_s, t_new, A_HEADS, A_HEAD_DIM))
        ss_l.append(s_new)
    return (yp.reshape(batch, t, D_MODEL), ys.reshape(nb_s, t_new, D_MODEL),
            jnp.stack(kp_l), jnp.stack(vp_l), jnp.stack(sp_l),
            jnp.stack(ks_l), jnp.stack(vs_l), jnp.stack(ss_l))
```

```python
import functools

import jax
import jax.numpy as jnp
import numpy as np
from jax import lax
from jax.experimental import pallas as pl
from jax.experimental.pallas import tpu as pltpu

F32 = jnp.float32
BF16 = jnp.bfloat16

D_MODEL = 1024
A_HEADS = 8
A_HEAD_DIM = 64
A_WIDTH = A_HEADS * A_HEAD_DIM
MOBA_BLOCK = 256
MOBA_TOPK = 3
B_HEADS = 4
B_KEY_DIM = 64
B_VAL_DIM = 128
B_KEY_WIDTH = B_HEADS * B_KEY_DIM
B_VAL_WIDTH = B_HEADS * B_VAL_DIM
GATE_RANK = 16
GATE_TAU = 16.0
PAGE_SIZE = 128
EPS = 1e-6
NEG = -1e30

LANE = 128
VMEM_LIMIT = 56 * 1024 * 1024

_OFF = np.cumsum([0, A_WIDTH, A_WIDTH, A_WIDTH, A_WIDTH, B_KEY_WIDTH, B_KEY_WIDTH,
                  B_VAL_WIDTH, B_VAL_WIDTH, GATE_RANK]).tolist()
_ROW_W = 5 * 512 + LANE
_AUG0 = 128
_MASK0 = 136
_LOG2E = 1.4426950408889634


def _dot(a, b, dims=(((1,), (0,)), ((), ())), precision=None):
    return lax.dot_general(a, b, dims, precision=precision, preferred_element_type=F32)


_NT = (((1,), (1,)), ((), ()))
_TN = (((0,), (0,)), ((), ()))


def _silu(x):
    return x / (1.0 + jnp.exp(-x))


def _log_sigmoid(x):
    return jnp.minimum(x, 0.0) - jnp.log1p(jnp.exp(-jnp.abs(x)))


def _proj_kernel(x_ref, gpre_ref, wt_ref, wrow_ref, wa2_ref, ba_ref, gq_ref, gk_ref, *out_refs,
                 tm, sample):
    x = x_ref[...]
    ms = jnp.mean(x * x, axis=-1, keepdims=True)
    h = ((x * lax.rsqrt(ms + EPS)) * gpre_ref[...]).astype(BF16)

    pt = _dot(wt_ref[...], h, _NT)

    def head_norm(t, g):
        t3 = t.reshape(A_HEADS, A_HEAD_DIM, tm)
        ss = jnp.mean(t3 * t3, axis=1, keepdims=True)
        return (t3 * lax.rsqrt(ss + EPS)).reshape(A_WIDTH, tm) * g

    q_t = head_norm(pt[0:A_WIDTH], gq_ref[...]) * (A_HEAD_DIM ** -0.5)
    k_t = head_norm(pt[A_WIDTH:2 * A_WIDTH], gk_ref[...])
    v_t = pt[2 * A_WIDTH:3 * A_WIDTH]

    def seg(i0, i1):
        return _dot(h, wrow_ref[:, i0:i1])

    sza = _silu(seg(0, 512)).astype(BF16)
    szb = _silu(seg(512, 1024)).astype(BF16)
    qb = seg(1024, 1536) * (B_KEY_DIM ** -0.5)
    kb = seg(1536, 2048)
    vb = seg(2048, 2560)
    ab = seg(2560, 2688).astype(BF16)
    pre = _dot(ab, wa2_ref[...]) + ba_ref[...]
    lg = _log_sigmoid(pre) * (_LOG2E / GATE_TAU)

    if sample:
        (q_ref, k_ref, v_ref, sza_ref, szb_ref, qb_ref, kb_ref, vb_ref, lg_ref) = out_refs
        q_ref[...] = q_t.T
        k_ref[...] = k_t.T
        v_ref[...] = v_t.T
    else:
        (qt_ref, kt_ref, vt_ref, krow_ref, kmean_ref,
         sza_ref, szb_ref, qb_ref, kb_ref, vb_ref, lg_ref) = out_refs
        qt_ref[0] = q_t
        kt_ref[0] = k_t
        vt_ref[0] = v_t
        k_row = k_t.T
        krow_ref[0] = k_row.astype(BF16)
        kmean_ref[0] = jnp.mean(k_row, axis=0, keepdims=True)
    sza_ref[...] = sza
    szb_ref[...] = szb
    qb_ref[...] = qb
    kb_ref[...] = kb
    vb_ref[...] = vb
    lg_ref[...] = lg


def _proj(x2d, w, batch, sample):
    n = x2d.shape[0]
    tm = MOBA_BLOCK
    nt = n // tm
    const = lambda i: (0, 0)
    row = lambda i: (i, 0)
    in_specs = [
        pl.BlockSpec((tm, D_MODEL), row),
        pl.BlockSpec((1, D_MODEL), const),
        pl.BlockSpec((3 * A_WIDTH, D_MODEL), const),
        pl.BlockSpec((D_MODEL, _ROW_W), const),
        pl.BlockSpec((LANE, 512), const),
        pl.BlockSpec((1, 512), const),
        pl.BlockSpec((A_WIDTH, tm), const),
        pl.BlockSpec((A_WIDTH, tm), const),
    ]
    row_specs = [pl.BlockSpec((tm, 512), row)] * 6
    row_shapes = [jax.ShapeDtypeStruct((n, 512), BF16)] * 2 + [jax.ShapeDtypeStruct((n, 512), F32)] * 4
    if sample:
        out_specs = [pl.BlockSpec((tm, 512), row)] * 3 + row_specs
        out_shape = [jax.ShapeDtypeStruct((n, 512), F32)] * 3 + row_shapes
    else:
        t = n // batch
        tpb = t // tm
        feat = lambda i: (i // tpb, 0, i % tpb)
        out_specs = ([pl.BlockSpec((1, A_WIDTH, tm), feat)] * 3
                     + [pl.BlockSpec((1, tm, 512), lambda i: (i // tpb, i % tpb, 0)),
                        pl.BlockSpec((1, 1, 512), lambda i: (i, 0, 0))]
                     + row_specs)
        out_shape = ([jax.ShapeDtypeStruct((batch, A_WIDTH, t), F32)] * 3
                     + [jax.ShapeDtypeStruct((batch, t, 512), BF16),
                        jax.ShapeDtypeStruct((nt, 1, 512), F32)]
                     + row_shapes)
    return pl.pallas_call(
        functools.partial(_proj_kernel, tm=tm, sample=sample),
        grid=(nt,),
        in_specs=in_specs,
        out_specs=out_specs,
        out_shape=out_shape,
        compiler_params=pltpu.CompilerParams(dimension_semantics=("arbitrary",),
                                             vmem_limit_bytes=VMEM_LIMIT),
        name="proj_sample" if sample else "proj_prompt",
    )(x2d, w["g_pre"], w["w_t"], w["w_row"], w["w_a2p"], w["b_ap"], w["gq_t"], w["gk_t"])


def _top3_rows(g, n):
    idx = lax.broadcasted_iota(jnp.int32, g.shape, 0)
    sel = jnp.zeros(g.shape, F32)
    for _ in range(MOBA_TOPK):
        m = jnp.max(g, axis=0, keepdims=True)
        first = jnp.min(jnp.where(g == m, idx, n), axis=0, keepdims=True)
        pick = idx == first
        sel = jnp.where(pick, 1.0, sel)
        g = jnp.where(pick, -jnp.inf, g)
    return sel


_QB = 4


def _moba_prompt_kernel(qt_ref, krow_ref, vt_ref, kmean_ref, qaug_ref, kaug_ref, o_ref,
                        qop_ref, acc_ref, m_ref, s_ref, cm_ref, *, nb):
    blk = MOBA_BLOCK
    qw = _QB * blk
    i0 = pl.program_id(2) * _QB
    qt = qt_ref[0]
    row128 = lax.broadcasted_iota(jnp.int32, (2 * A_HEAD_DIM, qw), 0)
    lane128 = lax.broadcasted_iota(jnp.int32, (nb, LANE), 1)
    bidx = lax.broadcasted_iota(jnp.int32, (nb, qw), 0)
    iq = i0 + lax.broadcasted_iota(jnp.int32, (nb, qw), 1) // blk
    kmean = kmean_ref[0]

    for hh in range(2):
        in_head = (row128 >= hh * A_HEAD_DIM) & (row128 < (hh + 1) * A_HEAD_DIM)
        q_h = jnp.where(in_head, qt * _LOG2E, 0.0)
        km_h = jnp.where((lane128 >= hh * A_HEAD_DIM) & (lane128 < (hh + 1) * A_HEAD_DIM), kmean, 0.0)
        gate = _dot(km_h, qt, precision=lax.Precision.HIGHEST)
        past = bidx < iq
        sel = _top3_rows(jnp.where(past, gate, NEG), nb)
        keep = ((sel > 0.5) & past) | (bidx == iq)
        maskbias = jnp.where(keep, 0.0, NEG)
        for qb in range(_QB):
            c = hh * _QB + qb
            cols = slice(qb * blk, (qb + 1) * blk)
            qop = jnp.concatenate(
                [q_h[:, cols], qaug_ref[0, c], maskbias[:, cols],
                 jnp.zeros((2 * LANE - _MASK0 - nb, blk), F32)], axis=0)
            qop_ref[c] = qop.astype(BF16)
            acc_ref[c] = jnp.zeros(acc_ref.shape[1:], F32)
            m_ref[c] = jnp.full(m_ref.shape[1:], -jnp.inf, F32)

    lane_k = lax.broadcasted_iota(jnp.int32, (blk, LANE), 1)
    kaug_base = kaug_ref[...]
    ones_rows = jnp.ones((8, blk), BF16)
    kk = lax.broadcasted_iota(jnp.int32, (blk, blk), 0)
    qq = lax.broadcasted_iota(jnp.int32, (blk, blk), 1)

    def scores(j, slot, chains):
        off = pl.multiple_of(j * blk, blk)
        shift = ((j - i0) * blk).astype(F32)
        kaug = jnp.where((lane_k >= 3) & (lane_k < 6), shift.astype(BF16),
                         jnp.where(lane_k == 8 + j, jnp.ones((), BF16), kaug_base))
        kop = jnp.concatenate([krow_ref[0, pl.ds(off, blk), :], kaug], axis=1)
        for c in chains:
            s = _dot(kop, qop_ref[c])
            s_ref[slot, c] = s
            cm_ref[slot, c] = jnp.max(s.reshape(blk // 8, 8, blk), axis=0)

    def absorb(j, slot, chains):
        off = pl.multiple_of(j * blk, blk)
        vops = []
        for hh in range(2):
            v_t = vt_ref[0, hh * A_HEAD_DIM:(hh + 1) * A_HEAD_DIM, pl.ds(off, blk)].astype(BF16)
            vops.append(jnp.concatenate([v_t, ones_rows], axis=0))
        for c, causal in chains:
            s = s_ref[slot, c]
            if causal:
                s = jnp.where(kk <= qq, s, NEG)
            m_prev = m_ref[c][0:1]
            cmax = s if causal else cm_ref[slot, c]
            m_new = jnp.maximum(m_prev, jnp.max(cmax, axis=0, keepdims=True))
            alpha = jnp.exp2(m_prev - m_new)
            p = jnp.exp2(s - m_new).astype(BF16)
            acc_ref[c] = alpha * acc_ref[c] + _dot(vops[c // _QB], p)
            m_ref[c] = jnp.broadcast_to(m_new, m_ref.shape[1:])

    all_c = list(range(2 * _QB))
    plain = [(c, False) for c in all_c]
    scores(0, 0, all_c)

    def body(jj, carry):
        j = _QB * jj
        for d in range(_QB):
            scores(j + d + 1, (d + 1) % 2, all_c)
            absorb(j + d, d % 2, plain)
        return carry

    lax.fori_loop(0, i0 // _QB, body, 0)
    for d in range(_QB):
        if d + 1 < _QB:
            scores(i0 + d + 1, (d + 1) % 2, [c for c in all_c if c % _QB >= d + 1])
        absorb(i0 + d, d % 2, [(c, c % _QB == d) for c in all_c if c % _QB >= d])

    for qb in range(_QB):
        outs = []
        for hh in range(2):
            acc = acc_ref[hh * _QB + qb]
            outs.append(acc[0:A_HEAD_DIM] / acc[A_HEAD_DIM:A_HEAD_DIM + 1])
        o_ref[0, qb * blk:(qb + 1) * blk, :] = jnp.concatenate(outs, axis=0).T.astype(o_ref.dtype)


def _moba_prompt(qt, krow, vt, kmean, w):
    batch, _, t = qt.shape
    nb = t // MOBA_BLOCK
    blk = MOBA_BLOCK
    qw = _QB * blk
    return pl.pallas_call(
        functools.partial(_moba_prompt_kernel, nb=nb),
        grid=(batch, A_HEADS // 2, nb // _QB),
        in_specs=[
            pl.BlockSpec((1, 2 * A_HEAD_DIM, qw), lambda b, hp, i: (b, hp, i)),
            pl.BlockSpec((1, t, LANE), lambda b, hp, i: (b, 0, hp)),
            pl.BlockSpec((1, 2 * A_HEAD_DIM, t), lambda b, hp, i: (b, hp, 0)),
            pl.BlockSpec((1, nb, LANE), lambda b, hp, i: (b, 0, hp)),
            pl.BlockSpec((1, 2 * _QB, 8, blk), lambda b, hp, i: (hp, 0, 0, 0)),
            pl.BlockSpec((blk, LANE), lambda b, hp, i: (0, 0)),
        ],
        out_specs=pl.BlockSpec((1, qw, LANE), lambda b, hp, i: (b, i, hp)),
        out_shape=jax.ShapeDtypeStruct((batch, t, A_WIDTH), BF16),
        scratch_shapes=[
            pltpu.VMEM((2 * _QB, 2 * LANE, blk), BF16),
            pltpu.VMEM((2 * _QB, A_HEAD_DIM + 8, blk), F32),
            pltpu.VMEM((2 * _QB, 8, blk), F32),
            pltpu.VMEM((2, 2 * _QB, blk, blk), F32),
            pltpu.VMEM((2, 2 * _QB, 8, blk), F32),
        ],
        compiler_params=pltpu.CompilerParams(
            dimension_semantics=("arbitrary", "arbitrary", "arbitrary"),
            vmem_limit_bytes=VMEM_LIMIT),
        name="moba_prompt",
    )(qt, krow, vt, kmean.reshape(batch, nb, A_WIDTH), w["qaug"], w["kaug"])


def _moba_sample_kernel(pt_ref, q_ref, kn_ref, vn_ref, slope_ref, ck_hbm, cv_hbm, o_ref,
                        kbuf, sall, gate_ref, idx_v, idx_s, vbuf, sem_k, sem_v, sem_i,
                        *, n_pages, npg, ring):
    b = pl.program_id(0)
    n_chunks = n_pages // npg
    cw = npg * PAGE_SIZE
    bpc = cw // MOBA_BLOCK
    past = n_pages * PAGE_SIZE
    n_blocks = past // MOBA_BLOCK
    t_new = q_ref.shape[2]
    rows = A_HEADS * t_new

    def k_copy(bb, c, p, slot):
        page = pt_ref[bb, c * npg + p]
        return pltpu.make_async_copy(
            ck_hbm.at[0, page], kbuf.at[slot, :, :, pl.ds(p * PAGE_SIZE, PAGE_SIZE)], sem_k.at[slot])

    def start_chunk(bb, c, slot):
        for p in range(npg):
            k_copy(bb, c, p, slot).start()

    def wait_chunk(c, slot):
        for p in range(npg):
            k_copy(b, c, p, slot).wait()

    @pl.when(b == 0)
    def _():
        for c0 in range(ring):
            start_chunk(0, c0, c0)

    q_ops = []
    for h in range(A_HEADS):
        qh = q_ref[0, h]
        hi = qh.astype(BF16)
        lo = (qh - hi.astype(F32)).astype(BF16)
        q_ops.append(jnp.concatenate([hi, lo], axis=0))

    gate_ref[...] = jnp.zeros(gate_ref.shape, F32)
    lane_g = lax.broadcasted_iota(jnp.int32, (t_new, LANE), 1)

    def chunk_step(c, refill):
        slot = c % ring
        wait_chunk(c, slot)
        off = c * cw if isinstance(c, int) else pl.multiple_of(c * cw, cw)
        for h in range(A_HEADS):
            kt = kbuf[slot, h].astype(BF16)
            s2 = _dot(q_ops[h], kt)
            s = s2[0:t_new] + s2[t_new:2 * t_new]
            sall[h * t_new:(h + 1) * t_new, pl.ds(off, cw)] = s
            g = gate_ref[h * t_new:(h + 1) * t_new, :]
            for jb in range(bpc):
                t2 = (s[:, jb * MOBA_BLOCK:jb * MOBA_BLOCK + LANE]
                      + s[:, jb * MOBA_BLOCK + LANE:(jb + 1) * MOBA_BLOCK])
                r = jnp.sum(t2, axis=1, keepdims=True)
                g = jnp.where(lane_g == c * bpc + jb, r, g)
            gate_ref[h * t_new:(h + 1) * t_new, :] = g
        if refill:
            start_chunk(b, c + ring, slot)

    def refill_body(c, carry):
        chunk_step(c, True)
        return carry

    def drain_body(c, carry):
        chunk_step(c, False)
        return carry

    lax.fori_loop(0, n_chunks - ring, refill_body, 0)
    lax.fori_loop(n_chunks - ring, n_chunks, drain_body, 0)

    @pl.when(b + 1 < pl.num_programs(0))
    def _():
        for c0 in range(ring):
            start_chunk(b + 1, c0, c0)

    lane_r = lax.broadcasted_iota(jnp.int32, (rows, LANE), 1)
    g = jnp.where(lane_r < n_blocks, gate_ref[...], -jnp.inf)
    idx = jnp.zeros((rows, LANE), jnp.int32)
    for r in range(MOBA_TOPK):
        m = jnp.max(g, axis=1, keepdims=True)
        first = jnp.min(jnp.where(g == m, lane_r, LANE), axis=1, keepdims=True)
        g = jnp.where(lane_r == first, -jnp.inf, g)
        idx = jnp.where(lane_r == r, first, idx)
    idx_v[...] = idx
    cp_i = pltpu.make_async_copy(idx_v, idx_s, sem_i)
    cp_i.start()
    cp_i.wait()

    picks = [(t, r) for t in range(t_new) for r in range(MOBA_TOPK)]

    def v_copy(h, t, r, pg):
        bsel = idx_s[h * t_new + t, r]
        page = pt_ref[b, 2 * bsel + pg]
        col = ((t * MOBA_TOPK + r) * 2 + pg) * PAGE_SIZE
        return pltpu.make_async_copy(
            cv_hbm.at[0, page, h], vbuf.at[h, :, pl.ds(col, PAGE_SIZE)], sem_v.at[h])

    def v_start(h):
        for t, r in picks:
            for pg in range(2):
                v_copy(h, t, r, pg).start(priority=1)

    def v_wait(h):
        for t, r in picks:
            for pg in range(2):
                v_copy(h, t, r, pg).wait()

    sub = lax.broadcasted_iota(jnp.int32, (t_new, MOBA_BLOCK), 0)
    klane = lax.broadcasted_iota(jnp.int32, (t_new, MOBA_BLOCK), 1)
    trow = lax.broadcasted_iota(jnp.int32, (t_new, t_new), 0)
    tcol = lax.broadcasted_iota(jnp.int32, (t_new, t_new), 1)
    v_ahead = 2
    for h0 in range(v_ahead):
        v_start(h0)

    def head_step(h, prefetch):
        if prefetch:
            v_start(h + v_ahead)
        r0 = h * t_new if isinstance(h, int) else pl.multiple_of(h * t_new, t_new)
        slope = slope_ref[pl.ds(r0, t_new), :][:, 0:1]
        pieces = []
        for t, r in picks:
            bsel = idx_s[r0 + t, r]
            off = pl.multiple_of(bsel * MOBA_BLOCK, MOBA_BLOCK)
            dist = ((past + t - bsel * MOBA_BLOCK) - klane).astype(F32)
            sc = sall[pl.ds(r0, t_new), pl.ds(off, MOBA_BLOCK)] - slope * dist
            pieces.append(jnp.where(sub == t, sc, NEG))
        s_sel = jnp.concatenate(pieces, axis=1)
        qh = q_ref[0, h]
        s_new = _dot(qh, kn_ref[0, h], _NT) - slope * (trow - tcol).astype(F32)
        s_new = jnp.where(tcol <= trow, s_new, NEG)
        m = jnp.maximum(jnp.max(s_sel, axis=1, keepdims=True), jnp.max(s_new, axis=1, keepdims=True))
        p_sel = jnp.exp(s_sel - m)
        p_new = jnp.exp(s_new - m)
        l = jnp.sum(p_sel, axis=1, keepdims=True) + jnp.sum(p_new, axis=1, keepdims=True)
        v_wait(h)
        acc = _dot(p_sel.astype(BF16), vbuf[h].astype(BF16), _NT) + _dot(p_new, vn_ref[0, h])
        o_ref[0, h] = acc / l

    def head_body(h, carry):
        head_step(h, True)
        return carry

    lax.fori_loop(0, A_HEADS - v_ahead, head_body, 0)
    for h0 in range(A_HEADS - v_ahead, A_HEADS):
        head_step(h0, False)


def _moba_sample(q4, kn4, vn4, cache_kt, cache_vt, page_table, slope_rows):
    nbatch, _, t_new, _ = q4.shape
    n_pages = page_table.shape[1]
    npg = 16
    ring = min(4, n_pages // npg)
    rows = A_HEADS * t_new
    past = n_pages * PAGE_SIZE
    blk4 = pl.BlockSpec((1, A_HEADS, t_new, A_HEAD_DIM), lambda b, pt: (b, 0, 0, 0))
    return pl.pallas_call(
        functools.partial(_moba_sample_kernel, n_pages=n_pages, npg=npg, ring=ring),
        grid_spec=pltpu.PrefetchScalarGridSpec(
            num_scalar_prefetch=1,
            grid=(nbatch,),
            in_specs=[blk4, blk4, blk4,
                      pl.BlockSpec((rows, LANE), lambda b, pt: (0, 0)),
                      pl.BlockSpec(memory_space=pl.ANY),
                      pl.BlockSpec(memory_space=pl.ANY)],
            out_specs=blk4,
            scratch_shapes=[
                pltpu.VMEM((ring, A_HEADS, A_HEAD_DIM, npg * PAGE_SIZE), F32),
                pltpu.VMEM((rows, past), F32),
                pltpu.VMEM((rows, LANE), F32),
                pltpu.VMEM((rows, LANE), jnp.int32),
                pltpu.SMEM((rows, LANE), jnp.int32),
                pltpu.VMEM((A_HEADS, A_HEAD_DIM, t_new * MOBA_TOPK * MOBA_BLOCK), F32),
                pltpu.SemaphoreType.DMA((ring,)),
                pltpu.SemaphoreType.DMA((A_HEADS,)),
                pltpu.SemaphoreType.DMA,
            ]),
        out_shape=jax.ShapeDtypeStruct((nbatch, A_HEADS, t_new, A_HEAD_DIM), F32),
        compiler_params=pltpu.CompilerParams(dimension_semantics=("arbitrary",),
                                             vmem_limit_bytes=VMEM_LIMIT),
        name="moba_sample",
    )(page_table, q4, kn4, vn4, slope_rows, cache_kt, cache_vt)


def _gla_kernel(q_ref, k_ref, g_ref, v_ref, s0_ref, o_ref, sfin_ref, state, *, chunk, n_chunks, nbb):
    c = chunk
    mx = BF16 if c >= 16 else F32

    @pl.when(pl.program_id(2) == 0)
    def _():
        for bi in range(nbb):
            state[bi] = jnp.concatenate([s0_ref[bi, 0], jnp.zeros((LANE - B_KEY_DIM, B_VAL_DIM), F32)],
                                        axis=0).T

    ri = lax.broadcasted_iota(jnp.int32, (c, LANE), 0)
    ti = lax.broadcasted_iota(jnp.int32, (c, c), 0)
    si = lax.broadcasted_iota(jnp.int32, (c, c), 1)
    tril = jnp.where(si <= ti, 1.0, 0.0).astype(mx)
    sub3 = lax.broadcasted_iota(jnp.int32, (c // 8, 8, LANE), 1)
    levels = []
    half = c // 2
    while half >= 1:
        width = 2 * half
        sb = int(np.log2(width))
        upper = (ri & (width - 1)) >= half
        pair = ((ti >> sb) == (si >> sb)) & ((ti & (width - 1)) >= half) & ((si & (width - 1)) < half)
        levels.append((half, width, upper, pair))
        half //= 2

    seq = [(bi, ci) for bi in range(nbb) for ci in range(n_chunks)]
    loaded = []
    for bi, ci in seq:
        g = g_ref[bi, pl.ds(ci * c, c), :]
        g_hi = g.astype(BF16)
        g_lo = (g - g_hi.astype(F32)).astype(BF16)
        loaded.append((g, _dot(tril, g_hi.astype(mx)) + _dot(tril, g_lo.astype(mx))))

    prepared = []
    for (bi, ci), (g, b) in zip(seq, loaded):
        rows = pl.ds(ci * c, c)
        q = q_ref[bi, rows, :]
        k = k_ref[bi, rows, :]
        v32 = v_ref[bi, rows, :]
        b_last = b[c - 1:c, :]
        b3 = b.reshape(c // 8, 8, LANE)
        zs = []
        for half, width, upper, _ in levels:
            if half == 1:
                x = jnp.where(upper, g, 0.0)
            else:
                if half >= 8:
                    mid = jnp.concatenate(
                        [jnp.broadcast_to(b[m * width + half - 1:m * width + half, :], (width, LANE))
                         for m in range(c // width)], axis=0)
                else:
                    mid3 = jnp.broadcast_to(b3[:, half - 1:half, :], b3.shape)
                    for m in range(1, 8 // width):
                        r = m * width + half - 1
                        mid3 = jnp.where(sub3 >= m * width,
                                         jnp.broadcast_to(b3[:, r:r + 1, :], b3.shape), mid3)
                    mid = mid3.reshape(c, LANE)
                x = jnp.where(upper, b - mid, mid - b)
            zs.append((jnp.where(upper, q, k) * jnp.exp2(x)).astype(mx))
        qd = (q * jnp.exp2(b)).astype(mx)
        kd = (k * jnp.exp2(b_last - b)).astype(mx)
        o_same = jnp.sum(q * k, axis=1, keepdims=True) * v32
        prepared.append((v32.astype(mx), zs, qd, kd, jnp.exp2(b_last), o_same))

    grams, updates = [], []
    for v, zs, qd, kd, a_last, o_same in prepared:
        grams.append([_dot(z, z, _NT) for z in zs])
        updates.append(_dot(v, kd, _TN))

    attns = []
    for gr in grams:
        attn = jnp.where(levels[0][3], gr[0], 0.0)
        for (_, _, _, pair), gm in zip(levels[1:], gr[1:]):
            attn = attn + jnp.where(pair, gm, 0.0)
        attns.append(attn.astype(mx))

    states = []
    for n, (bi, ci) in enumerate(seq):
        st = state[bi] if ci == 0 else states[-1][1]
        states.append((st, prepared[n][4] * st + updates[n]))
        if ci == n_chunks - 1:
            state[bi] = states[-1][1]

    for n, (bi, ci) in enumerate(seq):
        v, zs, qd, kd, a_last, o_same = prepared[n]
        o_ref[bi, pl.ds(ci * c, c), :] = (o_same + _dot(attns[n], v)
                                          + _dot(qd, states[n][0].astype(mx), _NT))

    @pl.when(pl.program_id(2) == pl.num_programs(2) - 1)
    def _():
        for bi in range(nbb):
            sfin_ref[bi, 0] = state[bi].T[0:B_KEY_DIM, :]


def _gla(qb, kb, lg, vb, s0, chunk):
    batch, t, _ = qb.shape
    tc = min(t, 1024)
    n_chunks = tc // chunk
    nbb = 8 if (t == tc and batch % 8 == 0) else 1
    seq = pl.BlockSpec((nbb, tc, LANE), lambda b, h, c: (b, c, h))
    st = pl.BlockSpec((nbb, 1, B_KEY_DIM, B_VAL_DIM), lambda b, h, c: (b, h, 0, 0))
    return pl.pallas_call(
        functools.partial(_gla_kernel, chunk=chunk, n_chunks=n_chunks, nbb=nbb),
        grid=(batch // nbb, B_HEADS, t // tc),
        in_specs=[seq, seq, seq, seq, st],
        out_specs=[seq, st],
        out_shape=[jax.ShapeDtypeStruct((batch, t, B_VAL_WIDTH), F32),
                   jax.ShapeDtypeStruct((batch, B_HEADS, B_KEY_DIM, B_VAL_DIM), F32)],
        scratch_shapes=[pltpu.VMEM((nbb, LANE, B_VAL_DIM), F32)],
        compiler_params=pltpu.CompilerParams(
            dimension_semantics=("arbitrary", "arbitrary", "arbitrary"),
            vmem_limit_bytes=VMEM_LIMIT),
        name="gla_c%d" % chunk,
    )(qb, kb, lg, vb, s0)


def _merge_kernel(x_ref, oa_ref, sza_ref, ob_ref, szb_ref, gg_ref, wo_ref, y_ref):
    ya = oa_ref[...].astype(F32) * sza_ref[...].astype(F32)
    ob = ob_ref[...]
    parts = []
    for h in range(B_HEADS):
        oh = ob[:, h * B_VAL_DIM:(h + 1) * B_VAL_DIM]
        ms = jnp.mean(oh * oh, axis=-1, keepdims=True)
        parts.append(oh * lax.rsqrt(ms + EPS))
    yb = (jnp.concatenate(parts, axis=1) * gg_ref[...]) * szb_ref[...].astype(F32)
    cat = jnp.concatenate([ya, yb], axis=1).astype(BF16)
    y_ref[...] = x_ref[...] + _dot(cat, wo_ref[...])


def _merge(x2d, oa, sza, ob, szb, w):
    n = x2d.shape[0]
    tm = min(n, 512)
    row = lambda i: (i, 0)
    const = lambda i: (0, 0)
    return pl.pallas_call(
        _merge_kernel,
        grid=(n // tm,),
        in_specs=[pl.BlockSpec((tm, D_MODEL), row), pl.BlockSpec((tm, 512), row),
                  pl.BlockSpec((tm, 512), row), pl.BlockSpec((tm, 512), row),
                  pl.BlockSpec((tm, 512), row), pl.BlockSpec((1, 512), const),
                  pl.BlockSpec((D_MODEL, D_MODEL), const)],
        out_specs=pl.BlockSpec((tm, D_MODEL), row),
        out_shape=jax.ShapeDtypeStruct((n, D_MODEL), F32),
        compiler_params=pltpu.CompilerParams(dimension_semantics=("arbitrary",),
                                             vmem_limit_bytes=VMEM_LIMIT),
        name="merge_out",
    )(x2d, oa, sza, ob, szb, w["g_gla"], w["w_out"])


def _pad_heads(a, n_heads, dim):
    lead = a.shape[:-1]
    a = a.reshape(lead + (n_heads, dim))
    a = jnp.pad(a, [(0, 0)] * len(lead) + [(0, 0), (0, LANE - dim)])
    return a.reshape(lead + (n_heads * LANE,))


def _layer_weights(g_pre, w_in, g_q, g_k, w_a2, b_a, g_gla, w_out):
    o = _OFF
    w_t = w_in[:, o[0]:o[3]].T.astype(BF16)
    w_row = jnp.concatenate([
        w_in[:, o[3]:o[4]], w_in[:, o[7]:o[8]],
        _pad_heads(w_in[:, o[4]:o[5]], B_HEADS, B_KEY_DIM),
        _pad_heads(w_in[:, o[5]:o[6]], B_HEADS, B_KEY_DIM),
        w_in[:, o[6]:o[7]],
        jnp.pad(w_in[:, o[8]:o[9]], ((0, 0), (0, LANE - GATE_RANK)))], axis=1).astype(BF16)
    w_a2p = jnp.pad(_pad_heads(w_a2, B_HEADS, B_KEY_DIM), ((0, LANE - GATE_RANK), (0, 0))).astype(BF16)
    b_ap = _pad_heads(b_a, B_HEADS, B_KEY_DIM).reshape(1, 512)
    gq_t = jnp.broadcast_to(jnp.tile(g_q, A_HEADS)[:, None], (A_WIDTH, MOBA_BLOCK))
    gk_t = jnp.broadcast_to(jnp.tile(g_k, A_HEADS)[:, None], (A_WIDTH, MOBA_BLOCK))
    slopes = jnp.asarray([2.0 ** (-8.0 * (h + 1) / A_HEADS) for h in range(A_HEADS)], F32)
    c = slopes * _LOG2E
    pieces = []
    rem = c
    for _ in range(3):
        pc = rem.astype(BF16).astype(F32)
        pieces.append(pc)
        rem = rem - pc
    qq = jnp.arange(MOBA_BLOCK, dtype=F32)
    qaug = jnp.zeros((A_HEADS, _QB, 8, MOBA_BLOCK), F32)
    for p_i, pc in enumerate(pieces):
        qaug = qaug.at[:, :, p_i, :].set(pc[:, None, None])
        qaug = qaug.at[:, :, 3 + p_i, :].set(pc[:, None, None])
    qpos = qq[None, None, :] + (MOBA_BLOCK * jnp.arange(_QB, dtype=F32))[None, :, None]
    qaug = qaug.at[:, :, 6, :].set(-c[:, None, None] * qpos)
    qaug = qaug.reshape(A_HEADS // 2, 2 * _QB, 8, MOBA_BLOCK)
    kaug = jnp.zeros((MOBA_BLOCK, LANE), F32)
    kaug = kaug.at[:, 0:3].set(qq[:, None]).at[:, 6].set(1.0).astype(BF16)
    return {
        "g_pre": g_pre.reshape(1, D_MODEL), "w_t": w_t, "w_row": w_row, "w_a2p": w_a2p, "b_ap": b_ap,
        "gq_t": gq_t, "gk_t": gk_t, "g_gla": g_gla.reshape(1, B_VAL_WIDTH),
        "w_out": w_out.astype(BF16), "qaug": qaug, "kaug": kaug, "slopes": slopes,
    }


def kernel(x_prompt, x_sample, cache_k, cache_v, state_gla, page_table, g_pre, w_in, g_q, g_k, w_a2, b_a, g_gla, w_out):
    depth = g_pre.shape[0]
    batch, t, _ = x_prompt.shape
    nb_s, t_new, _ = x_sample.shape
    cache_kt = jnp.transpose(cache_k, (0, 1, 3, 4, 2))
    cache_vt = jnp.transpose(cache_v, (0, 1, 3, 4, 2))
    yp = x_prompt.reshape(batch * t, D_MODEL)
    ys = x_sample.reshape(nb_s * t_new, D_MODEL)
    kp_l, vp_l, sp_l, ks_l, vs_l, ss_l = [], [], [], [], [], []
    for l in range(depth):
        w = _layer_weights(g_pre[l], w_in[l], g_q[l], g_k[l], w_a2[l], b_a[l], g_gla[l], w_out[l])
        qt, kt, vt, krow, kmean, sza, szb, qb, kb, vb, lg = _proj(yp, w, batch, sample=False)
        oa = _moba_prompt(qt, krow, vt, kmean, w)
        r3 = lambda a: a.reshape(batch, t, 512)
        s0 = jnp.zeros((batch, B_HEADS, B_KEY_DIM, B_VAL_DIM), F32)
        ob, s_fin = _gla(r3(qb), r3(kb), r3(lg), r3(vb), s0, chunk=64)
        yp = _merge(yp, oa.reshape(batch * t, A_WIDTH), sza, ob.reshape(batch * t, 512), szb, w)
        kp_l.append(jnp.transpose(kt.reshape(batch, A_HEADS, A_HEAD_DIM, t), (0, 3, 1, 2)))
        vp_l.append(jnp.transpose(vt.reshape(batch, A_HEADS, A_HEAD_DIM, t), (0, 3, 1, 2)))
        sp_l.append(s_fin)
        qs, ks, vs, sza, szb, qb, kb, vb, lg = _proj(ys, w, nb_s, sample=True)
        h4 = lambda a: jnp.transpose(a.reshape(nb_s, t_new, A_HEADS, A_HEAD_DIM), (0, 2, 1, 3))
        slope_rows = jnp.broadcast_to(jnp.repeat(w["slopes"], t_new)[:, None], (A_HEADS * t_new, LANE))
        oa4 = _moba_sample(h4(qs), h4(ks), h4(vs), cache_kt[l:l + 1], cache_vt[l:l + 1],
                           page_table, slope_rows)
        oa = jnp.transpose(oa4, (0, 2, 1, 3)).reshape(nb_s * t_new, A_WIDTH)
        r3 = lambda a: a.reshape(nb_s, t_new, 512)
        ob, s_new = _gla(r3(qb), r3(kb), r3(lg), r3(vb), state_gla[l], chunk=t_new)
        ys = _merge(ys, oa, sza, ob.reshape(nb_s * t_new, 512), szb, w)
        ks_l.append(ks.reshape(nb_s, t_new, A_HEADS, A_HEAD_DIM))
        vs_l.append(vs.reshape(nb_s, t_new, A_HEADS, A_HEAD_DIM))
        ss_l.append(s_new)
    return (yp.reshape(batch, t, D_MODEL), ys.reshape(nb_s, t_new, D_MODEL),
            jnp.stack(kp_l), jnp.stack(vp_l), jnp.stack(sp_l),
            jnp.stack(ks_l), jnp.stack(vs_l), jnp.stack(ss_l))
```

```python
import functools

import jax
import jax.numpy as jnp
import numpy as np
from jax import lax
from jax.experimental import pallas as pl
from jax.experimental.pallas import tpu as pltpu

F32 = jnp.float32
BF16 = jnp.bfloat16

D_MODEL = 1024
A_HEADS = 8
A_HEAD_DIM = 64
A_WIDTH = A_HEADS * A_HEAD_DIM
MOBA_BLOCK = 256
MOBA_TOPK = 3
B_HEADS = 4
B_KEY_DIM = 64
B_VAL_DIM = 128
B_KEY_WIDTH = B_HEADS * B_KEY_DIM
B_VAL_WIDTH = B_HEADS * B_VAL_DIM
GATE_RANK = 16
GATE_TAU = 16.0
PAGE_SIZE = 128
EPS = 1e-6
NEG = -1e30

LANE = 128
VMEM_LIMIT = 56 * 1024 * 1024

_OFF = np.cumsum([0, A_WIDTH, A_WIDTH, A_WIDTH, A_WIDTH, B_KEY_WIDTH, B_KEY_WIDTH,
                  B_VAL_WIDTH, B_VAL_WIDTH, GATE_RANK]).tolist()
_ROW_W = 5 * 512 + LANE
_AUG0 = 128
_MASK0 = 136
_LOG2E = 1.4426950408889634


def _dot(a, b, dims=(((1,), (0,)), ((), ())), precision=None):
    return lax.dot_general(a, b, dims, precision=precision, preferred_element_type=F32)


_NT = (((1,), (1,)), ((), ()))
_TN = (((0,), (0,)), ((), ()))


def _silu(x):
    return x / (1.0 + jnp.exp(-x))


def _log_sigmoid(x):
    return jnp.minimum(x, 0.0) - jnp.log1p(jnp.exp(-jnp.abs(x)))


def _proj_kernel(x_ref, gpre_ref, wt_ref, wrow_ref, wa2_ref, ba_ref, gq_ref, gk_ref, *out_refs,
                 tm, sample):
    x = x_ref[...]
    ms = jnp.mean(x * x, axis=-1, keepdims=True)
    h = ((x * lax.rsqrt(ms + EPS)) * gpre_ref[...]).astype(BF16)

    pt = _dot(wt_ref[...], h, _NT)

    def head_norm(t, g):
        t3 = t.reshape(A_HEADS, A_HEAD_DIM, tm)
        ss = jnp.mean(t3 * t3, axis=1, keepdims=True)
        return (t3 * lax.rsqrt(ss + EPS)).reshape(A_WIDTH, tm) * g

    q_t = head_norm(pt[0:A_WIDTH], gq_ref[...]) * (A_HEAD_DIM ** -0.5)
    k_t = head_norm(pt[A_WIDTH:2 * A_WIDTH], gk_ref[...])
    v_t = pt[2 * A_WIDTH:3 * A_WIDTH]

    def seg(i0, i1):
        return _dot(h, wrow_ref[:, i0:i1])

    sza = _silu(seg(0, 512)).astype(BF16)
    szb = _silu(seg(512, 1024)).astype(BF16)
    qb = seg(1024, 1536) * (B_KEY_DIM ** -0.5)
    kb = seg(1536, 2048)
    vb = seg(2048, 2560)
    ab = seg(2560, 2688).astype(BF16)
    pre = _dot(ab, wa2_ref[...]) + ba_ref[...]
    lg = _log_sigmoid(pre) * (_LOG2E / GATE_TAU)

    if sample:
        (q_ref, k_ref, v_ref, sza_ref, szb_ref, qb_ref, kb_ref, vb_ref, lg_ref) = out_refs
        q_ref[...] = q_t.T
        k_ref[...] = k_t.T
        v_ref[...] = v_t.T
    else:
        (qt_ref, kt_ref, vt_ref, krow_ref, kmean_ref,
         sza_ref, szb_ref, qb_ref, kb_ref, vb_ref, lg_ref) = out_refs
        qt_ref[0] = q_t
        kt_ref[0] = k_t
        vt_ref[0] = v_t
        k_row = k_t.T
        krow_ref[0] = k_row.astype(BF16)
        kmean_ref[0] = jnp.mean(k_row, axis=0, keepdims=True)
    sza_ref[...] = sza
    szb_ref[...] = szb
    qb_ref[...] = qb
    kb_ref[...] = kb
    vb_ref[...] = vb
    lg_ref[...] = lg


def _proj(x2d, w, batch, sample):
    n = x2d.shape[0]
    tm = MOBA_BLOCK
    nt = n // tm
    const = lambda i: (0, 0)
    row = lambda i: (i, 0)
    in_specs = [
        pl.BlockSpec((tm, D_MODEL), row),
        pl.BlockSpec((1, D_MODEL), const),
        pl.BlockSpec((3 * A_WIDTH, D_MODEL), const),
        pl.BlockSpec((D_MODEL, _ROW_W), const),
        pl.BlockSpec((LANE, 512), const),
        pl.BlockSpec((1, 512), const),
        pl.BlockSpec((A_WIDTH, tm), const),
        pl.BlockSpec((A_WIDTH, tm), const),
    ]
    row_specs = [pl.BlockSpec((tm, 512), row)] * 6
    row_shapes = [jax.ShapeDtypeStruct((n, 512), BF16)] * 2 + [jax.ShapeDtypeStruct((n, 512), F32)] * 4
    if sample:
        out_specs = [pl.BlockSpec((tm, 512), row)] * 3 + row_specs
        out_shape = [jax.ShapeDtypeStruct((n, 512), F32)] * 3 + row_shapes
    else:
        t = n // batch
        tpb = t // tm
        feat = lambda i: (i // tpb, 0, i % tpb)
        out_specs = ([pl.BlockSpec((1, A_WIDTH, tm), feat)] * 3
                     + [pl.BlockSpec((1, tm, 512), lambda i: (i // tpb, i % tpb, 0)),
                        pl.BlockSpec((1, 1, 512), lambda i: (i, 0, 0))]
                     + row_specs)
        out_shape = ([jax.ShapeDtypeStruct((batch, A_WIDTH, t), F32)] * 3
                     + [jax.ShapeDtypeStruct((batch, t, 512), BF16),
                        jax.ShapeDtypeStruct((nt, 1, 512), F32)]
                     + row_shapes)
    return pl.pallas_call(
        functools.partial(_proj_kernel, tm=tm, sample=sample),
        grid=(nt,),
        in_specs=in_specs,
        out_specs=out_specs,
        out_shape=out_shape,
        compiler_params=pltpu.CompilerParams(dimension_semantics=("arbitrary",),
                                             vmem_limit_bytes=VMEM_LIMIT),
        name="proj_sample" if sample else "proj_prompt",
    )(x2d, w["g_pre"], w["w_t"], w["w_row"], w["w_a2p"], w["b_ap"], w["gq_t"], w["gk_t"])


def _top3_rows(g, n):
    idx = lax.broadcasted_iota(jnp.int32, g.shape, 0)
    sel = jnp.zeros(g.shape, F32)
    for _ in range(MOBA_TOPK):
        m = jnp.max(g, axis=0, keepdims=True)
        first = jnp.min(jnp.where(g == m, idx, n), axis=0, keepdims=True)
        pick = idx == first
        sel = jnp.where(pick, 1.0, sel)
        g = jnp.where(pick, -jnp.inf, g)
    return sel


_QB = 4


def _moba_prompt_kernel(qt_ref, krow_ref, vt_ref, kmean_ref, qaug_ref, kaug_ref, o_ref,
                        qop_ref, acc_ref, m_ref, s_ref, cm_ref, *, nb):
    blk = MOBA_BLOCK
    qw = _QB * blk
    i0 = pl.program_id(2) * _QB

    qt = qt_ref[0]
    row128 = lax.broadcasted_iota(jnp.int32, (2 * A_HEAD_DIM, qw), 0)
    lane128 = lax.broadcasted_iota(jnp.int32, (nb, LANE), 1)
    bidx = lax.broadcasted_iota(jnp.int32, (nb, qw), 0)
    iq = i0 + lax.broadcasted_iota(jnp.int32, (nb, qw), 1) // blk
    kmean = kmean_ref[0]

    for hh in range(2):
        in_head = (row128 >= hh * A_HEAD_DIM) & (row128 < (hh + 1) * A_HEAD_DIM)
        q_h = jnp.where(in_head, qt * _LOG2E, 0.0)
        km_h = jnp.where((lane128 >= hh * A_HEAD_DIM) & (lane128 < (hh + 1) * A_HEAD_DIM), kmean, 0.0)
        gate = _dot(km_h, qt, precision=lax.Precision.HIGHEST)
        past = bidx < iq
        sel = _top3_rows(jnp.where(past, gate, NEG), nb)
        keep = ((sel > 0.5) & past) | (bidx == iq)
        maskbias = jnp.where(keep, 0.0, NEG)
        for qb in range(_QB):
            c = hh * _QB + qb
            cols = slice(qb * blk, (qb + 1) * blk)
            qop = jnp.concatenate(
                [q_h[:, cols], qaug_ref[0, c], maskbias[:, cols],
                 jnp.zeros((2 * LANE - _MASK0 - nb, blk), F32)], axis=0)
            qop_ref[c] = qop.astype(BF16)
            acc_ref[c] = jnp.zeros(acc_ref.shape[1:], F32)
            m_ref[c] = jnp.full(m_ref.shape[1:], -jnp.inf, F32)

    lane_k = lax.broadcasted_iota(jnp.int32, (blk, LANE), 1)
    kaug_base = kaug_ref[...]
    ones_rows = jnp.ones((8, blk), BF16)
    kk = lax.broadcasted_iota(jnp.int32, (blk, blk), 0)
    qq = lax.broadcasted_iota(jnp.int32, (blk, blk), 1)

    def scores(j, slot, chains):
        off = pl.multiple_of(j * blk, blk)
        shift = ((j - i0) * blk).astype(F32)
        kaug = jnp.where((lane_k >= 3) & (lane_k < 6), shift.astype(BF16),
                         jnp.where(lane_k == 8 + j, jnp.ones((), BF16), kaug_base))
        kop = jnp.concatenate([krow_ref[0, pl.ds(off, blk), :], kaug], axis=1)
        for c in chains:
            s = _dot(kop, qop_ref[c])
            s_ref[slot, c] = s
            cm_ref[slot, c] = jnp.max(s.reshape(blk // 8, 8, blk), axis=0)

    def absorb(j, slot, chains):
        off = pl.multiple_of(j * blk, blk)
        vops = []
        for hh in range(2):
            v_t = vt_ref[0, hh * A_HEAD_DIM:(hh + 1) * A_HEAD_DIM, pl.ds(off, blk)].astype(BF16)
            vops.append(jnp.concatenate([v_t, ones_rows], axis=0))
        for c, causal in chains:
            s = s_ref[slot, c]
            if causal:
                s = jnp.where(kk <= qq, s, NEG)
            m_prev = m_ref[c][0:1]
            cmax = s if causal else cm_ref[slot, c]
            m_new = jnp.maximum(m_prev, jnp.max(cmax, axis=0, keepdims=True))
            alpha = jnp.exp2(m_prev - m_new)
            p = jnp.exp2(s - m_new).astype(BF16)
            acc_ref[c] = alpha * acc_ref[c] + _dot(vops[c // _QB], p)
            m_ref[c] = jnp.broadcast_to(m_new, m_ref.shape[1:])

    all_c = list(range(2 * _QB))
    plain = [(c, False) for c in all_c]
    scores(0, 0, all_c)

    def body(jj, carry):
        j = _QB * jj
        for d in range(_QB):
            scores(j + d + 1, (d + 1) % 2, all_c)
            absorb(j + d, d % 2, plain)
        return carry

    lax.fori_loop(0, i0 // _QB, body, 0)
    for d in range(_QB):
        if d + 1 < _QB:
            scores(i0 + d + 1, (d + 1) % 2, [c for c in all_c if c % _QB >= d + 1])
        absorb(i0 + d, d % 2, [(c, c % _QB == d) for c in all_c if c % _QB >= d])

    for qb in range(_QB):
        outs = []
        for hh in range(2):
            acc = acc_ref[hh * _QB + qb]
            outs.append(acc[0:A_HEAD_DIM] / acc[A_HEAD_DIM:A_HEAD_DIM + 1])
        o_ref[0, qb * blk:(qb + 1) * blk, :] = jnp.concatenate(outs, axis=0).T.astype(o_ref.dtype)


def _moba_prompt(qt, krow, vt, kmean, w):
    batch, _, t = qt.shape
    nb = t // MOBA_BLOCK
    blk = MOBA_BLOCK
    qw = _QB * blk
    return pl.pallas_call(
        functools.partial(_moba_prompt_kernel, nb=nb),
        grid=(batch, A_HEADS // 2, nb // _QB),
        in_specs=[
            pl.BlockSpec((1, 2 * A_HEAD_DIM, qw), lambda b, hp, i: (b, hp, i)),
            pl.BlockSpec((1, t, LANE), lambda b, hp, i: (b, 0, hp)),
            pl.BlockSpec((1, 2 * A_HEAD_DIM, t), lambda b, hp, i: (b, hp, 0)),
            pl.BlockSpec((1, nb, LANE), lambda b, hp, i: (b, 0, hp)),
            pl.BlockSpec((1, 2 * _QB, 8, blk), lambda b, hp, i: (hp, 0, 0, 0)),
            pl.BlockSpec((blk, LANE), lambda b, hp, i: (0, 0)),
        ],
        out_specs=pl.BlockSpec((1, qw, LANE), lambda b, hp, i: (b, i, hp)),
        out_shape=jax.ShapeDtypeStruct((batch, t, A_WIDTH), BF16),
        scratch_shapes=[
            pltpu.VMEM((2 * _QB, 2 * LANE, blk), BF16),
            pltpu.VMEM((2 * _QB, A_HEAD_DIM + 8, blk), F32),
            pltpu.VMEM((2 * _QB, 8, blk), F32),
            pltpu.VMEM((2, 2 * _QB, blk, blk), F32),
            pltpu.VMEM((2, 2 * _QB, 8, blk), F32),
        ],
        compiler_params=pltpu.CompilerParams(
            dimension_semantics=("arbitrary", "arbitrary", "arbitrary"),
            vmem_limit_bytes=VMEM_LIMIT),
        name="moba_prompt",
    )(qt, krow, vt, kmean.reshape(batch, nb, A_WIDTH), w["qaug"], w["kaug"])


def _moba_sample_kernel(pt_ref, q_ref, qn_ref, kn_ref, vn_ref, slope_ref, ck_hbm, cv_hbm, o_ref,
                        kbuf, sall, gate_ref, idx_v, idx_s, vbuf, sem_k, sem_v, sem_i,
                        *, n_pages, npg, ring):
    b = pl.program_id(0)
    nbatch = pl.num_programs(0)
    nxt = jnp.minimum(b + 1, nbatch - 1)
    n_chunks = n_pages // npg
    cw = npg * PAGE_SIZE
    bpc = cw // MOBA_BLOCK
    past = n_pages * PAGE_SIZE
    n_blocks = past // MOBA_BLOCK
    t_new = q_ref.shape[2]
    rows = A_HEADS * t_new
    ppb = MOBA_BLOCK // PAGE_SIZE
    slot_b = b % 2

    def k_copy(s, c, p):
        page = pt_ref[s, c * npg + p]
        return pltpu.make_async_copy(
            ck_hbm.at[0, page], kbuf.at[c % ring, :, :, pl.ds(p * PAGE_SIZE, PAGE_SIZE)],
            sem_k.at[c % ring])

    def start_chunk(s, c):
        for p in range(npg):
            k_copy(s, c, p).start()

    def start_ring(s):
        for c in range(ring):
            start_chunk(s, c)

    lane_g = lax.broadcasted_iota(jnp.int32, (t_new, LANE), 1)

    def split_q(qr):
        ops = []
        for h in range(A_HEADS):
            qh = qr[0, h]
            hi = qh.astype(BF16)
            ops.append(jnp.concatenate([hi, (qh - hi.astype(F32)).astype(BF16)], axis=0))
        return ops

    def sweep_chunk(s, q_ops, slot, c):
        for p in range(npg):
            k_copy(s, c, p).wait()
        for h in range(A_HEADS):
            s2 = _dot(q_ops[h], kbuf[c % ring, h].astype(BF16))
            sc = s2[0:t_new] + s2[t_new:2 * t_new]
            sall[slot, h * t_new:(h + 1) * t_new, c * cw:(c + 1) * cw] = sc
            g = gate_ref[h * t_new:(h + 1) * t_new, :]
            for jb in range(bpc):
                t2 = (sc[:, jb * MOBA_BLOCK:jb * MOBA_BLOCK + LANE]
                      + sc[:, jb * MOBA_BLOCK + LANE:(jb + 1) * MOBA_BLOCK])
                g = jnp.where(lane_g == c * bpc + jb, jnp.sum(t2, axis=1, keepdims=True), g)
            gate_ref[h * t_new:(h + 1) * t_new, :] = g
        if c + ring < n_chunks:
            start_chunk(s, c + ring)

    picks = [(t, r) for t in range(t_new) for r in range(MOBA_TOPK)]

    def v_copy(s, h, t, r, pg):
        bsel = idx_s[h * t_new + t, r]
        page = pt_ref[s, ppb * bsel + pg]
        col = ((t * MOBA_TOPK + r) * ppb + pg) * PAGE_SIZE
        return pltpu.make_async_copy(
            cv_hbm.at[0, page, h], vbuf.at[h, :, pl.ds(col, PAGE_SIZE)], sem_v.at[h])

    def v_start(s, h):
        for t, r in picks:
            for pg in range(ppb):
                v_copy(s, h, t, r, pg).start(priority=1)

    ahead = 2

    def select_blocks(s):
        lane_r = lax.broadcasted_iota(jnp.int32, (rows, LANE), 1)
        g = jnp.where(lane_r < n_blocks, gate_ref[...], -jnp.inf)
        idx = jnp.zeros((rows, LANE), jnp.int32)
        for r in range(MOBA_TOPK):
            m = jnp.max(g, axis=1, keepdims=True)
            first = jnp.min(jnp.where(g == m, lane_r, LANE), axis=1, keepdims=True)
            g = jnp.where(lane_r == first, -jnp.inf, g)
            idx = jnp.where(lane_r == r, first, idx)
        idx_v[...] = idx
        cp_i = pltpu.make_async_copy(idx_v, idx_s, sem_i)
        cp_i.start()
        cp_i.wait()
        for h0 in range(ahead):
            v_start(s, h0)

    @pl.when(b == 0)
    def _():
        start_ring(0)
        gate_ref[...] = jnp.zeros(gate_ref.shape, F32)
        q0 = split_q(q_ref)
        for c in range(n_chunks):
            sweep_chunk(0, q0, 0, c)
        start_ring(nxt)
        select_blocks(0)

    sub = lax.broadcasted_iota(jnp.int32, (t_new, MOBA_BLOCK), 0)
    klane = lax.broadcasted_iota(jnp.int32, (t_new, MOBA_BLOCK), 1)
    trow = lax.broadcasted_iota(jnp.int32, (t_new, t_new), 0)
    tcol = lax.broadcasted_iota(jnp.int32, (t_new, t_new), 1)

    def finish_head(h):
        if h + ahead < A_HEADS:
            v_start(b, h + ahead)
        r0 = h * t_new
        slope = slope_ref[r0:r0 + t_new, :][:, 0:1]
        pieces = []
        for t, r in picks:
            bsel = idx_s[r0 + t, r]
            off = pl.multiple_of(bsel * MOBA_BLOCK, MOBA_BLOCK)
            dist = ((past + t - bsel * MOBA_BLOCK) - klane).astype(F32)
            sc = sall[slot_b, r0:r0 + t_new, pl.ds(off, MOBA_BLOCK)] - slope * dist
            pieces.append(jnp.where(sub == t, sc, NEG))
        s_sel = jnp.concatenate(pieces, axis=1)
        s_new = _dot(q_ref[0, h], kn_ref[0, h], _NT) - slope * (trow - tcol).astype(F32)
        s_new = jnp.where(tcol <= trow, s_new, NEG)
        m = jnp.maximum(jnp.max(s_sel, axis=1, keepdims=True), jnp.max(s_new, axis=1, keepdims=True))
        p_sel = jnp.exp(s_sel - m)
        p_new = jnp.exp(s_new - m)
        l = jnp.sum(p_sel, axis=1, keepdims=True) + jnp.sum(p_new, axis=1, keepdims=True)
        for t, r in picks:
            for pg in range(ppb):
                v_copy(b, h, t, r, pg).wait()
        acc = _dot(p_sel.astype(BF16), vbuf[h].astype(BF16), _NT) + _dot(p_new, vn_ref[0, h])
        o_ref[0, h] = acc / l

    gate_ref[...] = jnp.zeros(gate_ref.shape, F32)
    qn = split_q(qn_ref)
    for i in range(max(n_chunks, A_HEADS)):
        if i < n_chunks:
            sweep_chunk(nxt, qn, 1 - slot_b, i)
        if i < A_HEADS:
            finish_head(i)

    @pl.when(b + 1 < nbatch)
    def _():
        start_ring(jnp.minimum(b + 2, nbatch - 1))
        select_blocks(nxt)


def _moba_sample(q4, kn4, vn4, cache_kt, cache_vt, page_table, slope_rows):
    nbatch, _, t_new, _ = q4.shape
    n_pages = page_table.shape[1]
    npg = 16
    ring = min(4, n_pages // npg)
    rows = A_HEADS * t_new
    past = n_pages * PAGE_SIZE
    shape4 = (1, A_HEADS, t_new, A_HEAD_DIM)
    blk4 = pl.BlockSpec(shape4, lambda b, pt: (b, 0, 0, 0))
    nxt4 = pl.BlockSpec(shape4, lambda b, pt: (jnp.minimum(b + 1, nbatch - 1), 0, 0, 0))
    return pl.pallas_call(
        functools.partial(_moba_sample_kernel, n_pages=n_pages, npg=npg, ring=ring),
        grid_spec=pltpu.PrefetchScalarGridSpec(
            num_scalar_prefetch=1,
            grid=(nbatch,),
            in_specs=[blk4, nxt4, blk4, blk4,
                      pl.BlockSpec((rows, LANE), lambda b, pt: (0, 0)),
                      pl.BlockSpec(memory_space=pl.ANY),
                      pl.BlockSpec(memory_space=pl.ANY)],
            out_specs=blk4,
            scratch_shapes=[
                pltpu.VMEM((ring, A_HEADS, A_HEAD_DIM, npg * PAGE_SIZE), F32),
                pltpu.VMEM((2, rows, past), F32),
                pltpu.VMEM((rows, LANE), F32),
                pltpu.VMEM((rows, LANE), jnp.int32),
                pltpu.SMEM((rows, LANE), jnp.int32),
                pltpu.VMEM((A_HEADS, A_HEAD_DIM, t_new * MOBA_TOPK * MOBA_BLOCK), F32),
                pltpu.SemaphoreType.DMA((ring,)),
                pltpu.SemaphoreType.DMA((A_HEADS,)),
                pltpu.SemaphoreType.DMA,
            ]),
        out_shape=jax.ShapeDtypeStruct((nbatch, A_HEADS, t_new, A_HEAD_DIM), F32),
        compiler_params=pltpu.CompilerParams(dimension_semantics=("arbitrary",),
                                             vmem_limit_bytes=VMEM_LIMIT),
        name="moba_sample",
    )(page_table, q4, q4, kn4, vn4, slope_rows, cache_kt, cache_vt)


def _gla_kernel(q_ref, k_ref, g_ref, v_ref, s0_ref, o_ref, sfin_ref, state, *, chunk, n_chunks, nbb):
    c = chunk
    mx = BF16 if c >= 16 else F32

    @pl.when(pl.program_id(2) == 0)
    def _():
        for bi in range(nbb):
            state[bi] = jnp.concatenate([s0_ref[bi, 0], jnp.zeros((LANE - B_KEY_DIM, B_VAL_DIM), F32)],
                                        axis=0).T

    ri = lax.broadcasted_iota(jnp.int32, (c, LANE), 0)
    ti = lax.broadcasted_iota(jnp.int32, (c, c), 0)
    si = lax.broadcasted_iota(jnp.int32, (c, c), 1)
    tril = jnp.where(si <= ti, 1.0, 0.0).astype(mx)
    sub3 = lax.broadcasted_iota(jnp.int32, (c // 8, 8, LANE), 1)
    levels = []
    half = c // 2
    while half >= 1:
        width = 2 * half
        sb = int(np.log2(width))
        upper = (ri & (width - 1)) >= half
        pair = ((ti >> sb) == (si >> sb)) & ((ti & (width - 1)) >= half) & ((si & (width - 1)) < half)
        levels.append((half, width, upper, pair))
        half //= 2

    seq = [(bi, ci) for bi in range(nbb) for ci in range(n_chunks)]
    loaded = []
    for bi, ci in seq:
        g = g_ref[bi, pl.ds(ci * c, c), :]
        g_hi = g.astype(BF16)
        g_lo = (g - g_hi.astype(F32)).astype(BF16)
        loaded.append((g, _dot(tril, g_hi.astype(mx)) + _dot(tril, g_lo.astype(mx))))

    prepared = []
    for (bi, ci), (g, b) in zip(seq, loaded):
        rows = pl.ds(ci * c, c)
        q = q_ref[bi, rows, :]
        k = k_ref[bi, rows, :]
        v32 = v_ref[bi, rows, :]
        b_last = b[c - 1:c, :]
        b3 = b.reshape(c // 8, 8, LANE)
        zs = []
        for half, width, upper, _ in levels:
            if half == 1:
                x = jnp.where(upper, g, 0.0)
            else:
                if half >= 8:
                    mid = jnp.concatenate(
                        [jnp.broadcast_to(b[m * width + half - 1:m * width + half, :], (width, LANE))
                         for m in range(c // width)], axis=0)
                else:
                    mid3 = jnp.broadcast_to(b3[:, half - 1:half, :], b3.shape)
                    for m in range(1, 8 // width):
                        r = m * width + half - 1
                        mid3 = jnp.where(sub3 >= m * width,
                                         jnp.broadcast_to(b3[:, r:r + 1, :], b3.shape), mid3)
                    mid = mid3.reshape(c, LANE)
                x = jnp.where(upper, b - mid, mid - b)
            zs.append((jnp.where(upper, q, k) * jnp.exp2(x)).astype(mx))
        qd = (q * jnp.exp2(b)).astype(mx)
        kd = (k * jnp.exp2(b_last - b)).astype(mx)
        o_same = jnp.sum(q * k, axis=1, keepdims=True) * v32
        prepared.append((v32.astype(mx), zs, qd, kd, jnp.exp2(b_last), o_same))

    grams, updates = [], []
    for v, zs, qd, kd, a_last, o_same in prepared:
        grams.append([_dot(z, z, _NT) for z in zs])
        updates.append(_dot(v, kd, _TN))

    attns = []
    for gr in grams:
        attn = jnp.where(levels[0][3], gr[0], 0.0)
        for (_, _, _, pair), gm in zip(levels[1:], gr[1:]):
            attn = attn + jnp.where(pair, gm, 0.0)
        attns.append(attn.astype(mx))

    states = []
    for n, (bi, ci) in enumerate(seq):
        st = state[bi] if ci == 0 else states[-1][1]
        states.append((st, prepared[n][4] * st + updates[n]))
        if ci == n_chunks - 1:
            state[bi] = states[-1][1]

    for n, (bi, ci) in enumerate(seq):
        v, zs, qd, kd, a_last, o_same = prepared[n]
        o_ref[bi, pl.ds(ci * c, c), :] = (o_same + _dot(attns[n], v)
                                          + _dot(qd, states[n][0].astype(mx), _NT))

    @pl.when(pl.program_id(2) == pl.num_programs(2) - 1)
    def _():
        for bi in range(nbb):
            sfin_ref[bi, 0] = state[bi].T[0:B_KEY_DIM, :]


def _gla(qb, kb, lg, vb, s0, chunk):
    batch, t, _ = qb.shape
    tc = min(t, 1024)
    n_chunks = tc // chunk
    nbb = 8 if (t == tc and batch % 8 == 0) else 1
    seq = pl.BlockSpec((nbb, tc, LANE), lambda b, h, c: (b, c, h))
    st = pl.BlockSpec((nbb, 1, B_KEY_DIM, B_VAL_DIM), lambda b, h, c: (b, h, 0, 0))
    return pl.pallas_call(
        functools.partial(_gla_kernel, chunk=chunk, n_chunks=n_chunks, nbb=nbb),
        grid=(batch // nbb, B_HEADS, t // tc),
        in_specs=[seq, seq, seq, seq, st],
        out_specs=[seq, st],
        out_shape=[jax.ShapeDtypeStruct((batch, t, B_VAL_WIDTH), F32),
                   jax.ShapeDtypeStruct((batch, B_HEADS, B_KEY_DIM, B_VAL_DIM), F32)],
        scratch_shapes=[pltpu.VMEM((nbb, LANE, B_VAL_DIM), F32)],
        compiler_params=pltpu.CompilerParams(
            dimension_semantics=("arbitrary", "arbitrary", "arbitrary"),
            vmem_limit_bytes=VMEM_LIMIT),
        name="gla_c%d" % chunk,
    )(qb, kb, lg, vb, s0)


def _merge_kernel(x_ref, oa_ref, sza_ref, ob_ref, szb_ref, gg_ref, wo_ref, y_ref):
    ya = oa_ref[...].astype(F32) * sza_ref[...].astype(F32)
    ob = ob_ref[...]
    parts = []
    for h in range(B_HEADS):
        oh = ob[:, h * B_VAL_DIM:(h + 1) * B_VAL_DIM]
        ms = jnp.mean(oh * oh, axis=-1, keepdims=True)
        parts.append(oh * lax.rsqrt(ms + EPS))
    yb = (jnp.concatenate(parts, axis=1) * gg_ref[...]) * szb_ref[...].astype(F32)
    cat = jnp.concatenate([ya, yb], axis=1).astype(BF16)
    y_ref[...] = x_ref[...] + _dot(cat, wo_ref[...])


def _merge(x2d, oa, sza, ob, szb, w):
    n = x2d.shape[0]
    tm = min(n, 512)
    row = lambda i: (i, 0)
    const = lambda i: (0, 0)
    return pl.pallas_call(
        _merge_kernel,
        grid=(n // tm,),
        in_specs=[pl.BlockSpec((tm, D_MODEL), row), pl.BlockSpec((tm, 512), row),
                  pl.BlockSpec((tm, 512), row), pl.BlockSpec((tm, 512), row),
                  pl.BlockSpec((tm, 512), row), pl.BlockSpec((1, 512), const),
                  pl.BlockSpec((D_MODEL, D_MODEL), const)],
        out_specs=pl.BlockSpec((tm, D_MODEL), row),
        out_shape=jax.ShapeDtypeStruct((n, D_MODEL), F32),
        compiler_params=pltpu.CompilerParams(dimension_semantics=("arbitrary",),
                                             vmem_limit_bytes=VMEM_LIMIT),
        name="merge_out",
    )(x2d, oa, sza, ob, szb, w["g_gla"], w["w_out"])


def _pad_heads(a, n_heads, dim):
    lead = a.shape[:-1]
    a = a.reshape(lead + (n_heads, dim))
    a = jnp.pad(a, [(0, 0)] * len(lead) + [(0, 0), (0, LANE - dim)])
    return a.reshape(lead + (n_heads * LANE,))


def _layer_weights(g_pre, w_in, g_q, g_k, w_a2, b_a, g_gla, w_out):
    o = _OFF
    w_t = w_in[:, o[0]:o[3]].T.astype(BF16)
    w_row = jnp.concatenate([
        w_in[:, o[3]:o[4]], w_in[:, o[7]:o[8]],
        _pad_heads(w_in[:, o[4]:o[5]], B_HEADS, B_KEY_DIM),
        _pad_heads(w_in[:, o[5]:o[6]], B_HEADS, B_KEY_DIM),
        w_in[:, o[6]:o[7]],
        jnp.pad(w_in[:, o[8]:o[9]], ((0, 0), (0, LANE - GATE_RANK)))], axis=1).astype(BF16)
    w_a2p = jnp.pad(_pad_heads(w_a2, B_HEADS, B_KEY_DIM), ((0, LANE - GATE_RANK), (0, 0))).astype(BF16)
    b_ap = _pad_heads(b_a, B_HEADS, B_KEY_DIM).reshape(1, 512)
    gq_t = jnp.broadcast_to(jnp.tile(g_q, A_HEADS)[:, None], (A_WIDTH, MOBA_BLOCK))
    gk_t = jnp.broadcast_to(jnp.tile(g_k, A_HEADS)[:, None], (A_WIDTH, MOBA_BLOCK))
    slopes = jnp.asarray([2.0 ** (-8.0 * (h + 1) / A_HEADS) for h in range(A_HEADS)], F32)
    c = slopes * _LOG2E
    pieces = []
    rem = c
    for _ in range(3):
        pc = rem.astype(BF16).astype(F32)
        pieces.append(pc)
        rem = rem - pc
    qq = jnp.arange(MOBA_BLOCK, dtype=F32)
    qaug = jnp.zeros((A_HEADS, _QB, 8, MOBA_BLOCK), F32)
    for p_i, pc in enumerate(pieces):
        qaug = qaug.at[:, :, p_i, :].set(pc[:, None, None])
        qaug = qaug.at[:, :, 3 + p_i, :].set(pc[:, None, None])
    qpos = qq[None, None, :] + (MOBA_BLOCK * jnp.arange(_QB, dtype=F32))[None, :, None]
    qaug = qaug.at[:, :, 6, :].set(-c[:, None, None] * qpos)
    qaug = qaug.reshape(A_HEADS // 2, 2 * _QB, 8, MOBA_BLOCK)
    kaug = jnp.zeros((MOBA_BLOCK, LANE), F32)
    kaug = kaug.at[:, 0:3].set(qq[:, None]).at[:, 6].set(1.0).astype(BF16)
    return {
        "g_pre": g_pre.reshape(1, D_MODEL), "w_t": w_t, "w_row": w_row, "w_a2p": w_a2p, "b_ap": b_ap,
        "gq_t": gq_t, "gk_t": gk_t, "g_gla": g_gla.reshape(1, B_VAL_WIDTH),
        "w_out": w_out.astype(BF16), "qaug": qaug, "kaug": kaug, "slopes": slopes,
    }


def kernel(x_prompt, x_sample, cache_k, cache_v, state_gla, page_table, g_pre, w_in, g_q, g_k, w_a2, b_a, g_gla, w_out):
    depth = g_pre.shape[0]
    batch, t, _ = x_prompt.shape
    nb_s, t_new, _ = x_sample.shape
    cache_kt = jnp.transpose(cache_k, (0, 1, 3, 4, 2))
    cache_vt = jnp.transpose(cache_v, (0, 1, 3, 4, 2))
    yp = x_prompt.reshape(batch * t, D_MODEL)
    ys = x_sample.reshape(nb_s * t_new, D_MODEL)
    kp_l, vp_l, sp_l, ks_l, vs_l, ss_l = [], [], [], [], [], []
    for l in range(depth):
        w = _layer_weights(g_pre[l], w_in[l], g_q[l], g_k[l], w_a2[l], b_a[l], g_gla[l], w_out[l])
        qt, kt, vt, krow, kmean, sza, szb, qb, kb, vb, lg = _proj(yp, w, batch, sample=False)
        oa = _moba_prompt(qt, krow, vt, kmean, w)
        r3 = lambda a: a.reshape(batch, t, 512)
        s0 = jnp.zeros((batch, B_HEADS, B_KEY_DIM, B_VAL_DIM), F32)
        ob, s_fin = _gla(r3(qb), r3(kb), r3(lg), r3(vb), s0, chunk=64)
        yp = _merge(yp, oa.reshape(batch * t, A_WIDTH), sza, ob.reshape(batch * t, 512), szb, w)
        kp_l.append(jnp.transpose(kt.reshape(batch, A_HEADS, A_HEAD_DIM, t), (0, 3, 1, 2)))
        vp_l.append(jnp.transpose(vt.reshape(batch, A_HEADS, A_HEAD_DIM, t), (0, 3, 1, 2)))
        sp_l.append(s_fin)
        qs, ks, vs, sza, szb, qb, kb, vb, lg = _proj(ys, w, nb_s, sample=True)
        h4 = lambda a: jnp.transpose(a.reshape(nb_s, t_new, A_HEADS, A_HEAD_DIM), (0, 2, 1, 3))
        slope_rows = jnp.broadcast_to(jnp.repeat(w["slopes"], t_new)[:, None], (A_HEADS * t_new, LANE))
        oa4 = _moba_sample(h4(qs), h4(ks), h4(vs), cache_kt[l:l + 1], cache_vt[l:l + 1],
                           page_table, slope_rows)
        oa = jnp.transpose(oa4, (0, 2, 1, 3)).reshape(nb_s * t_new, A_WIDTH)
        r3 = lambda a: a.reshape(nb_s, t_new, 512)
        ob, s_new = _gla(r3(qb), r3(kb), r3(lg), r3(vb), state_gla[l], chunk=t_new)
        ys = _merge(ys, oa, sza, ob.reshape(nb_s * t_new, 512), szb, w)
        ks_l.append(ks.reshape(nb_s, t_new, A_HEADS, A_HEAD_DIM))
        vs_l.append(vs.reshape(nb_s, t_new, A_HEADS, A_HEAD_DIM))
        ss_l.append(s_new)
    return (yp.reshape(batch, t, D_MODEL), ys.reshape(nb_s, t_new, D_MODEL),
            jnp.stack(kp_l), jnp.stack(vp_l), jnp.stack(sp_l),
            jnp.stack(ks_l), jnp.stack(vs_l), jnp.stack(ss_l))
```

```python
import functools

import jax
import jax.numpy as jnp
import numpy as np
from jax import lax
from jax.experimental import pallas as pl
from jax.experimental.pallas import tpu as pltpu

F32 = jnp.float32
BF16 = jnp.bfloat16

D_MODEL = 1024
A_HEADS = 8
A_HEAD_DIM = 64
A_WIDTH = A_HEADS * A_HEAD_DIM
MOBA_BLOCK = 256
MOBA_TOPK = 3
B_HEADS = 4
B_KEY_DIM = 64
B_VAL_DIM = 128
B_KEY_WIDTH = B_HEADS * B_KEY_DIM
B_VAL_WIDTH = B_HEADS * B_VAL_DIM
GATE_RANK = 16
GATE_TAU = 16.0
PAGE_SIZE = 128
EPS = 1e-6
NEG = -1e30

LANE = 128
VMEM_LIMIT = 56 * 1024 * 1024

_OFF = np.cumsum([0, A_WIDTH, A_WIDTH, A_WIDTH, A_WIDTH, B_KEY_WIDTH, B_KEY_WIDTH,
                  B_VAL_WIDTH, B_VAL_WIDTH, GATE_RANK]).tolist()
_ROW_W = 3 * 512 + 2 * B_KEY_WIDTH + LANE
_AUG0 = 128
_MASK0 = 136
_LOG2E = 1.4426950408889634


def _dot(a, b, dims=(((1,), (0,)), ((), ())), precision=None):
    return lax.dot_general(a, b, dims, precision=precision, preferred_element_type=F32)


_NT = (((1,), (1,)), ((), ()))
_TN = (((0,), (0,)), ((), ()))


def _silu(x):
    return x / (1.0 + jnp.exp(-x))


def _log_sigmoid(x):
    return jnp.minimum(x, 0.0) - jnp.log1p(jnp.exp(-jnp.abs(x)))


def _proj_kernel(x_ref, gpre_ref, wt_ref, wrow_ref, wa2_ref, ba_ref, gq_ref, gk_ref, *out_refs,
                 tm, sample):
    x = x_ref[...]
    ms = jnp.mean(x * x, axis=-1, keepdims=True)
    h = ((x * lax.rsqrt(ms + EPS)) * gpre_ref[...]).astype(BF16)

    pt = _dot(wt_ref[...], h, _NT)

    def head_norm(t, g):
        t3 = t.reshape(A_HEADS, A_HEAD_DIM, tm)
        ss = jnp.mean(t3 * t3, axis=1, keepdims=True)
        return (t3 * lax.rsqrt(ss + EPS)).reshape(A_WIDTH, tm) * g

    q_t = head_norm(pt[0:A_WIDTH], gq_ref[...]) * (A_HEAD_DIM ** -0.5)
    k_t = head_norm(pt[A_WIDTH:2 * A_WIDTH], gk_ref[...])
    v_t = pt[2 * A_WIDTH:3 * A_WIDTH]

    def seg(i0, i1):
        return _dot(h, wrow_ref[:, i0:i1])

    sza = _silu(seg(0, 512)).astype(BF16)
    szb = _silu(seg(512, 1024)).astype(BF16)
    qk = seg(1024, 1536)
    qb = qk[:, 0:B_KEY_WIDTH] * (B_KEY_DIM ** -0.5)
    kb = qk[:, B_KEY_WIDTH:2 * B_KEY_WIDTH]
    vb = seg(1536, 2048)
    ab = seg(2048, 2176).astype(BF16)
    pre = _dot(ab, wa2_ref[...]) + ba_ref[...]
    lg = _log_sigmoid(pre) * (_LOG2E / GATE_TAU)

    if sample:
        (q_ref, k_ref, v_ref, sza_ref, szb_ref, qb_ref, kb_ref, vb_ref, lg_ref) = out_refs
        q_ref[...] = q_t.T
        k_ref[...] = k_t.T
        v_ref[...] = v_t.T
    else:
        (qt_ref, kt_ref, vt_ref, krow_ref, kmean_ref,
         sza_ref, szb_ref, qb_ref, kb_ref, vb_ref, lg_ref) = out_refs
        qt_ref[0] = q_t
        kt_ref[0] = k_t
        vt_ref[0] = v_t
        k_row = k_t.T
        krow_ref[0] = k_row.astype(BF16)
        kmean_ref[0] = jnp.mean(k_row, axis=0, keepdims=True)
    sza_ref[...] = sza
    szb_ref[...] = szb
    qb_ref[...] = qb
    kb_ref[...] = kb
    vb_ref[...] = vb
    lg_ref[...] = lg


def _proj(x2d, w, batch, sample):
    n = x2d.shape[0]
    tm = MOBA_BLOCK
    nt = n // tm
    const = lambda i: (0, 0)
    row = lambda i: (i, 0)
    in_specs = [
        pl.BlockSpec((tm, D_MODEL), row),
        pl.BlockSpec((1, D_MODEL), const),
        pl.BlockSpec((3 * A_WIDTH, D_MODEL), const),
        pl.BlockSpec((D_MODEL, _ROW_W), const),
        pl.BlockSpec((LANE, B_KEY_WIDTH), const),
        pl.BlockSpec((1, B_KEY_WIDTH), const),
        pl.BlockSpec((A_WIDTH, tm), const),
        pl.BlockSpec((A_WIDTH, tm), const),
    ]
    row_specs = [pl.BlockSpec((tm, wd), row) for wd in (512, 512, B_KEY_WIDTH, B_KEY_WIDTH, 512, B_KEY_WIDTH)]
    row_shapes = [jax.ShapeDtypeStruct((n, wd), dt) for wd, dt in
                  ((512, BF16), (512, BF16), (B_KEY_WIDTH, F32), (B_KEY_WIDTH, F32), (512, F32),
                   (B_KEY_WIDTH, F32))]
    if sample:
        out_specs = [pl.BlockSpec((tm, 512), row)] * 3 + row_specs
        out_shape = [jax.ShapeDtypeStruct((n, 512), F32)] * 3 + row_shapes
    else:
        t = n // batch
        tpb = t // tm
        feat = lambda i: (i // tpb, 0, i % tpb)
        out_specs = ([pl.BlockSpec((1, A_WIDTH, tm), feat)] * 3
                     + [pl.BlockSpec((1, tm, 512), lambda i: (i // tpb, i % tpb, 0)),
                        pl.BlockSpec((1, 1, 512), lambda i: (i, 0, 0))]
                     + row_specs)
        out_shape = ([jax.ShapeDtypeStruct((batch, A_WIDTH, t), F32)] * 3
                     + [jax.ShapeDtypeStruct((batch, t, 512), BF16),
                        jax.ShapeDtypeStruct((nt, 1, 512), F32)]
                     + row_shapes)
    return pl.pallas_call(
        functools.partial(_proj_kernel, tm=tm, sample=sample),
        grid=(nt,),
        in_specs=in_specs,
        out_specs=out_specs,
        out_shape=out_shape,
        compiler_params=pltpu.CompilerParams(dimension_semantics=("arbitrary",),
                                             vmem_limit_bytes=VMEM_LIMIT),
        name="proj_sample" if sample else "proj_prompt",
    )(x2d, w["g_pre"], w["w_t"], w["w_row"], w["w_a2p"], w["b_ap"], w["gq_t"], w["gk_t"])


def _top3_rows(g, n):
    idx = lax.broadcasted_iota(jnp.int32, g.shape, 0)
    sel = jnp.zeros(g.shape, F32)
    for _ in range(MOBA_TOPK):
        m = jnp.max(g, axis=0, keepdims=True)
        first = jnp.min(jnp.where(g == m, idx, n), axis=0, keepdims=True)
        pick = idx == first
        sel = jnp.where(pick, 1.0, sel)
        g = jnp.where(pick, -jnp.inf, g)
    return sel


_QB = 4


def _moba_prompt_kernel(qt_ref, krow_ref, vt_ref, kmean_ref, qaug_ref, kaug_ref, o_ref,
                        qop_ref, acc_ref, m_ref, s_ref, cm_ref, *, nb):
    blk = MOBA_BLOCK
    qw = _QB * blk
    i0 = pl.program_id(2) * _QB

    qt = qt_ref[0]
    row128 = lax.broadcasted_iota(jnp.int32, (2 * A_HEAD_DIM, qw), 0)
    lane128 = lax.broadcasted_iota(jnp.int32, (nb, LANE), 1)
    bidx = lax.broadcasted_iota(jnp.int32, (nb, qw), 0)
    iq = i0 + lax.broadcasted_iota(jnp.int32, (nb, qw), 1) // blk
    kmean = kmean_ref[0]
    qt_hi = qt.astype(BF16)
    qt_lo = (qt - qt_hi.astype(F32)).astype(BF16)

    for hh in range(2):
        in_head = (row128 >= hh * A_HEAD_DIM) & (row128 < (hh + 1) * A_HEAD_DIM)
        q_h = jnp.where(in_head, qt * _LOG2E, 0.0)
        km_h = jnp.where((lane128 >= hh * A_HEAD_DIM) & (lane128 < (hh + 1) * A_HEAD_DIM), kmean, 0.0)
        km_hi = km_h.astype(BF16)
        km_lo = (km_h - km_hi.astype(F32)).astype(BF16)
        gate = _dot(km_hi, qt_hi) + _dot(km_hi, qt_lo) + _dot(km_lo, qt_hi)
        past = bidx < iq
        sel = _top3_rows(jnp.where(past, gate, NEG), nb)
        keep = ((sel > 0.5) & past) | (bidx == iq)
        maskbias = jnp.where(keep, 0.0, NEG)
        for qb in range(_QB):
            c = hh * _QB + qb
            cols = slice(qb * blk, (qb + 1) * blk)
            qop = jnp.concatenate(
                [q_h[:, cols], qaug_ref[0, c], maskbias[:, cols],
                 jnp.zeros((2 * LANE - _MASK0 - nb, blk), F32)], axis=0)
            qop_ref[c] = qop.astype(BF16)
            acc_ref[c] = jnp.zeros(acc_ref.shape[1:], F32)
            m_ref[c] = jnp.full(m_ref.shape[1:], -jnp.inf, F32)

    lane_k = lax.broadcasted_iota(jnp.int32, (blk, LANE), 1)
    kaug_base = kaug_ref[...]
    ones_rows = jnp.ones((8, blk), BF16)
    kk = lax.broadcasted_iota(jnp.int32, (blk, blk), 0)
    qq = lax.broadcasted_iota(jnp.int32, (blk, blk), 1)

    def scores(j, slot, chains):
        off = pl.multiple_of(j * blk, blk)
        shift = ((j - i0) * blk).astype(F32)
        kaug = jnp.where((lane_k >= 3) & (lane_k < 6), shift.astype(BF16),
                         jnp.where(lane_k == 8 + j, jnp.ones((), BF16), kaug_base))
        kop = jnp.concatenate([krow_ref[0, pl.ds(off, blk), :], kaug], axis=1)
        for c in chains:
            s = _dot(kop, qop_ref[c])
            s_ref[slot, c] = s
            cm_ref[slot, c] = jnp.max(s.reshape(blk // 8, 8, blk), axis=0)

    def absorb(j, slot, chains):
        off = pl.multiple_of(j * blk, blk)
        vops = []
        for hh in range(2):
            v_t = vt_ref[0, hh * A_HEAD_DIM:(hh + 1) * A_HEAD_DIM, pl.ds(off, blk)].astype(BF16)
            vops.append(jnp.concatenate([v_t, ones_rows], axis=0))
        for c, causal in chains:
            s = s_ref[slot, c]
            if causal:
                s = jnp.where(kk <= qq, s, NEG)
            m_prev = m_ref[c][0:1]
            cmax = s if causal else cm_ref[slot, c]
            m_new = jnp.maximum(m_prev, jnp.max(cmax, axis=0, keepdims=True))
            alpha = jnp.exp2(m_prev - m_new)
            p = jnp.exp2(s - m_new).astype(BF16)
            acc_ref[c] = alpha * acc_ref[c] + _dot(vops[c // _QB], p)
            m_ref[c] = jnp.broadcast_to(m_new, m_ref.shape[1:])

    all_c = list(range(2 * _QB))
    plain = [(c, False) for c in all_c]
    scores(0, 0, all_c)

    def body(jj, carry):
        j = _QB * jj
        for d in range(_QB):
            scores(j + d + 1, (d + 1) % 2, all_c)
            absorb(j + d, d % 2, plain)
        return carry

    lax.fori_loop(0, i0 // _QB, body, 0)
    for d in range(_QB):
        if d + 1 < _QB:
            scores(i0 + d + 1, (d + 1) % 2, [c for c in all_c if c % _QB >= d + 1])
        absorb(i0 + d, d % 2, [(c, c % _QB == d) for c in all_c if c % _QB >= d])

    for qb in range(_QB):
        outs = []
        for hh in range(2):
            acc = acc_ref[hh * _QB + qb]
            outs.append(acc[0:A_HEAD_DIM] / acc[A_HEAD_DIM:A_HEAD_DIM + 1])
        o_ref[0, qb * blk:(qb + 1) * blk, :] = jnp.concatenate(outs, axis=0).T.astype(o_ref.dtype)


def _moba_prompt(qt, krow, vt, kmean, w):
    batch, _, t = qt.shape
    nb = t // MOBA_BLOCK
    blk = MOBA_BLOCK
    qw = _QB * blk
    return pl.pallas_call(
        functools.partial(_moba_prompt_kernel, nb=nb),
        grid=(batch, A_HEADS // 2, nb // _QB),
        in_specs=[
            pl.BlockSpec((1, 2 * A_HEAD_DIM, qw), lambda b, hp, i: (b, hp, i)),
            pl.BlockSpec((1, t, LANE), lambda b, hp, i: (b, 0, hp)),
            pl.BlockSpec((1, 2 * A_HEAD_DIM, t), lambda b, hp, i: (b, hp, 0)),
            pl.BlockSpec((1, nb, LANE), lambda b, hp, i: (b, 0, hp)),
            pl.BlockSpec((1, 2 * _QB, 8, blk), lambda b, hp, i: (hp, 0, 0, 0)),
            pl.BlockSpec((blk, LANE), lambda b, hp, i: (0, 0)),
        ],
        out_specs=pl.BlockSpec((1, qw, LANE), lambda b, hp, i: (b, i, hp)),
        out_shape=jax.ShapeDtypeStruct((batch, t, A_WIDTH), BF16),
        scratch_shapes=[
            pltpu.VMEM((2 * _QB, 2 * LANE, blk), BF16),
            pltpu.VMEM((2 * _QB, A_HEAD_DIM + 8, blk), F32),
            pltpu.VMEM((2 * _QB, 8, blk), F32),
            pltpu.VMEM((2, 2 * _QB, blk, blk), F32),
            pltpu.VMEM((2, 2 * _QB, 8, blk), F32),
        ],
        compiler_params=pltpu.CompilerParams(
            dimension_semantics=("arbitrary", "arbitrary", "arbitrary"),
            vmem_limit_bytes=VMEM_LIMIT),
        name="moba_prompt",
    )(qt, krow, vt, kmean.reshape(batch, nb, A_WIDTH), w["qaug"], w["kaug"])


def _moba_sample_kernel(pt_ref, q_ref, qn_ref, kn_ref, vn_ref, slope_ref, ck_hbm, cv_hbm, o_ref,
                        kbuf, sall, gate_ref, idx_v, idx_s, vbuf, sem_k, sem_v, sem_i,
                        *, n_pages, npg, ring):
    b = pl.program_id(0)
    nbatch = pl.num_programs(0)
    nxt = jnp.minimum(b + 1, nbatch - 1)
    n_chunks = n_pages // npg
    cw = npg * PAGE_SIZE
    bpc = cw // MOBA_BLOCK
    past = n_pages * PAGE_SIZE
    n_blocks = past // MOBA_BLOCK
    t_new = q_ref.shape[2]
    rows = A_HEADS * t_new
    ppb = MOBA_BLOCK // PAGE_SIZE
    slot_b = b % 2

    def k_copy(s, c, p):
        page = pt_ref[s, c * npg + p]
        return pltpu.make_async_copy(
            ck_hbm.at[0, page], kbuf.at[c % ring, :, :, pl.ds(p * PAGE_SIZE, PAGE_SIZE)],
            sem_k.at[c % ring])

    def start_chunk(s, c):
        for p in range(npg):
            k_copy(s, c, p).start()

    def start_ring(s):
        for c in range(ring):
            start_chunk(s, c)

    lane_g = lax.broadcasted_iota(jnp.int32, (t_new, LANE), 1)

    def split_q(qr):
        ops = []
        for h in range(A_HEADS):
            qh = qr[0, h]
            hi = qh.astype(BF16)
            ops.append(jnp.concatenate([hi, (qh - hi.astype(F32)).astype(BF16)], axis=0))
        return ops

    def sweep_chunk(s, q_ops, slot, c):
        for p in range(npg):
            k_copy(s, c, p).wait()
        for h in range(A_HEADS):
            s2 = _dot(q_ops[h], kbuf[c % ring, h].astype(BF16))
            sc = s2[0:t_new] + s2[t_new:2 * t_new]
            sall[slot, h * t_new:(h + 1) * t_new, c * cw:(c + 1) * cw] = sc
            g = gate_ref[h * t_new:(h + 1) * t_new, :]
            for jb in range(bpc):
                t2 = (sc[:, jb * MOBA_BLOCK:jb * MOBA_BLOCK + LANE]
                      + sc[:, jb * MOBA_BLOCK + LANE:(jb + 1) * MOBA_BLOCK])
                g = jnp.where(lane_g == c * bpc + jb, jnp.sum(t2, axis=1, keepdims=True), g)
            gate_ref[h * t_new:(h + 1) * t_new, :] = g
        if c + ring < n_chunks:
            start_chunk(s, c + ring)

    picks = [(t, r) for t in range(t_new) for r in range(MOBA_TOPK)]

    def v_copy(s, h, t, r, pg):
        bsel = idx_s[h * t_new + t, r]
        page = pt_ref[s, ppb * bsel + pg]
        col = ((t * MOBA_TOPK + r) * ppb + pg) * PAGE_SIZE
        return pltpu.make_async_copy(
            cv_hbm.at[0, page, h], vbuf.at[h, :, pl.ds(col, PAGE_SIZE)], sem_v.at[h])

    def v_start(s, h):
        for t, r in picks:
            for pg in range(ppb):
                v_copy(s, h, t, r, pg).start(priority=1)

    ahead = 2

    def select_blocks(s):
        lane_r = lax.broadcasted_iota(jnp.int32, (rows, LANE), 1)
        g = jnp.where(lane_r < n_blocks, gate_ref[...], -jnp.inf)
        idx = jnp.zeros((rows, LANE), jnp.int32)
        for r in range(MOBA_TOPK):
            m = jnp.max(g, axis=1, keepdims=True)
            first = jnp.min(jnp.where(g == m, lane_r, LANE), axis=1, keepdims=True)
            g = jnp.where(lane_r == first, -jnp.inf, g)
            idx = jnp.where(lane_r == r, first, idx)
        idx_v[...] = idx
        cp_i = pltpu.make_async_copy(idx_v, idx_s, sem_i)
        cp_i.start()
        cp_i.wait()
        for h0 in range(ahead):
            v_start(s, h0)

    @pl.when(b == 0)
    def _():
        start_ring(0)
        gate_ref[...] = jnp.zeros(gate_ref.shape, F32)
        q0 = split_q(q_ref)
        for c in range(n_chunks):
            sweep_chunk(0, q0, 0, c)
        start_ring(nxt)
        select_blocks(0)

    sub = lax.broadcasted_iota(jnp.int32, (t_new, MOBA_BLOCK), 0)
    klane = lax.broadcasted_iota(jnp.int32, (t_new, MOBA_BLOCK), 1)
    trow = lax.broadcasted_iota(jnp.int32, (t_new, t_new), 0)
    tcol = lax.broadcasted_iota(jnp.int32, (t_new, t_new), 1)

    def finish_head(h):
        if h + ahead < A_HEADS:
            v_start(b, h + ahead)
        r0 = h * t_new
        slope = slope_ref[r0:r0 + t_new, :][:, 0:1]
        pieces = []
        for t, r in picks:
            bsel = idx_s[r0 + t, r]
            off = pl.multiple_of(bsel * MOBA_BLOCK, MOBA_BLOCK)
            dist = ((past + t - bsel * MOBA_BLOCK) - klane).astype(F32)
            sc = sall[slot_b, r0:r0 + t_new, pl.ds(off, MOBA_BLOCK)] - slope * dist
            pieces.append(jnp.where(sub == t, sc, NEG))
        s_sel = jnp.concatenate(pieces, axis=1)
        s_new = _dot(q_ref[0, h], kn_ref[0, h], _NT) - slope * (trow - tcol).astype(F32)
        s_new = jnp.where(tcol <= trow, s_new, NEG)
        m = jnp.maximum(jnp.max(s_sel, axis=1, keepdims=True), jnp.max(s_new, axis=1, keepdims=True))
        p_sel = jnp.exp(s_sel - m)
        p_new = jnp.exp(s_new - m)
        l = jnp.sum(p_sel, axis=1, keepdims=True) + jnp.sum(p_new, axis=1, keepdims=True)
        for t, r in picks:
            for pg in range(ppb):
                v_copy(b, h, t, r, pg).wait()
        acc = _dot(p_sel.astype(BF16), vbuf[h].astype(BF16), _NT) + _dot(p_new, vn_ref[0, h])
        o_ref[0, h] = acc / l

    gate_ref[...] = jnp.zeros(gate_ref.shape, F32)
    qn = split_q(qn_ref)
    for i in range(max(n_chunks, A_HEADS)):
        if i < n_chunks:
            sweep_chunk(nxt, qn, 1 - slot_b, i)
        if i < A_HEADS:
            finish_head(i)

    @pl.when(b + 1 < nbatch)
    def _():
        start_ring(jnp.minimum(b + 2, nbatch - 1))
        select_blocks(nxt)


def _moba_sample(q4, kn4, vn4, cache_kt, cache_vt, page_table, slope_rows):
    nbatch, _, t_new, _ = q4.shape
    n_pages = page_table.shape[1]
    npg = 16
    ring = min(4, n_pages // npg)
    rows = A_HEADS * t_new
    past = n_pages * PAGE_SIZE
    shape4 = (1, A_HEADS, t_new, A_HEAD_DIM)
    blk4 = pl.BlockSpec(shape4, lambda b, pt: (b, 0, 0, 0))
    nxt4 = pl.BlockSpec(shape4, lambda b, pt: (jnp.minimum(b + 1, nbatch - 1), 0, 0, 0))
    return pl.pallas_call(
        functools.partial(_moba_sample_kernel, n_pages=n_pages, npg=npg, ring=ring),
        grid_spec=pltpu.PrefetchScalarGridSpec(
            num_scalar_prefetch=1,
            grid=(nbatch,),
            in_specs=[blk4, nxt4, blk4, blk4,
                      pl.BlockSpec((rows, LANE), lambda b, pt: (0, 0)),
                      pl.BlockSpec(memory_space=pl.ANY),
                      pl.BlockSpec(memory_space=pl.ANY)],
            out_specs=blk4,
            scratch_shapes=[
                pltpu.VMEM((ring, A_HEADS, A_HEAD_DIM, npg * PAGE_SIZE), F32),
                pltpu.VMEM((2, rows, past), F32),
                pltpu.VMEM((rows, LANE), F32),
                pltpu.VMEM((rows, LANE), jnp.int32),
                pltpu.SMEM((rows, LANE), jnp.int32),
                pltpu.VMEM((A_HEADS, A_HEAD_DIM, t_new * MOBA_TOPK * MOBA_BLOCK), F32),
                pltpu.SemaphoreType.DMA((ring,)),
                pltpu.SemaphoreType.DMA((A_HEADS,)),
                pltpu.SemaphoreType.DMA,
            ]),
        out_shape=jax.ShapeDtypeStruct((nbatch, A_HEADS, t_new, A_HEAD_DIM), F32),
        compiler_params=pltpu.CompilerParams(dimension_semantics=("arbitrary",),
                                             vmem_limit_bytes=VMEM_LIMIT),
        name="moba_sample",
    )(page_table, q4, q4, kn4, vn4, slope_rows, cache_kt, cache_vt)


def _gla_kernel(q_ref, k_ref, g_ref, v_ref, s0_ref, o_ref, sfin_ref, state, *, chunk, n_chunks, nbb):
    c = chunk
    mx = BF16 if c >= 16 else F32
    hpl = LANE // B_KEY_DIM
    zpad = jnp.zeros((B_KEY_DIM, B_VAL_DIM), F32)

    @pl.when(pl.program_id(2) == 0)
    def _():
        for bi in range(nbb):
            for hh in range(hpl):
                parts = [s0_ref[bi, hh] if j == hh else zpad for j in range(hpl)]
                state[bi, hh] = jnp.concatenate(parts, axis=0).T

    ri = lax.broadcasted_iota(jnp.int32, (c, LANE), 0)
    lane_c = lax.broadcasted_iota(jnp.int32, (c, LANE), 1)
    head_lanes = [(lane_c // B_KEY_DIM) == hh for hh in range(hpl)]
    ti = lax.broadcasted_iota(jnp.int32, (c, c), 0)
    si = lax.broadcasted_iota(jnp.int32, (c, c), 1)
    tril = jnp.where(si <= ti, 1.0, 0.0).astype(mx)
    sub3 = lax.broadcasted_iota(jnp.int32, (c // 8, 8, LANE), 1)
    levels = []
    half = c // 2
    while half >= 1:
        width = 2 * half
        sb = int(np.log2(width))
        upper = (ri & (width - 1)) >= half
        pair = ((ti >> sb) == (si >> sb)) & ((ti & (width - 1)) >= half) & ((si & (width - 1)) < half)
        levels.append((half, width, upper, pair))
        half //= 2

    seq = [(bi, ci) for bi in range(nbb) for ci in range(n_chunks)]
    loaded = []
    for bi, ci in seq:
        g = g_ref[bi, pl.ds(ci * c, c), :]
        g_hi = g.astype(BF16)
        g_lo = (g - g_hi.astype(F32)).astype(BF16)
        loaded.append((g, _dot(tril, g_hi.astype(mx)) + _dot(tril, g_lo.astype(mx))))

    prepared = []
    for (bi, ci), (g, b) in zip(seq, loaded):
        rows = pl.ds(ci * c, c)
        q = q_ref[bi, rows, :]
        k = k_ref[bi, rows, :]
        v32 = v_ref[bi, rows, :]
        b_last = b[c - 1:c, :]
        b3 = b.reshape(c // 8, 8, LANE)
        zs = []
        for half, width, upper, _ in levels:
            if half == 1:
                x = jnp.where(upper, g, 0.0)
            else:
                if half >= 8:
                    mid = jnp.concatenate(
                        [jnp.broadcast_to(b[m * width + half - 1:m * width + half, :], (width, LANE))
                         for m in range(c // width)], axis=0)
                else:
                    mid3 = jnp.broadcast_to(b3[:, half - 1:half, :], b3.shape)
                    for m in range(1, 8 // width):
                        r = m * width + half - 1
                        mid3 = jnp.where(sub3 >= m * width,
                                         jnp.broadcast_to(b3[:, r:r + 1, :], b3.shape), mid3)
                    mid = mid3.reshape(c, LANE)
                x = jnp.where(upper, b - mid, mid - b)
            zs.append(jnp.where(upper, q, k) * jnp.exp2(x))
        qd = q * jnp.exp2(b)
        kd = k * jnp.exp2(b_last - b)
        qk = q * k
        heads = []
        for hh in range(hpl):
            own = head_lanes[hh]
            cut = lambda a: jnp.where(own, a, 0.0).astype(mx)
            v_h = v32[:, hh * B_VAL_DIM:(hh + 1) * B_VAL_DIM]
            o_same = jnp.sum(jnp.where(own, qk, 0.0), axis=1, keepdims=True) * v_h
            heads.append((v_h.astype(mx), [cut(z) for z in zs], cut(qd), cut(kd), o_same))
        prepared.append((heads, jnp.exp2(b_last)))

    grams, updates = [], []
    for heads, _ in prepared:
        grams.append([[_dot(z, z, _NT) for z in zh] for _, zh, _, _, _ in heads])
        updates.append([_dot(v, kd, _TN) for v, _, _, kd, _ in heads])

    attns = []
    for gr in grams:
        per_head = []
        for gh in gr:
            attn = jnp.where(levels[0][3], gh[0], 0.0)
            for (_, _, _, pair), gm in zip(levels[1:], gh[1:]):
                attn = attn + jnp.where(pair, gm, 0.0)
            per_head.append(attn.astype(mx))
        attns.append(per_head)

    states = []
    for n, (bi, ci) in enumerate(seq):
        a_last = prepared[n][1]
        cur = []
        for hh in range(hpl):
            st = state[bi, hh] if ci == 0 else states[-1][hh][1]
            cur.append((st, a_last * st + updates[n][hh]))
            if ci == n_chunks - 1:
                state[bi, hh] = cur[-1][1]
        states.append(cur)

    for n, (bi, ci) in enumerate(seq):
        outs = []
        for hh, (v, _, qd, _, o_same) in enumerate(prepared[n][0]):
            outs.append(o_same + _dot(attns[n][hh], v) + _dot(qd, states[n][hh][0].astype(mx), _NT))
        o_ref[bi, pl.ds(ci * c, c), :] = jnp.concatenate(outs, axis=1).astype(o_ref.dtype)

    @pl.when(pl.program_id(2) == pl.num_programs(2) - 1)
    def _():
        for bi in range(nbb):
            for hh in range(hpl):
                sfin_ref[bi, hh] = state[bi, hh].T[hh * B_KEY_DIM:(hh + 1) * B_KEY_DIM, :]


def _gla(qb, kb, lg, vb, s0, chunk):
    batch, t, _ = qb.shape
    tc = min(t, 1024)
    n_chunks = tc // chunk
    nbb = 8 if (t == tc and batch % 8 == 0) else 1
    hpl = LANE // B_KEY_DIM
    seq = pl.BlockSpec((nbb, tc, LANE), lambda b, h, c: (b, c, h))
    seq_v = pl.BlockSpec((nbb, tc, hpl * B_VAL_DIM), lambda b, h, c: (b, c, h))
    st = pl.BlockSpec((nbb, hpl, B_KEY_DIM, B_VAL_DIM), lambda b, h, c: (b, h, 0, 0))
    return pl.pallas_call(
        functools.partial(_gla_kernel, chunk=chunk, n_chunks=n_chunks, nbb=nbb),
        grid=(batch // nbb, B_HEADS // hpl, t // tc),
        in_specs=[seq, seq, seq, seq_v, st],
        out_specs=[seq_v, st],
        out_shape=[jax.ShapeDtypeStruct((batch, t, B_VAL_WIDTH), BF16 if chunk >= 16 else F32),
                   jax.ShapeDtypeStruct((batch, B_HEADS, B_KEY_DIM, B_VAL_DIM), F32)],
        scratch_shapes=[pltpu.VMEM((nbb, hpl, B_VAL_DIM, LANE), F32)],
        compiler_params=pltpu.CompilerParams(
            dimension_semantics=("arbitrary", "arbitrary", "arbitrary"),
            vmem_limit_bytes=VMEM_LIMIT),
        name="gla_c%d" % chunk,
    )(qb, kb, lg, vb, s0)


def _merge_kernel(x_ref, oa_ref, sza_ref, ob_ref, szb_ref, gg_ref, wo_ref, y_ref):
    ya = oa_ref[...].astype(F32) * sza_ref[...].astype(F32)
    ob = ob_ref[...].astype(F32)
    parts = []
    for h in range(B_HEADS):
        oh = ob[:, h * B_VAL_DIM:(h + 1) * B_VAL_DIM]
        ms = jnp.mean(oh * oh, axis=-1, keepdims=True)
        parts.append(oh * lax.rsqrt(ms + EPS))
    yb = (jnp.concatenate(parts, axis=1) * gg_ref[...]) * szb_ref[...].astype(F32)
    cat = jnp.concatenate([ya, yb], axis=1).astype(BF16)
    y_ref[...] = x_ref[...] + _dot(cat, wo_ref[...])


def _merge(x2d, oa, sza, ob, szb, w):
    n = x2d.shape[0]
    tm = min(n, 512)
    row = lambda i: (i, 0)
    const = lambda i: (0, 0)
    return pl.pallas_call(
        _merge_kernel,
        grid=(n // tm,),
        in_specs=[pl.BlockSpec((tm, D_MODEL), row), pl.BlockSpec((tm, 512), row),
                  pl.BlockSpec((tm, 512), row), pl.BlockSpec((tm, 512), row),
                  pl.BlockSpec((tm, 512), row), pl.BlockSpec((1, 512), const),
                  pl.BlockSpec((D_MODEL, D_MODEL), const)],
        out_specs=pl.BlockSpec((tm, D_MODEL), row),
        out_shape=jax.ShapeDtypeStruct((n, D_MODEL), F32),
        compiler_params=pltpu.CompilerParams(dimension_semantics=("arbitrary",),
                                             vmem_limit_bytes=VMEM_LIMIT),
        name="merge_out",
    )(x2d, oa, sza, ob, szb, w["g_gla"], w["w_out"])


def _layer_weights(g_pre, w_in, g_q, g_k, w_a2, b_a, g_gla, w_out):
    o = _OFF
    w_t = w_in[:, o[0]:o[3]].T.astype(BF16)
    w_row = jnp.concatenate([
        w_in[:, o[3]:o[4]], w_in[:, o[7]:o[8]], w_in[:, o[4]:o[5]], w_in[:, o[5]:o[6]],
        w_in[:, o[6]:o[7]],
        jnp.pad(w_in[:, o[8]:o[9]], ((0, 0), (0, LANE - GATE_RANK)))], axis=1).astype(BF16)
    w_a2p = jnp.pad(w_a2, ((0, LANE - GATE_RANK), (0, 0))).astype(BF16)
    b_ap = b_a.reshape(1, B_KEY_WIDTH)
    gq_t = jnp.broadcast_to(jnp.tile(g_q, A_HEADS)[:, None], (A_WIDTH, MOBA_BLOCK))
    gk_t = jnp.broadcast_to(jnp.tile(g_k, A_HEADS)[:, None], (A_WIDTH, MOBA_BLOCK))
    slopes = jnp.asarray([2.0 ** (-8.0 * (h + 1) / A_HEADS) for h in range(A_HEADS)], F32)
    c = slopes * _LOG2E
    pieces = []
    rem = c
    for _ in range(3):
        pc = rem.astype(BF16).astype(F32)
        pieces.append(pc)
        rem = rem - pc
    qq = jnp.arange(MOBA_BLOCK, dtype=F32)
    qaug = jnp.zeros((A_HEADS, _QB, 8, MOBA_BLOCK), F32)
    for p_i, pc in enumerate(pieces):
        qaug = qaug.at[:, :, p_i, :].set(pc[:, None, None])
        qaug = qaug.at[:, :, 3 + p_i, :].set(pc[:, None, None])
    qpos = qq[None, None, :] + (MOBA_BLOCK * jnp.arange(_QB, dtype=F32))[None, :, None]
    qaug = qaug.at[:, :, 6, :].set(-c[:, None, None] * qpos)
    qaug = qaug.reshape(A_HEADS // 2, 2 * _QB, 8, MOBA_BLOCK)
    kaug = jnp.zeros((MOBA_BLOCK, LANE), F32)
    kaug = kaug.at[:, 0:3].set(qq[:, None]).at[:, 6].set(1.0).astype(BF16)
    return {
        "g_pre": g_pre.reshape(1, D_MODEL), "w_t": w_t, "w_row": w_row, "w_a2p": w_a2p, "b_ap": b_ap,
        "gq_t": gq_t, "gk_t": gk_t, "g_gla": g_gla.reshape(1, B_VAL_WIDTH),
        "w_out": w_out.astype(BF16), "qaug": qaug, "kaug": kaug, "slopes": slopes,
    }


def kernel(x_prompt, x_sample, cache_k, cache_v, state_gla, page_table, g_pre, w_in, g_q, g_k, w_a2, b_a, g_gla, w_out):
    depth = g_pre.shape[0]
    batch, t, _ = x_prompt.shape
    nb_s, t_new, _ = x_sample.shape
    cache_kt = jnp.transpose(cache_k, (0, 1, 3, 4, 2))
    cache_vt = jnp.transpose(cache_v, (0, 1, 3, 4, 2))
    yp = x_prompt.reshape(batch * t, D_MODEL)
    ys = x_sample.reshape(nb_s * t_new, D_MODEL)
    kp_l, vp_l, sp_l, ks_l, vs_l, ss_l = [], [], [], [], [], []
    for l in range(depth):
        w = _layer_weights(g_pre[l], w_in[l], g_q[l], g_k[l], w_a2[l], b_a[l], g_gla[l], w_out[l])
        qt, kt, vt, krow, kmean, sza, szb, qb, kb, vb, lg = _proj(yp, w, batch, sample=False)
        oa = _moba_prompt(qt, krow, vt, kmean, w)
        r3 = lambda a: a.reshape(batch, t, a.shape[-1])
        s0 = jnp.zeros((batch, B_HEADS, B_KEY_DIM, B_VAL_DIM), F32)
        ob, s_fin = _gla(r3(qb), r3(kb), r3(lg), r3(vb), s0, chunk=64)
        yp = _merge(yp, oa.reshape(batch * t, A_WIDTH), sza, ob.reshape(batch * t, 512), szb, w)
        kp_l.append(jnp.transpose(kt.reshape(batch, A_HEADS, A_HEAD_DIM, t), (0, 3, 1, 2)))
        vp_l.append(jnp.transpose(vt.reshape(batch, A_HEADS, A_HEAD_DIM, t), (0, 3, 1, 2)))
        sp_l.append(s_fin)
        qs, ks, vs, sza, szb, qb, kb, vb, lg = _proj(ys, w, nb_s, sample=True)
        h4 = lambda a: jnp.transpose(a.reshape(nb_s, t_new, A_HEADS, A_HEAD_DIM), (0, 2, 1, 3))
        slope_rows = jnp.broadcast_to(jnp.repeat(w["slopes"], t_new)[:, None], (A_HEADS * t_new, LANE))
        oa4 = _moba_sample(h4(qs), h4(ks), h4(vs), cache_kt[l:l + 1], cache_vt[l:l + 1],
                           page_table, slope_rows)
        oa = jnp.transpose(oa4, (0, 2, 1, 3)).reshape(nb_s * t_new, A_WIDTH)
        r3 = lambda a: a.reshape(nb_s, t_new, a.shape[-1])
        ob, s_new = _gla(r3(qb), r3(kb), r3(lg), r3(vb), state_gla[l], chunk=t_new)
        ys = _merge(ys, oa, sza, ob.reshape(nb_s * t_new, 512), szb, w)
        ks_l.append(ks.reshape(nb_s, t_new, A_HEADS, A_HEAD_DIM))
        vs_l.append(vs.reshape(nb_s, t_new, A_HEADS, A_HEAD_DIM))
        ss_l.append(s_new)
    return (yp.reshape(batch, t, D_MODEL), ys.reshape(nb_s, t_new, D_MODEL),
            jnp.stack(kp_l), jnp.stack(vp_l), jnp.stack(sp_l),
            jnp.stack(ks_l), jnp.stack(vs_l), jnp.stack(ss_l))
```

```python
import functools

import jax
import jax.numpy as jnp
import numpy as np
from jax import lax
from jax.experimental import pallas as pl
from jax.experimental.pallas import tpu as pltpu

F32 = jnp.float32
BF16 = jnp.bfloat16

D_MODEL = 1024
A_HEADS = 8
A_HEAD_DIM = 64
A_WIDTH = A_HEADS * A_HEAD_DIM
MOBA_BLOCK = 256
MOBA_TOPK = 3
B_HEADS = 4
B_KEY_DIM = 64
B_VAL_DIM = 128
B_KEY_WIDTH = B_HEADS * B_KEY_DIM
B_VAL_WIDTH = B_HEADS * B_VAL_DIM
GATE_RANK = 16
GATE_TAU = 16.0
PAGE_SIZE = 128
EPS = 1e-6
NEG = -1e30

LANE = 128
VMEM_LIMIT = 56 * 1024 * 1024

_OFF = np.cumsum([0, A_WIDTH, A_WIDTH, A_WIDTH, A_WIDTH, B_KEY_WIDTH, B_KEY_WIDTH,
                  B_VAL_WIDTH, B_VAL_WIDTH, GATE_RANK]).tolist()
_ROW_W = 3 * 512 + 2 * B_KEY_WIDTH + LANE
_AUG0 = 128
_MASK0 = 136
_LOG2E = 1.4426950408889634
_PROJ_ROWS = 512


def _dot(a, b, dims=(((1,), (0,)), ((), ())), precision=None):
    return lax.dot_general(a, b, dims, precision=precision, preferred_element_type=F32)


_NT = (((1,), (1,)), ((), ()))
_TN = (((0,), (0,)), ((), ()))


def _silu(x):
    return x / (1.0 + jnp.exp(-x))


def _log_sigmoid(x):
    return jnp.minimum(x, 0.0) - jnp.log1p(jnp.exp(-jnp.abs(x)))


def _proj_kernel(x_ref, gpre_ref, wt_ref, wrow_ref, wa2_ref, ba_ref, gq_ref, gk_ref, *out_refs,
                 tm, sample):
    x = x_ref[...]
    ms = jnp.mean(x * x, axis=-1, keepdims=True)
    h = ((x * lax.rsqrt(ms + EPS)) * gpre_ref[...]).astype(BF16)

    pt = _dot(wt_ref[...], h, _NT)

    def head_norm(t, g):
        t3 = t.reshape(A_HEADS, A_HEAD_DIM, tm)
        ss = jnp.mean(t3 * t3, axis=1, keepdims=True)
        return (t3 * lax.rsqrt(ss + EPS)).reshape(A_WIDTH, tm) * g

    q_t = head_norm(pt[0:A_WIDTH], gq_ref[...]) * (A_HEAD_DIM ** -0.5)
    k_t = head_norm(pt[A_WIDTH:2 * A_WIDTH], gk_ref[...])
    v_t = pt[2 * A_WIDTH:3 * A_WIDTH]

    def seg(i0, i1):
        return _dot(h, wrow_ref[:, i0:i1])

    sza = _silu(seg(0, 512)).astype(BF16)
    szb = _silu(seg(512, 1024)).astype(BF16)
    qk = seg(1024, 1536)
    qb = qk[:, 0:B_KEY_WIDTH] * (B_KEY_DIM ** -0.5)
    kb = qk[:, B_KEY_WIDTH:2 * B_KEY_WIDTH]
    vb = seg(1536, 2048)
    ab = seg(2048, 2176).astype(BF16)
    pre = _dot(ab, wa2_ref[...]) + ba_ref[...]
    lg = _log_sigmoid(pre) * (_LOG2E / GATE_TAU)

    if sample:
        (q_ref, k_ref, v_ref, sza_ref, szb_ref, qb_ref, kb_ref, vb_ref, lg_ref) = out_refs
        q_ref[...] = q_t.T
        k_ref[...] = k_t.T
        v_ref[...] = v_t.T
    else:
        (qt_ref, kt_ref, vt_ref, krow_ref, kmean_ref,
         sza_ref, szb_ref, qb_ref, kb_ref, vb_ref, lg_ref) = out_refs
        qt_ref[0] = q_t
        kt_ref[0] = k_t
        vt_ref[0] = v_t
        k_row = k_t.T
        krow_ref[0] = k_row.astype(BF16)
        kmean_ref[0] = jnp.mean(k_row.reshape(tm // MOBA_BLOCK, MOBA_BLOCK, A_WIDTH), axis=1)
    sza_ref[...] = sza
    szb_ref[...] = szb
    qb_ref[...] = qb
    kb_ref[...] = kb
    vb_ref[...] = vb
    lg_ref[...] = lg


def _proj(x2d, w, batch, sample):
    n = x2d.shape[0]
    tm = min(_PROJ_ROWS, n)
    nt = n // tm
    bpt = tm // MOBA_BLOCK
    const = lambda i: (0, 0)
    row = lambda i: (i, 0)
    in_specs = [
        pl.BlockSpec((tm, D_MODEL), row),
        pl.BlockSpec((1, D_MODEL), const),
        pl.BlockSpec((3 * A_WIDTH, D_MODEL), const),
        pl.BlockSpec((D_MODEL, _ROW_W), const),
        pl.BlockSpec((LANE, B_KEY_WIDTH), const),
        pl.BlockSpec((1, B_KEY_WIDTH), const),
        pl.BlockSpec((A_WIDTH, tm), const),
        pl.BlockSpec((A_WIDTH, tm), const),
    ]
    row_specs = [pl.BlockSpec((tm, wd), row) for wd in (512, 512, B_KEY_WIDTH, B_KEY_WIDTH, 512, B_KEY_WIDTH)]
    row_shapes = [jax.ShapeDtypeStruct((n, wd), dt) for wd, dt in
                  ((512, BF16), (512, BF16), (B_KEY_WIDTH, F32), (B_KEY_WIDTH, F32), (512, F32),
                   (B_KEY_WIDTH, F32))]
    if sample:
        out_specs = [pl.BlockSpec((tm, 512), row)] * 3 + row_specs
        out_shape = [jax.ShapeDtypeStruct((n, 512), F32)] * 3 + row_shapes
    else:
        t = n // batch
        tpb = t // tm
        feat = lambda i: (i // tpb, 0, i % tpb)
        out_specs = ([pl.BlockSpec((1, A_WIDTH, tm), feat)] * 3
                     + [pl.BlockSpec((1, tm, 512), lambda i: (i // tpb, i % tpb, 0)),
                        pl.BlockSpec((1, bpt, 512), lambda i: (i, 0, 0))]
                     + row_specs)
        out_shape = ([jax.ShapeDtypeStruct((batch, A_WIDTH, t), F32)] * 3
                     + [jax.ShapeDtypeStruct((batch, t, 512), BF16),
                        jax.ShapeDtypeStruct((nt, bpt, 512), F32)]
                     + row_shapes)
    return pl.pallas_call(
        functools.partial(_proj_kernel, tm=tm, sample=sample),
        grid=(nt,),
        in_specs=in_specs,
        out_specs=out_specs,
        out_shape=out_shape,
        compiler_params=pltpu.CompilerParams(dimension_semantics=("arbitrary",),
                                             vmem_limit_bytes=VMEM_LIMIT),
        name="proj_sample" if sample else "proj_prompt",
    )(x2d, w["g_pre"], w["w_t"], w["w_row"], w["w_a2p"], w["b_ap"], w["gq_t"], w["gk_t"])


def _top3_rows(g, n):
    idx = lax.broadcasted_iota(jnp.int32, g.shape, 0)
    sel = jnp.zeros(g.shape, F32)
    for _ in range(MOBA_TOPK):
        m = jnp.max(g, axis=0, keepdims=True)
        first = jnp.min(jnp.where(g == m, idx, n), axis=0, keepdims=True)
        pick = idx == first
        sel = jnp.where(pick, 1.0, sel)
        g = jnp.where(pick, -jnp.inf, g)
    return sel


_QB = 8


def _moba_prompt_kernel(qt_ref, krow_ref, vt_ref, kmean_ref, qaug_ref, kaug_ref, o_ref,
                        qop_ref, acc_ref, m_ref, s_ref, cm_ref, *, nb):
    blk = MOBA_BLOCK
    qw = _QB * blk
    i0 = pl.program_id(2) * _QB

    qt = qt_ref[0]
    row128 = lax.broadcasted_iota(jnp.int32, (2 * A_HEAD_DIM, qw), 0)
    lane128 = lax.broadcasted_iota(jnp.int32, (nb, LANE), 1)
    bidx = lax.broadcasted_iota(jnp.int32, (nb, qw), 0)
    iq = i0 + lax.broadcasted_iota(jnp.int32, (nb, qw), 1) // blk
    kmean = kmean_ref[0]
    qt_hi = qt.astype(BF16)
    qt_lo = (qt - qt_hi.astype(F32)).astype(BF16)

    for hh in range(2):
        in_head = (row128 >= hh * A_HEAD_DIM) & (row128 < (hh + 1) * A_HEAD_DIM)
        q_h = jnp.where(in_head, qt * _LOG2E, 0.0)
        km_h = jnp.where((lane128 >= hh * A_HEAD_DIM) & (lane128 < (hh + 1) * A_HEAD_DIM), kmean, 0.0)
        km_hi = km_h.astype(BF16)
        km_lo = (km_h - km_hi.astype(F32)).astype(BF16)
        gate = _dot(km_hi, qt_hi) + _dot(km_hi, qt_lo) + _dot(km_lo, qt_hi)
        past = bidx < iq
        sel = _top3_rows(jnp.where(past, gate, NEG), nb)
        keep = ((sel > 0.5) & past) | (bidx == iq)
        maskbias = jnp.where(keep, 0.0, NEG)
        for qb in range(_QB):
            c = hh * _QB + qb
            cols = slice(qb * blk, (qb + 1) * blk)
            qop = jnp.concatenate(
                [q_h[:, cols], qaug_ref[0, c], maskbias[:, cols],
                 jnp.zeros((2 * LANE - _MASK0 - nb, blk), F32)], axis=0)
            qop_ref[c] = qop.astype(BF16)
            acc_ref[c] = jnp.zeros(acc_ref.shape[1:], F32)
            m_ref[c] = jnp.full(m_ref.shape[1:], -jnp.inf, F32)

    lane_k = lax.broadcasted_iota(jnp.int32, (blk, LANE), 1)
    kaug_base = kaug_ref[...]
    ones_rows = jnp.ones((8, blk), BF16)
    kk = lax.broadcasted_iota(jnp.int32, (blk, blk), 0)
    qq = lax.broadcasted_iota(jnp.int32, (blk, blk), 1)

    def scores(j, slot, chains):
        off = pl.multiple_of(j * blk, blk)
        shift = ((j - i0) * blk).astype(F32)
        kaug = jnp.where((lane_k >= 3) & (lane_k < 6), shift.astype(BF16),
                         jnp.where(lane_k == 8 + j, jnp.ones((), BF16), kaug_base))
        kop = jnp.concatenate([krow_ref[0, pl.ds(off, blk), :], kaug], axis=1)
        for c in chains:
            s = _dot(kop, qop_ref[c])
            s_ref[slot, c] = s
            cm_ref[slot, c] = jnp.max(s.reshape(blk // 8, 8, blk), axis=0)

    def absorb(j, slot, chains):
        off = pl.multiple_of(j * blk, blk)
        vops = []
        for hh in range(2):
            v_t = vt_ref[0, hh * A_HEAD_DIM:(hh + 1) * A_HEAD_DIM, pl.ds(off, blk)].astype(BF16)
            vops.append(jnp.concatenate([v_t, ones_rows], axis=0))
        for c, causal in chains:
            s = s_ref[slot, c]
            if causal:
                s = jnp.where(kk <= qq, s, NEG)
            m_prev = m_ref[c][0:1]
            cmax = s if causal else cm_ref[slot, c]
            m_new = jnp.maximum(m_prev, jnp.max(cmax, axis=0, keepdims=True))
            alpha = jnp.exp2(m_prev - m_new)
            p = jnp.exp2(s - m_new).astype(BF16)
            acc_ref[c] = alpha * acc_ref[c] + _dot(vops[c // _QB], p)
            m_ref[c] = jnp.broadcast_to(m_new, m_ref.shape[1:])

    all_c = list(range(2 * _QB))
    plain = [(c, False) for c in all_c]
    scores(0, 0, all_c)

    def body(jj, carry):
        j = _QB * jj
        for d in range(_QB):
            scores(j + d + 1, (d + 1) % 2, all_c)
            absorb(j + d, d % 2, plain)
        return carry

    lax.fori_loop(0, i0 // _QB, body, 0)
    for d in range(_QB):
        if d + 1 < _QB:
            scores(i0 + d + 1, (d + 1) % 2, [c for c in all_c if c % _QB >= d + 1])
        absorb(i0 + d, d % 2, [(c, c % _QB == d) for c in all_c if c % _QB >= d])

    for qb in range(_QB):
        outs = []
        for hh in range(2):
            acc = acc_ref[hh * _QB + qb]
            outs.append(acc[0:A_HEAD_DIM] / acc[A_HEAD_DIM:A_HEAD_DIM + 1])
        o_ref[0, qb * blk:(qb + 1) * blk, :] = jnp.concatenate(outs, axis=0).T.astype(o_ref.dtype)


def _moba_prompt(qt, krow, vt, kmean, w):
    batch, _, t = qt.shape
    nb = t // MOBA_BLOCK
    blk = MOBA_BLOCK
    qw = _QB * blk
    return pl.pallas_call(
        functools.partial(_moba_prompt_kernel, nb=nb),
        grid=(batch, A_HEADS // 2, nb // _QB),
        in_specs=[
            pl.BlockSpec((1, 2 * A_HEAD_DIM, qw), lambda b, hp, i: (b, hp, i)),
            pl.BlockSpec((1, t, LANE), lambda b, hp, i: (b, 0, hp)),
            pl.BlockSpec((1, 2 * A_HEAD_DIM, t), lambda b, hp, i: (b, hp, 0)),
            pl.BlockSpec((1, nb, LANE), lambda b, hp, i: (b, 0, hp)),
            pl.BlockSpec((1, 2 * _QB, 8, blk), lambda b, hp, i: (hp, 0, 0, 0)),
            pl.BlockSpec((blk, LANE), lambda b, hp, i: (0, 0)),
        ],
        out_specs=pl.BlockSpec((1, qw, LANE), lambda b, hp, i: (b, i, hp)),
        out_shape=jax.ShapeDtypeStruct((batch, t, A_WIDTH), BF16),
        scratch_shapes=[
            pltpu.VMEM((2 * _QB, 2 * LANE, blk), BF16),
            pltpu.VMEM((2 * _QB, A_HEAD_DIM + 8, blk), F32),
            pltpu.VMEM((2 * _QB, 8, blk), F32),
            pltpu.VMEM((2, 2 * _QB, blk, blk), F32),
            pltpu.VMEM((2, 2 * _QB, 8, blk), F32),
        ],
        compiler_params=pltpu.CompilerParams(
            dimension_semantics=("arbitrary", "arbitrary", "arbitrary"),
            vmem_limit_bytes=VMEM_LIMIT),
        name="moba_prompt",
    )(qt, krow, vt, kmean.reshape(batch, nb, A_WIDTH), w["qaug"], w["kaug"])


def _moba_sample_kernel(pt_ref, q_ref, qn_ref, kn_ref, vn_ref, slope_ref, ck_hbm, cv_hbm, o_ref,
                        kbuf, sall, gate_ref, idx_v, idx_s, vbuf, sem_k, sem_v, sem_i,
                        *, n_pages, npg, ring):
    b = pl.program_id(0)
    nbatch = pl.num_programs(0)
    nxt = jnp.minimum(b + 1, nbatch - 1)
    n_chunks = n_pages // npg
    cw = npg * PAGE_SIZE
    bpc = cw // MOBA_BLOCK
    past = n_pages * PAGE_SIZE
    n_blocks = past // MOBA_BLOCK
    t_new = q_ref.shape[2]
    rows = A_HEADS * t_new
    ppb = MOBA_BLOCK // PAGE_SIZE
    slot_b = b % 2

    def k_copy(s, c, p):
        page = pt_ref[s, c * npg + p]
        return pltpu.make_async_copy(
            ck_hbm.at[0, page], kbuf.at[c % ring, :, :, pl.ds(p * PAGE_SIZE, PAGE_SIZE)],
            sem_k.at[c % ring])

    def start_chunk(s, c):
        for p in range(npg):
            k_copy(s, c, p).start()

    def start_ring(s):
        for c in range(ring):
            start_chunk(s, c)

    lane_g = lax.broadcasted_iota(jnp.int32, (t_new, LANE), 1)

    def split_q(qr):
        ops = []
        for h in range(A_HEADS):
            qh = qr[0, h]
            hi = qh.astype(BF16)
            ops.append(jnp.concatenate([hi, (qh - hi.astype(F32)).astype(BF16)], axis=0))
        return ops

    def sweep_chunk(s, q_ops, slot, c):
        for p in range(npg):
            k_copy(s, c, p).wait()
        for h in range(A_HEADS):
            s2 = _dot(q_ops[h], kbuf[c % ring, h].astype(BF16))
            sc = s2[0:t_new] + s2[t_new:2 * t_new]
            sall[slot, h * t_new:(h + 1) * t_new, c * cw:(c + 1) * cw] = sc
            g = gate_ref[h * t_new:(h + 1) * t_new, :]
            for jb in range(bpc):
                t2 = (sc[:, jb * MOBA_BLOCK:jb * MOBA_BLOCK + LANE]
                      + sc[:, jb * MOBA_BLOCK + LANE:(jb + 1) * MOBA_BLOCK])
                g = jnp.where(lane_g == c * bpc + jb, jnp.sum(t2, axis=1, keepdims=True), g)
            gate_ref[h * t_new:(h + 1) * t_new, :] = g
        if c + ring < n_chunks:
            start_chunk(s, c + ring)

    picks = [(t, r) for t in range(t_new) for r in range(MOBA_TOPK)]

    def v_copy(s, h, t, r, pg):
        bsel = idx_s[h * t_new + t, r]
        page = pt_ref[s, ppb * bsel + pg]
        col = ((t * MOBA_TOPK + r) * ppb + pg) * PAGE_SIZE
        return pltpu.make_async_copy(
            cv_hbm.at[0, page, h], vbuf.at[h, :, pl.ds(col, PAGE_SIZE)], sem_v.at[h])

    def v_start(s, h):
        for t, r in picks:
            for pg in range(ppb):
                v_copy(s, h, t, r, pg).start(priority=1)

    ahead = 2

    def select_blocks(s):
        lane_r = lax.broadcasted_iota(jnp.int32, (rows, LANE), 1)
        g = jnp.where(lane_r < n_blocks, gate_ref[...], -jnp.inf)
        idx = jnp.zeros((rows, LANE), jnp.int32)
        for r in range(MOBA_TOPK):
            m = jnp.max(g, axis=1, keepdims=True)
            first = jnp.min(jnp.where(g == m, lane_r, LANE), axis=1, keepdims=True)
            g = jnp.where(lane_r == first, -jnp.inf, g)
            idx = jnp.where(lane_r == r, first, idx)
        idx_v[...] = idx
        cp_i = pltpu.make_async_copy(idx_v, idx_s, sem_i)
        cp_i.start()
        cp_i.wait()
        for h0 in range(ahead):
            v_start(s, h0)

    @pl.when(b == 0)
    def _():
        start_ring(0)
        gate_ref[...] = jnp.zeros(gate_ref.shape, F32)
        q0 = split_q(q_ref)
        for c in range(n_chunks):
            sweep_chunk(0, q0, 0, c)
        start_ring(nxt)
        select_blocks(0)

    sub = lax.broadcasted_iota(jnp.int32, (t_new, MOBA_BLOCK), 0)
    klane = lax.broadcasted_iota(jnp.int32, (t_new, MOBA_BLOCK), 1)
    trow = lax.broadcasted_iota(jnp.int32, (t_new, t_new), 0)
    tcol = lax.broadcasted_iota(jnp.int32, (t_new, t_new), 1)

    def finish_head(h):
        if h + ahead < A_HEADS:
            v_start(b, h + ahead)
        r0 = h * t_new
        slope = slope_ref[r0:r0 + t_new, :][:, 0:1]
        pieces = []
        for t, r in picks:
            bsel = idx_s[r0 + t, r]
            off = pl.multiple_of(bsel * MOBA_BLOCK, MOBA_BLOCK)
            dist = ((past + t - bsel * MOBA_BLOCK) - klane).astype(F32)
            sc = sall[slot_b, r0:r0 + t_new, pl.ds(off, MOBA_BLOCK)] - slope * dist
            pieces.append(jnp.where(sub == t, sc, NEG))
        s_sel = jnp.concatenate(pieces, axis=1)
        s_new = _dot(q_ref[0, h], kn_ref[0, h], _NT) - slope * (trow - tcol).astype(F32)
        s_new = jnp.where(tcol <= trow, s_new, NEG)
        m = jnp.maximum(jnp.max(s_sel, axis=1, keepdims=True), jnp.max(s_new, axis=1, keepdims=True))
        p_sel = jnp.exp(s_sel - m)
        p_new = jnp.exp(s_new - m)
        l = jnp.sum(p_sel, axis=1, keepdims=True) + jnp.sum(p_new, axis=1, keepdims=True)
        for t, r in picks:
            for pg in range(ppb):
                v_copy(b, h, t, r, pg).wait()
        acc = _dot(p_sel.astype(BF16), vbuf[h].astype(BF16), _NT) + _dot(p_new, vn_ref[0, h])
        o_ref[0, h] = acc / l

    gate_ref[...] = jnp.zeros(gate_ref.shape, F32)
    qn = split_q(qn_ref)
    for i in range(max(n_chunks, A_HEADS)):
        if i < n_chunks:
            sweep_chunk(nxt, qn, 1 - slot_b, i)
        if i < A_HEADS:
            finish_head(i)

    @pl.when(b + 1 < nbatch)
    def _():
        start_ring(jnp.minimum(b + 2, nbatch - 1))
        select_blocks(nxt)


def _moba_sample(q4, kn4, vn4, cache_kt, cache_vt, page_table, slope_rows):
    nbatch, _, t_new, _ = q4.shape
    n_pages = page_table.shape[1]
    npg = 16
    ring = min(4, n_pages // npg)
    rows = A_HEADS * t_new
    past = n_pages * PAGE_SIZE
    shape4 = (1, A_HEADS, t_new, A_HEAD_DIM)
    blk4 = pl.BlockSpec(shape4, lambda b, pt: (b, 0, 0, 0))
    nxt4 = pl.BlockSpec(shape4, lambda b, pt: (jnp.minimum(b + 1, nbatch - 1), 0, 0, 0))
    return pl.pallas_call(
        functools.partial(_moba_sample_kernel, n_pages=n_pages, npg=npg, ring=ring),
        grid_spec=pltpu.PrefetchScalarGridSpec(
            num_scalar_prefetch=1,
            grid=(nbatch,),
            in_specs=[blk4, nxt4, blk4, blk4,
                      pl.BlockSpec((rows, LANE), lambda b, pt: (0, 0)),
                      pl.BlockSpec(memory_space=pl.ANY),
                      pl.BlockSpec(memory_space=pl.ANY)],
            out_specs=blk4,
            scratch_shapes=[
                pltpu.VMEM((ring, A_HEADS, A_HEAD_DIM, npg * PAGE_SIZE), F32),
                pltpu.VMEM((2, rows, past), F32),
                pltpu.VMEM((rows, LANE), F32),
                pltpu.VMEM((rows, LANE), jnp.int32),
                pltpu.SMEM((rows, LANE), jnp.int32),
                pltpu.VMEM((A_HEADS, A_HEAD_DIM, t_new * MOBA_TOPK * MOBA_BLOCK), F32),
                pltpu.SemaphoreType.DMA((ring,)),
                pltpu.SemaphoreType.DMA((A_HEADS,)),
                pltpu.SemaphoreType.DMA,
            ]),
        out_shape=jax.ShapeDtypeStruct((nbatch, A_HEADS, t_new, A_HEAD_DIM), F32),
        compiler_params=pltpu.CompilerParams(dimension_semantics=("arbitrary",),
                                             vmem_limit_bytes=VMEM_LIMIT),
        name="moba_sample",
    )(page_table, q4, q4, kn4, vn4, slope_rows, cache_kt, cache_vt)


def _gla_kernel(q_ref, k_ref, g_ref, v_ref, s0_ref, o_ref, sfin_ref, state, *, chunk, n_chunks, nbb):
    c = chunk
    mx = BF16 if c >= 16 else F32
    hpl = LANE // B_KEY_DIM
    zpad = jnp.zeros((B_KEY_DIM, B_VAL_DIM), F32)

    @pl.when(pl.program_id(2) == 0)
    def _():
        for bi in range(nbb):
            for hh in range(hpl):
                parts = [s0_ref[bi, hh] if j == hh else zpad for j in range(hpl)]
                state[bi, hh] = jnp.concatenate(parts, axis=0).T

    ri = lax.broadcasted_iota(jnp.int32, (c, LANE), 0)
    lane_c = lax.broadcasted_iota(jnp.int32, (c, LANE), 1)
    head_lanes = [(lane_c // B_KEY_DIM) == hh for hh in range(hpl)]
    ti = lax.broadcasted_iota(jnp.int32, (c, c), 0)
    si = lax.broadcasted_iota(jnp.int32, (c, c), 1)
    tril = jnp.where(si <= ti, 1.0, 0.0).astype(mx)
    sub3 = lax.broadcasted_iota(jnp.int32, (c // 8, 8, LANE), 1)
    levels = []
    half = c // 2
    while half >= 1:
        width = 2 * half
        sb = int(np.log2(width))
        upper = (ri & (width - 1)) >= half
        pair = ((ti >> sb) == (si >> sb)) & ((ti & (width - 1)) >= half) & ((si & (width - 1)) < half)
        levels.append((half, width, upper, pair))
        half //= 2

    seq = [(bi, ci) for bi in range(nbb) for ci in range(n_chunks)]
    loaded = []
    for bi, ci in seq:
        g = g_ref[bi, pl.ds(ci * c, c), :]
        g_hi = g.astype(BF16)
        g_lo = (g - g_hi.astype(F32)).astype(BF16)
        loaded.append((g, _dot(tril, g_hi.astype(mx)) + _dot(tril, g_lo.astype(mx))))

    prepared = []
    for (bi, ci), (g, b) in zip(seq, loaded):
        rows = pl.ds(ci * c, c)
        q = q_ref[bi, rows, :]
        k = k_ref[bi, rows, :]
        v32 = v_ref[bi, rows, :]
        b_last = b[c - 1:c, :]
        b3 = b.reshape(c // 8, 8, LANE)
        zs = []
        for half, width, upper, _ in levels:
            if half == 1:
                x = jnp.where(upper, g, 0.0)
            else:
                if half >= 8:
                    mid = jnp.concatenate(
                        [jnp.broadcast_to(b[m * width + half - 1:m * width + half, :], (width, LANE))
                         for m in range(c // width)], axis=0)
                else:
                    mid3 = jnp.broadcast_to(b3[:, half - 1:half, :], b3.shape)
                    for m in range(1, 8 // width):
                        r = m * width + half - 1
                        mid3 = jnp.where(sub3 >= m * width,
                                         jnp.broadcast_to(b3[:, r:r + 1, :], b3.shape), mid3)
                    mid = mid3.reshape(c, LANE)
                x = jnp.where(upper, b - mid, mid - b)
            zs.append(jnp.where(upper, q, k) * jnp.exp2(x))
        qd = q * jnp.exp2(b)
        kd = k * jnp.exp2(b_last - b)
        qk = q * k
        heads = []
        for hh in range(hpl):
            own = head_lanes[hh]
            cut = lambda a: jnp.where(own, a, 0.0).astype(mx)
            v_h = v32[:, hh * B_VAL_DIM:(hh + 1) * B_VAL_DIM]
            o_same = jnp.sum(jnp.where(own, qk, 0.0), axis=1, keepdims=True) * v_h
            heads.append((v_h.astype(mx), [cut(z) for z in zs], cut(qd), cut(kd), o_same))
        prepared.append((heads, jnp.exp2(b_last)))

    grams, updates = [], []
    for heads, _ in prepared:
        grams.append([[_dot(z, z, _NT) for z in zh] for _, zh, _, _, _ in heads])
        updates.append([_dot(v, kd, _TN) for v, _, _, kd, _ in heads])

    attns = []
    for gr in grams:
        per_head = []
        for gh in gr:
            attn = jnp.where(levels[0][3], gh[0], 0.0)
            for (_, _, _, pair), gm in zip(levels[1:], gh[1:]):
                attn = attn + jnp.where(pair, gm, 0.0)
            per_head.append(attn.astype(mx))
        attns.append(per_head)

    states = []
    for n, (bi, ci) in enumerate(seq):
        a_last = prepared[n][1]
        cur = []
        for hh in range(hpl):
            st = state[bi, hh] if ci == 0 else states[-1][hh][1]
            cur.append((st, a_last * st + updates[n][hh]))
            if ci == n_chunks - 1:
                state[bi, hh] = cur[-1][1]
        states.append(cur)

    for n, (bi, ci) in enumerate(seq):
        outs = []
        for hh, (v, _, qd, _, o_same) in enumerate(prepared[n][0]):
            outs.append(o_same + _dot(attns[n][hh], v) + _dot(qd, states[n][hh][0].astype(mx), _NT))
        o_ref[bi, pl.ds(ci * c, c), :] = jnp.concatenate(outs, axis=1).astype(o_ref.dtype)

    @pl.when(pl.program_id(2) == pl.num_programs(2) - 1)
    def _():
        for bi in range(nbb):
            for hh in range(hpl):
                sfin_ref[bi, hh] = state[bi, hh].T[hh * B_KEY_DIM:(hh + 1) * B_KEY_DIM, :]


def _gla(qb, kb, lg, vb, s0, chunk):
    batch, t, _ = qb.shape
    tc = min(t, 1024)
    n_chunks = tc // chunk
    nbb = 8 if (t == tc and batch % 8 == 0) else 1
    hpl = LANE // B_KEY_DIM
    seq = pl.BlockSpec((nbb, tc, LANE), lambda b, h, c: (b, c, h))
    seq_v = pl.BlockSpec((nbb, tc, hpl * B_VAL_DIM), lambda b, h, c: (b, c, h))
    st = pl.BlockSpec((nbb, hpl, B_KEY_DIM, B_VAL_DIM), lambda b, h, c: (b, h, 0, 0))
    return pl.pallas_call(
        functools.partial(_gla_kernel, chunk=chunk, n_chunks=n_chunks, nbb=nbb),
        grid=(batch // nbb, B_HEADS // hpl, t // tc),
        in_specs=[seq, seq, seq, seq_v, st],
        out_specs=[seq_v, st],
        out_shape=[jax.ShapeDtypeStruct((batch, t, B_VAL_WIDTH), BF16 if chunk >= 16 else F32),
                   jax.ShapeDtypeStruct((batch, B_HEADS, B_KEY_DIM, B_VAL_DIM), F32)],
        scratch_shapes=[pltpu.VMEM((nbb, hpl, B_VAL_DIM, LANE), F32)],
        compiler_params=pltpu.CompilerParams(
            dimension_semantics=("arbitrary", "arbitrary", "arbitrary"),
            vmem_limit_bytes=VMEM_LIMIT),
        name="gla_c%d" % chunk,
    )(qb, kb, lg, vb, s0)


def _merge_kernel(x_ref, oa_ref, sza_ref, ob_ref, szb_ref, gg_ref, wo_ref, y_ref):
    ya = oa_ref[...].astype(F32) * sza_ref[...].astype(F32)
    ob = ob_ref[...].astype(F32)
    parts = []
    for h in range(B_HEADS):
        oh = ob[:, h * B_VAL_DIM:(h + 1) * B_VAL_DIM]
        ms = jnp.mean(oh * oh, axis=-1, keepdims=True)
        parts.append(oh * lax.rsqrt(ms + EPS))
    yb = (jnp.concatenate(parts, axis=1) * gg_ref[...]) * szb_ref[...].astype(F32)
    cat = jnp.concatenate([ya, yb], axis=1).astype(BF16)
    y_ref[...] = x_ref[...] + _dot(cat, wo_ref[...])


def _merge(x2d, oa, sza, ob, szb, w):
    n = x2d.shape[0]
    tm = min(n, 1024)
    row = lambda i: (i, 0)
    const = lambda i: (0, 0)
    return pl.pallas_call(
        _merge_kernel,
        grid=(n // tm,),
        in_specs=[pl.BlockSpec((tm, D_MODEL), row), pl.BlockSpec((tm, 512), row),
                  pl.BlockSpec((tm, 512), row), pl.BlockSpec((tm, 512), row),
                  pl.BlockSpec((tm, 512), row), pl.BlockSpec((1, 512), const),
                  pl.BlockSpec((D_MODEL, D_MODEL), const)],
        out_specs=pl.BlockSpec((tm, D_MODEL), row),
        out_shape=jax.ShapeDtypeStruct((n, D_MODEL), F32),
        compiler_params=pltpu.CompilerParams(dimension_semantics=("arbitrary",),
                                             vmem_limit_bytes=VMEM_LIMIT),
        name="merge_out",
    )(x2d, oa, sza, ob, szb, w["g_gla"], w["w_out"])


def _layer_weights(g_pre, w_in, g_q, g_k, w_a2, b_a, g_gla, w_out):
    o = _OFF
    w_t = w_in[:, o[0]:o[3]].T.astype(BF16)
    w_row = jnp.concatenate([
        w_in[:, o[3]:o[4]], w_in[:, o[7]:o[8]], w_in[:, o[4]:o[5]], w_in[:, o[5]:o[6]],
        w_in[:, o[6]:o[7]],
        jnp.pad(w_in[:, o[8]:o[9]], ((0, 0), (0, LANE - GATE_RANK)))], axis=1).astype(BF16)
    w_a2p = jnp.pad(w_a2, ((0, LANE - GATE_RANK), (0, 0))).astype(BF16)
    b_ap = b_a.reshape(1, B_KEY_WIDTH)
    gq_t = jnp.broadcast_to(jnp.tile(g_q, A_HEADS)[:, None], (A_WIDTH, _PROJ_ROWS))
    gk_t = jnp.broadcast_to(jnp.tile(g_k, A_HEADS)[:, None], (A_WIDTH, _PROJ_ROWS))
    slopes = jnp.asarray([2.0 ** (-8.0 * (h + 1) / A_HEADS) for h in range(A_HEADS)], F32)
    c = slopes * _LOG2E
    pieces = []
    rem = c
    for _ in range(3):
        pc = rem.astype(BF16).astype(F32)
        pieces.append(pc)
        rem = rem - pc
    qq = jnp.arange(MOBA_BLOCK, dtype=F32)
    qaug = jnp.zeros((A_HEADS, _QB, 8, MOBA_BLOCK), F32)
    for p_i, pc in enumerate(pieces):
        qaug = qaug.at[:, :, p_i, :].set(pc[:, None, None])
        qaug = qaug.at[:, :, 3 + p_i, :].set(pc[:, None, None])
    qpos = qq[None, None, :] + (MOBA_BLOCK * jnp.arange(_QB, dtype=F32))[None, :, None]
    qaug = qaug.at[:, :, 6, :].set(-c[:, None, None] * qpos)
    qaug = qaug.reshape(A_HEADS // 2, 2 * _QB, 8, MOBA_BLOCK)
    kaug = jnp.zeros((MOBA_BLOCK, LANE), F32)
    kaug = kaug.at[:, 0:3].set(qq[:, None]).at[:, 6].set(1.0).astype(BF16)
    return {
        "g_pre": g_pre.reshape(1, D_MODEL), "w_t": w_t, "w_row": w_row, "w_a2p": w_a2p, "b_ap": b_ap,
        "gq_t": gq_t, "gk_t": gk_t, "g_gla": g_gla.reshape(1, B_VAL_WIDTH),
        "w_out": w_out.astype(BF16), "qaug": qaug, "kaug": kaug, "slopes": slopes,
    }


def kernel(x_prompt, x_sample, cache_k, cache_v, state_gla, page_table, g_pre, w_in, g_q, g_k, w_a2, b_a, g_gla, w_out):
    depth = g_pre.shape[0]
    batch, t, _ = x_prompt.shape
    nb_s, t_new, _ = x_sample.shape
    cache_kt = jnp.transpose(cache_k, (0, 1, 3, 4, 2))
    cache_vt = jnp.transpose(cache_v, (0, 1, 3, 4, 2))
    yp = x_prompt.reshape(batch * t, D_MODEL)
    ys = x_sample.reshape(nb_s * t_new, D_MODEL)
    kp_l, vp_l, sp_l, ks_l, vs_l, ss_l = [], [], [], [], [], []
    for l in range(depth):
        w = _layer_weights(g_pre[l], w_in[l], g_q[l], g_k[l], w_a2[l], b_a[l], g_gla[l], w_out[l])
        qt, kt, vt, krow, kmean, sza, szb, qb, kb, vb, lg = _proj(yp, w, batch, sample=False)
        oa = _moba_prompt(qt, krow, vt, kmean, w)
        r3 = lambda a: a.reshape(batch, t, a.shape[-1])
        s0 = jnp.zeros((batch, B_HEADS, B_KEY_DIM, B_VAL_DIM), F32)
        ob, s_fin = _gla(r3(qb), r3(kb), r3(lg), r3(vb), s0, chunk=64)
        yp = _merge(yp, oa.reshape(batch * t, A_WIDTH), sza, ob.reshape(batch * t, 512), szb, w)
        kp_l.append(jnp.transpose(kt.reshape(batch, A_HEADS, A_HEAD_DIM, t), (0, 3, 1, 2)))
        vp_l.append(jnp.transpose(vt.reshape(batch, A_HEADS, A_HEAD_DIM, t), (0, 3, 1, 2)))
        sp_l.append(s_fin)
        qs, ks, vs, sza, szb, qb, kb, vb, lg = _proj(ys, w, nb_s, sample=True)
        h4 = lambda a: jnp.transpose(a.reshape(nb_s, t_new, A_HEADS, A_HEAD_DIM), (0, 2, 1, 3))
        slope_rows = jnp.broadcast_to(jnp.repeat(w["slopes"], t_new)[:, None], (A_HEADS * t_new, LANE))
        oa4 = _moba_sample(h4(qs), h4(ks), h4(vs), cache_kt[l:l + 1], cache_vt[l:l + 1],
                           page_table, slope_rows)
        oa = jnp.transpose(oa4, (0, 2, 1, 3)).reshape(nb_s * t_new, A_WIDTH)
        r3 = lambda a: a.reshape(nb_s, t_new, a.shape[-1])
        ob, s_new = _gla(r3(qb), r3(kb), r3(lg), r3(vb), state_gla[l], chunk=t_new)
        ys = _merge(ys, oa, sza, ob.reshape(nb_s * t_new, 512), szb, w)
        ks_l.append(ks.reshape(nb_s, t_new, A_HEADS, A_HEAD_DIM))
        vs_l.append(vs.reshape(nb_s, t_new, A_HEADS, A_HEAD_DIM))
        ss_l.append(s_new)
    return (yp.reshape(batch, t, D_MODEL), ys.reshape(nb_s, t_new, D_MODEL),
            jnp.stack(kp_l), jnp.stack(vp_l), jnp.stack(sp_l),
            jnp.stack(ks_l), jnp.stack(vs_l), jnp.stack(ss_l))
```

```python
import functools

import jax
import jax.numpy as jnp
import numpy as np
from jax import lax
from jax.experimental import pallas as pl
from jax.experimental.pallas import tpu as pltpu

F32 = jnp.float32
BF16 = jnp.bfloat16

D_MODEL = 1024
A_HEADS = 8
A_HEAD_DIM = 64
A_WIDTH = A_HEADS * A_HEAD_DIM
MOBA_BLOCK = 256
MOBA_TOPK = 3
B_HEADS = 4
B_KEY_DIM = 64
B_VAL_DIM = 128
B_KEY_WIDTH = B_HEADS * B_KEY_DIM
B_VAL_WIDTH = B_HEADS * B_VAL_DIM
GATE_RANK = 16
GATE_TAU = 16.0
PAGE_SIZE = 128
EPS = 1e-6
NEG = -1e30

LANE = 128
VMEM_LIMIT = 56 * 1024 * 1024

_OFF = np.cumsum([0, A_WIDTH, A_WIDTH, A_WIDTH, A_WIDTH, B_KEY_WIDTH, B_KEY_WIDTH,
                  B_VAL_WIDTH, B_VAL_WIDTH, GATE_RANK]).tolist()
_ROW_W = 3 * 512 + 2 * B_KEY_WIDTH + LANE
_AUG0 = 128
_MASK0 = 136
_LOG2E = 1.4426950408889634
_PROJ_ROWS = 512


def _dot(a, b, dims=(((1,), (0,)), ((), ())), precision=None):
    return lax.dot_general(a, b, dims, precision=precision, preferred_element_type=F32)


_NT = (((1,), (1,)), ((), ()))
_TN = (((0,), (0,)), ((), ()))


def _silu(x):
    return x / (1.0 + jnp.exp(-x))


def _log_sigmoid(x):
    return jnp.minimum(x, 0.0) - jnp.log1p(jnp.exp(-jnp.abs(x)))


def _proj_kernel(x_ref, gpre_ref, wt_ref, wrow_ref, wa2_ref, ba_ref, gq_ref, gk_ref, *out_refs,
                 tm, sample):
    x = x_ref[...]
    ms = jnp.mean(x * x, axis=-1, keepdims=True)
    h = ((x * lax.rsqrt(ms + EPS)) * gpre_ref[...]).astype(BF16)

    pt = _dot(wt_ref[...], h, _NT)

    def head_norm(t, g):
        t3 = t.reshape(A_HEADS, A_HEAD_DIM, tm)
        ss = jnp.mean(t3 * t3, axis=1, keepdims=True)
        return (t3 * lax.rsqrt(ss + EPS)).reshape(A_WIDTH, tm) * g

    q_t = head_norm(pt[0:A_WIDTH], gq_ref[...]) * (A_HEAD_DIM ** -0.5)
    k_t = head_norm(pt[A_WIDTH:2 * A_WIDTH], gk_ref[...])
    v_t = pt[2 * A_WIDTH:3 * A_WIDTH]

    def seg(i0, i1):
        return _dot(h, wrow_ref[:, i0:i1])

    sza = _silu(seg(0, 512)).astype(BF16)
    szb = _silu(seg(512, 1024)).astype(BF16)
    qk = seg(1024, 1536)
    qb = qk[:, 0:B_KEY_WIDTH] * (B_KEY_DIM ** -0.5)
    kb = qk[:, B_KEY_WIDTH:2 * B_KEY_WIDTH]
    vb = seg(1536, 2048)
    ab = seg(2048, 2176).astype(BF16)
    pre = _dot(ab, wa2_ref[...]) + ba_ref[...]
    lg = _log_sigmoid(pre) * (_LOG2E / GATE_TAU)

    if sample:
        (q_ref, k_ref, v_ref, sza_ref, szb_ref, qb_ref, kb_ref, vb_ref, lg_ref) = out_refs
        q_ref[...] = q_t.T
        k_ref[...] = k_t.T
        v_ref[...] = v_t.T
    else:
        (qt_ref, kt_ref, vt_ref, krow_ref, kmean_ref,
         sza_ref, szb_ref, qb_ref, kb_ref, vb_ref, lg_ref) = out_refs
        qt_ref[0] = q_t
        kt_ref[0] = k_t
        vt_ref[0] = v_t
        k_row = k_t.T
        krow_ref[0] = k_row.astype(BF16)
        kmean_ref[0] = jnp.mean(k_row.reshape(tm // MOBA_BLOCK, MOBA_BLOCK, A_WIDTH), axis=1)
    sza_ref[...] = sza
    szb_ref[...] = szb
    qb_ref[...] = qb
    kb_ref[...] = kb
    vb_ref[...] = vb
    lg_ref[...] = lg


def _proj(x2d, w, batch, sample):
    n = x2d.shape[0]
    tm = min(_PROJ_ROWS, n)
    nt = n // tm
    bpt = tm // MOBA_BLOCK
    const = lambda i: (0, 0)
    row = lambda i: (i, 0)
    in_specs = [
        pl.BlockSpec((tm, D_MODEL), row),
        pl.BlockSpec((1, D_MODEL), const),
        pl.BlockSpec((3 * A_WIDTH, D_MODEL), const),
        pl.BlockSpec((D_MODEL, _ROW_W), const),
        pl.BlockSpec((LANE, B_KEY_WIDTH), const),
        pl.BlockSpec((1, B_KEY_WIDTH), const),
        pl.BlockSpec((A_WIDTH, tm), const),
        pl.BlockSpec((A_WIDTH, tm), const),
    ]
    row_specs = [pl.BlockSpec((tm, wd), row) for wd in (512, 512, B_KEY_WIDTH, B_KEY_WIDTH, 512, B_KEY_WIDTH)]
    row_shapes = [jax.ShapeDtypeStruct((n, wd), dt) for wd, dt in
                  ((512, BF16), (512, BF16), (B_KEY_WIDTH, F32), (B_KEY_WIDTH, F32), (512, F32),
                   (B_KEY_WIDTH, F32))]
    if sample:
        out_specs = [pl.BlockSpec((tm, 512), row)] * 3 + row_specs
        out_shape = [jax.ShapeDtypeStruct((n, 512), F32)] * 3 + row_shapes
    else:
        t = n // batch
        tpb = t // tm
        feat = lambda i: (i // tpb, 0, i % tpb)
        out_specs = ([pl.BlockSpec((1, A_WIDTH, tm), feat)] * 3
                     + [pl.BlockSpec((1, tm, 512), lambda i: (i // tpb, i % tpb, 0)),
                        pl.BlockSpec((1, bpt, 512), lambda i: (i, 0, 0))]
                     + row_specs)
        out_shape = ([jax.ShapeDtypeStruct((batch, A_WIDTH, t), F32)] * 3
                     + [jax.ShapeDtypeStruct((batch, t, 512), BF16),
                        jax.ShapeDtypeStruct((nt, bpt, 512), F32)]
                     + row_shapes)
    return pl.pallas_call(
        functools.partial(_proj_kernel, tm=tm, sample=sample),
        grid=(nt,),
        in_specs=in_specs,
        out_specs=out_specs,
        out_shape=out_shape,
        compiler_params=pltpu.CompilerParams(dimension_semantics=("arbitrary",),
                                             vmem_limit_bytes=VMEM_LIMIT),
        name="proj_sample" if sample else "proj_prompt",
    )(x2d, w["g_pre"], w["w_t"], w["w_row"], w["w_a2p"], w["b_ap"], w["gq_t"], w["gk_t"])


def _top3_rows(g, n):
    idx = lax.broadcasted_iota(jnp.int32, g.shape, 0)
    sel = jnp.zeros(g.shape, F32)
    for _ in range(MOBA_TOPK):
        m = jnp.max(g, axis=0, keepdims=True)
        first = jnp.min(jnp.where(g == m, idx, n), axis=0, keepdims=True)
        pick = idx == first
        sel = jnp.where(pick, 1.0, sel)
        g = jnp.where(pick, -jnp.inf, g)
    return sel


_QB = 8


def _moba_prompt_kernel(qt_ref, krow_ref, vt_ref, kmean_ref, qaug_ref, kaug_ref, o_ref,
                        qop_ref, acc_ref, m_ref, s_ref, cm_ref, *, nb):
    blk = MOBA_BLOCK
    qw = _QB * blk
    i0 = pl.program_id(2) * _QB

    qt = qt_ref[0]
    row128 = lax.broadcasted_iota(jnp.int32, (2 * A_HEAD_DIM, qw), 0)
    lane128 = lax.broadcasted_iota(jnp.int32, (nb, LANE), 1)
    bidx = lax.broadcasted_iota(jnp.int32, (nb, qw), 0)
    iq = i0 + lax.broadcasted_iota(jnp.int32, (nb, qw), 1) // blk
    kmean = kmean_ref[0]
    qt_hi = qt.astype(BF16)
    qt_lo = (qt - qt_hi.astype(F32)).astype(BF16)

    for hh in range(2):
        in_head = (row128 >= hh * A_HEAD_DIM) & (row128 < (hh + 1) * A_HEAD_DIM)
        q_h = jnp.where(in_head, qt * _LOG2E, 0.0)
        km_h = jnp.where((lane128 >= hh * A_HEAD_DIM) & (lane128 < (hh + 1) * A_HEAD_DIM), kmean, 0.0)
        km_hi = km_h.astype(BF16)
        km_lo = (km_h - km_hi.astype(F32)).astype(BF16)
        gate = _dot(km_hi, qt_hi) + _dot(km_hi, qt_lo) + _dot(km_lo, qt_hi)
        past = bidx < iq
        sel = _top3_rows(jnp.where(past, gate, NEG), nb)
        keep = ((sel > 0.5) & past) | (bidx == iq)
        maskbias = jnp.where(keep, 0.0, NEG)
        for qb in range(_QB):
            c = hh * _QB + qb
            cols = slice(qb * blk, (qb + 1) * blk)
            qop = jnp.concatenate(
                [q_h[:, cols], qaug_ref[0, c], maskbias[:, cols],
                 jnp.zeros((2 * LANE - _MASK0 - nb, blk), F32)], axis=0)
            qop_ref[c] = qop.astype(BF16)
            acc_ref[c] = jnp.zeros(acc_ref.shape[1:], F32)
            m_ref[c] = jnp.full(m_ref.shape[1:], -jnp.inf, F32)

    lane_k = lax.broadcasted_iota(jnp.int32, (blk, LANE), 1)
    kaug_base = kaug_ref[...]
    ones_rows = jnp.ones((8, blk), BF16)
    kk = lax.broadcasted_iota(jnp.int32, (blk, blk), 0)
    qq = lax.broadcasted_iota(jnp.int32, (blk, blk), 1)

    def scores(j, slot, chains):
        off = pl.multiple_of(j * blk, blk)
        shift = ((j - i0) * blk).astype(F32)
        kaug = jnp.where((lane_k >= 3) & (lane_k < 6), shift.astype(BF16),
                         jnp.where(lane_k == 8 + j, jnp.ones((), BF16), kaug_base))
        kop = jnp.concatenate([krow_ref[0, pl.ds(off, blk), :], kaug], axis=1)
        for c in chains:
            s = _dot(kop, qop_ref[c])
            s_ref[slot, c] = s
            cm_ref[slot, c] = jnp.max(s.reshape(blk // 8, 8, blk), axis=0)

    def absorb(j, slot, chains):
        off = pl.multiple_of(j * blk, blk)
        vops = []
        for hh in range(2):
            v_t = vt_ref[0, hh * A_HEAD_DIM:(hh + 1) * A_HEAD_DIM, pl.ds(off, blk)].astype(BF16)
            vops.append(jnp.concatenate([v_t, ones_rows], axis=0))
        for c, causal in chains:
            s = s_ref[slot, c]
            if causal:
                s = jnp.where(kk <= qq, s, NEG)
            m_prev = m_ref[c][0:1]
            cmax = s if causal else cm_ref[slot, c]
            m_new = jnp.maximum(m_prev, jnp.max(cmax, axis=0, keepdims=True))
            alpha = jnp.exp2(m_prev - m_new)
            p = jnp.exp2(s - m_new).astype(BF16)
            acc_ref[c] = alpha * acc_ref[c] + _dot(vops[c // _QB], p)
            m_ref[c] = jnp.broadcast_to(m_new, m_ref.shape[1:])

    all_c = list(range(2 * _QB))
    plain = [(c, False) for c in all_c]
    scores(0, 0, all_c)

    def body(jj, carry):
        j = _QB * jj
        for d in range(_QB):
            scores(j + d + 1, (d + 1) % 2, all_c)
            absorb(j + d, d % 2, plain)
        return carry

    lax.fori_loop(0, i0 // _QB, body, 0)
    for d in range(_QB):
        if d + 1 < _QB:
            scores(i0 + d + 1, (d + 1) % 2, [c for c in all_c if c % _QB >= d + 1])
        absorb(i0 + d, d % 2, [(c, c % _QB == d) for c in all_c if c % _QB >= d])

    for qb in range(_QB):
        outs = []
        for hh in range(2):
            acc = acc_ref[hh * _QB + qb]
            outs.append(acc[0:A_HEAD_DIM] / acc[A_HEAD_DIM:A_HEAD_DIM + 1])
        o_ref[0, qb * blk:(qb + 1) * blk, :] = jnp.concatenate(outs, axis=0).T.astype(o_ref.dtype)


def _moba_prompt(qt, krow, vt, kmean, w):
    batch, _, t = qt.shape
    nb = t // MOBA_BLOCK
    blk = MOBA_BLOCK
    qw = _QB * blk
    return pl.pallas_call(
        functools.partial(_moba_prompt_kernel, nb=nb),
        grid=(batch, A_HEADS // 2, nb // _QB),
        in_specs=[
            pl.BlockSpec((1, 2 * A_HEAD_DIM, qw), lambda b, hp, i: (b, hp, i)),
            pl.BlockSpec((1, t, LANE), lambda b, hp, i: (b, 0, hp)),
            pl.BlockSpec((1, 2 * A_HEAD_DIM, t), lambda b, hp, i: (b, hp, 0)),
            pl.BlockSpec((1, nb, LANE), lambda b, hp, i: (b, 0, hp)),
            pl.BlockSpec((1, 2 * _QB, 8, blk), lambda b, hp, i: (hp, 0, 0, 0)),
            pl.BlockSpec((blk, LANE), lambda b, hp, i: (0, 0)),
        ],
        out_specs=pl.BlockSpec((1, qw, LANE), lambda b, hp, i: (b, i, hp)),
        out_shape=jax.ShapeDtypeStruct((batch, t, A_WIDTH), BF16),
        scratch_shapes=[
            pltpu.VMEM((2 * _QB, 2 * LANE, blk), BF16),
            pltpu.VMEM((2 * _QB, A_HEAD_DIM + 8, blk), F32),
            pltpu.VMEM((2 * _QB, 8, blk), F32),
            pltpu.VMEM((2, 2 * _QB, blk, blk), F32),
            pltpu.VMEM((2, 2 * _QB, 8, blk), F32),
        ],
        compiler_params=pltpu.CompilerParams(
            dimension_semantics=("arbitrary", "arbitrary", "arbitrary"),
            vmem_limit_bytes=VMEM_LIMIT),
        name="moba_prompt",
    )(qt, krow, vt, kmean.reshape(batch, nb, A_WIDTH), w["qaug"], w["kaug"])


def _moba_sample_kernel(pt_ref, q_ref, qn_ref, kn_ref, vn_ref, slope_ref, ck_hbm, cv_hbm, o_ref,
                        kbuf, sall, gate_ref, idx_v, idx_s, vbuf, sem_k, sem_v, sem_i,
                        *, n_pages, npg, ring):
    b = pl.program_id(0)
    nbatch = pl.num_programs(0)
    nxt = jnp.minimum(b + 1, nbatch - 1)
    n_chunks = n_pages // npg
    cw = npg * PAGE_SIZE
    bpc = cw // MOBA_BLOCK
    past = n_pages * PAGE_SIZE
    n_blocks = past // MOBA_BLOCK
    t_new = q_ref.shape[2]
    rows = A_HEADS * t_new
    ppb = MOBA_BLOCK // PAGE_SIZE
    slot_b = b % 2

    def k_copy(s, c, p):
        page = pt_ref[s, c * npg + p]
        return pltpu.make_async_copy(
            ck_hbm.at[0, page], kbuf.at[c % ring, p],
            sem_k.at[c % ring])

    def start_chunk(s, c):
        for p in range(npg):
            k_copy(s, c, p).start()

    def start_ring(s):
        for c in range(ring):
            start_chunk(s, c)

    lane_g = lax.broadcasted_iota(jnp.int32, (t_new, LANE), 1)

    def split_q(qr):
        ops = []
        for h in range(A_HEADS):
            qh = qr[0, h]
            hi = qh.astype(BF16)
            ops.append(jnp.concatenate([hi, (qh - hi.astype(F32)).astype(BF16)], axis=0))
        return ops

    def sweep_chunk(s, q_ops, slot, c):
        for p in range(npg):
            k_copy(s, c, p).wait()
        for h in range(A_HEADS):
            kt = jnp.concatenate([kbuf[c % ring, p, h] for p in range(npg)], axis=1)
            s2 = _dot(q_ops[h], kt.astype(BF16))
            sc = s2[0:t_new] + s2[t_new:2 * t_new]
            sall[slot, h * t_new:(h + 1) * t_new, c * cw:(c + 1) * cw] = sc
            g = gate_ref[h * t_new:(h + 1) * t_new, :]
            for jb in range(bpc):
                t2 = (sc[:, jb * MOBA_BLOCK:jb * MOBA_BLOCK + LANE]
                      + sc[:, jb * MOBA_BLOCK + LANE:(jb + 1) * MOBA_BLOCK])
                g = jnp.where(lane_g == c * bpc + jb, jnp.sum(t2, axis=1, keepdims=True), g)
            gate_ref[h * t_new:(h + 1) * t_new, :] = g
        if c + ring < n_chunks:
            start_chunk(s, c + ring)

    picks = [(t, r) for t in range(t_new) for r in range(MOBA_TOPK)]

    def v_copy(s, h, t, r, pg):
        bsel = idx_s[h * t_new + t, r]
        page = pt_ref[s, ppb * bsel + pg]
        return pltpu.make_async_copy(
            cv_hbm.at[0, page, h], vbuf.at[h, (t * MOBA_TOPK + r) * ppb + pg], sem_v.at[h])

    def v_start(s, h):
        for t, r in picks:
            for pg in range(ppb):
                v_copy(s, h, t, r, pg).start(priority=pg % 2)

    ahead = 2

    def select_blocks(s):
        lane_r = lax.broadcasted_iota(jnp.int32, (rows, LANE), 1)
        g = jnp.where(lane_r < n_blocks, gate_ref[...], -jnp.inf)
        idx = jnp.zeros((rows, LANE), jnp.int32)
        for r in range(MOBA_TOPK):
            m = jnp.max(g, axis=1, keepdims=True)
            first = jnp.min(jnp.where(g == m, lane_r, LANE), axis=1, keepdims=True)
            g = jnp.where(lane_r == first, -jnp.inf, g)
            idx = jnp.where(lane_r == r, first, idx)
        idx_v[...] = idx
        cp_i = pltpu.make_async_copy(idx_v, idx_s, sem_i)
        cp_i.start()
        cp_i.wait()
        for h0 in range(ahead):
            v_start(s, h0)

    @pl.when(b == 0)
    def _():
        start_ring(0)
        gate_ref[...] = jnp.zeros(gate_ref.shape, F32)
        q0 = split_q(q_ref)
        for c in range(n_chunks):
            sweep_chunk(0, q0, 0, c)
        start_ring(nxt)
        select_blocks(0)

    sub = lax.broadcasted_iota(jnp.int32, (t_new, MOBA_BLOCK), 0)
    klane = lax.broadcasted_iota(jnp.int32, (t_new, MOBA_BLOCK), 1)
    trow = lax.broadcasted_iota(jnp.int32, (t_new, t_new), 0)
    tcol = lax.broadcasted_iota(jnp.int32, (t_new, t_new), 1)

    def finish_head(h):
        if h + ahead < A_HEADS:
            v_start(b, h + ahead)
        r0 = h * t_new
        slope = slope_ref[r0:r0 + t_new, :][:, 0:1]
        pieces = []
        for t, r in picks:
            bsel = idx_s[r0 + t, r]
            off = pl.multiple_of(bsel * MOBA_BLOCK, MOBA_BLOCK)
            dist = ((past + t - bsel * MOBA_BLOCK) - klane).astype(F32)
            sc = sall[slot_b, r0:r0 + t_new, pl.ds(off, MOBA_BLOCK)] - slope * dist
            pieces.append(jnp.where(sub == t, sc, NEG))
        s_sel = jnp.concatenate(pieces, axis=1)
        s_new = _dot(q_ref[0, h], kn_ref[0, h], _NT) - slope * (trow - tcol).astype(F32)
        s_new = jnp.where(tcol <= trow, s_new, NEG)
        m = jnp.maximum(jnp.max(s_sel, axis=1, keepdims=True), jnp.max(s_new, axis=1, keepdims=True))
        p_sel = jnp.exp(s_sel - m)
        p_new = jnp.exp(s_new - m)
        l = jnp.sum(p_sel, axis=1, keepdims=True) + jnp.sum(p_new, axis=1, keepdims=True)
        for t, r in picks:
            for pg in range(ppb):
                v_copy(b, h, t, r, pg).wait()
        v_sel = jnp.concatenate([vbuf[h, j] for j in range(len(picks) * ppb)], axis=1)
        acc = _dot(p_sel.astype(BF16), v_sel.astype(BF16), _NT) + _dot(p_new, vn_ref[0, h])
        o_ref[0, h] = acc / l

    gate_ref[...] = jnp.zeros(gate_ref.shape, F32)
    qn = split_q(qn_ref)
    for i in range(max(n_chunks, A_HEADS)):
        if i < n_chunks:
            sweep_chunk(nxt, qn, 1 - slot_b, i)
        if i < A_HEADS:
            finish_head(i)

    @pl.when(b + 1 < nbatch)
    def _():
        start_ring(jnp.minimum(b + 2, nbatch - 1))
        select_blocks(nxt)


def _moba_sample(q4, kn4, vn4, cache_kt, cache_vt, page_table, slope_rows):
    nbatch, _, t_new, _ = q4.shape
    n_pages = page_table.shape[1]
    npg = 16
    ring = min(4, n_pages // npg)
    rows = A_HEADS * t_new
    past = n_pages * PAGE_SIZE
    shape4 = (1, A_HEADS, t_new, A_HEAD_DIM)
    blk4 = pl.BlockSpec(shape4, lambda b, pt: (b, 0, 0, 0))
    nxt4 = pl.BlockSpec(shape4, lambda b, pt: (jnp.minimum(b + 1, nbatch - 1), 0, 0, 0))
    return pl.pallas_call(
        functools.partial(_moba_sample_kernel, n_pages=n_pages, npg=npg, ring=ring),
        grid_spec=pltpu.PrefetchScalarGridSpec(
            num_scalar_prefetch=1,
            grid=(nbatch,),
            in_specs=[blk4, nxt4, blk4, blk4,
                      pl.BlockSpec((rows, LANE), lambda b, pt: (0, 0)),
                      pl.BlockSpec(memory_space=pl.ANY),
                      pl.BlockSpec(memory_space=pl.ANY)],
            out_specs=blk4,
            scratch_shapes=[
                pltpu.VMEM((ring, npg, A_HEADS, A_HEAD_DIM, PAGE_SIZE), F32),
                pltpu.VMEM((2, rows, past), F32),
                pltpu.VMEM((rows, LANE), F32),
                pltpu.VMEM((rows, LANE), jnp.int32),
                pltpu.SMEM((rows, LANE), jnp.int32),
                pltpu.VMEM((A_HEADS, t_new * MOBA_TOPK * (MOBA_BLOCK // PAGE_SIZE), A_HEAD_DIM, PAGE_SIZE), F32),
                pltpu.SemaphoreType.DMA((ring,)),
                pltpu.SemaphoreType.DMA((A_HEADS,)),
                pltpu.SemaphoreType.DMA,
            ]),
        out_shape=jax.ShapeDtypeStruct((nbatch, A_HEADS, t_new, A_HEAD_DIM), F32),
        compiler_params=pltpu.CompilerParams(dimension_semantics=("arbitrary",),
                                             vmem_limit_bytes=VMEM_LIMIT),
        name="moba_sample",
    )(page_table, q4, q4, kn4, vn4, slope_rows, cache_kt, cache_vt)


def _gla_kernel(q_ref, k_ref, g_ref, v_ref, s0_ref, o_ref, sfin_ref, state, *, chunk, n_chunks, nbb):
    c = chunk
    mx = BF16 if c >= 16 else F32
    hpl = LANE // B_KEY_DIM
    zpad = jnp.zeros((B_KEY_DIM, B_VAL_DIM), F32)

    @pl.when(pl.program_id(2) == 0)
    def _():
        for bi in range(nbb):
            for hh in range(hpl):
                parts = [s0_ref[bi, hh] if j == hh else zpad for j in range(hpl)]
                state[bi, hh] = jnp.concatenate(parts, axis=0).T

    ri = lax.broadcasted_iota(jnp.int32, (c, LANE), 0)
    lane_c = lax.broadcasted_iota(jnp.int32, (c, LANE), 1)
    head_lanes = [(lane_c // B_KEY_DIM) == hh for hh in range(hpl)]
    ti = lax.broadcasted_iota(jnp.int32, (c, c), 0)
    si = lax.broadcasted_iota(jnp.int32, (c, c), 1)
    tril = jnp.where(si <= ti, 1.0, 0.0).astype(mx)
    sub3 = lax.broadcasted_iota(jnp.int32, (c // 8, 8, LANE), 1)
    levels = []
    half = c // 2
    while half >= 1:
        width = 2 * half
        sb = int(np.log2(width))
        upper = (ri & (width - 1)) >= half
        pair = ((ti >> sb) == (si >> sb)) & ((ti & (width - 1)) >= half) & ((si & (width - 1)) < half)
        levels.append((half, width, upper, pair))
        half //= 2

    seq = [(bi, ci) for bi in range(nbb) for ci in range(n_chunks)]
    loaded = []
    for bi, ci in seq:
        g = g_ref[bi, pl.ds(ci * c, c), :]
        g_hi = g.astype(BF16)
        g_lo = (g - g_hi.astype(F32)).astype(BF16)
        loaded.append((g, _dot(tril, g_hi.astype(mx)) + _dot(tril, g_lo.astype(mx))))

    prepared = []
    for (bi, ci), (g, b) in zip(seq, loaded):
        rows = pl.ds(ci * c, c)
        q = q_ref[bi, rows, :]
        k = k_ref[bi, rows, :]
        v32 = v_ref[bi, rows, :]
        b_last = b[c - 1:c, :]
        b3 = b.reshape(c // 8, 8, LANE)
        zs = []
        for half, width, upper, _ in levels:
            if half == 1:
                x = jnp.where(upper, g, 0.0)
            else:
                if half >= 8:
                    mid = jnp.concatenate(
                        [jnp.broadcast_to(b[m * width + half - 1:m * width + half, :], (width, LANE))
                         for m in range(c // width)], axis=0)
                else:
                    mid3 = jnp.broadcast_to(b3[:, half - 1:half, :], b3.shape)
                    for m in range(1, 8 // width):
                        r = m * width + half - 1
                        mid3 = jnp.where(sub3 >= m * width,
                                         jnp.broadcast_to(b3[:, r:r + 1, :], b3.shape), mid3)
                    mid = mid3.reshape(c, LANE)
                x = jnp.where(upper, b - mid, mid - b)
            zs.append(jnp.where(upper, q, k) * jnp.exp2(x))
        qd = q * jnp.exp2(b)
        kd = k * jnp.exp2(b_last - b)
        qk = q * k
        heads = []
        for hh in range(hpl):
            own = head_lanes[hh]
            cut = lambda a: jnp.where(own, a, 0.0).astype(mx)
            v_h = v32[:, hh * B_VAL_DIM:(hh + 1) * B_VAL_DIM]
            o_same = jnp.sum(jnp.where(own, qk, 0.0), axis=1, keepdims=True) * v_h
            heads.append((v_h.astype(mx), [cut(z) for z in zs], cut(qd), cut(kd), o_same))
        prepared.append((heads, jnp.exp2(b_last)))

    grams, updates = [], []
    for heads, _ in prepared:
        grams.append([[_dot(z, z, _NT) for z in zh] for _, zh, _, _, _ in heads])
        updates.append([_dot(v, kd, _TN) for v, _, _, kd, _ in heads])

    attns = []
    for gr in grams:
        per_head = []
        for gh in gr:
            attn = jnp.where(levels[0][3], gh[0], 0.0)
            for (_, _, _, pair), gm in zip(levels[1:], gh[1:]):
                attn = attn + jnp.where(pair, gm, 0.0)
            per_head.append(attn.astype(mx))
        attns.append(per_head)

    states = []
    for n, (bi, ci) in enumerate(seq):
        a_last = prepared[n][1]
        cur = []
        for hh in range(hpl):
            st = state[bi, hh] if ci == 0 else states[-1][hh][1]
            cur.append((st, a_last * st + updates[n][hh]))
            if ci == n_chunks - 1:
                state[bi, hh] = cur[-1][1]
        states.append(cur)

    for n, (bi, ci) in enumerate(seq):
        outs = []
        for hh, (v, _, qd, _, o_same) in enumerate(prepared[n][0]):
            outs.append(o_same + _dot(attns[n][hh], v) + _dot(qd, states[n][hh][0].astype(mx), _NT))
        o_ref[bi, pl.ds(ci * c, c), :] = jnp.concatenate(outs, axis=1).astype(o_ref.dtype)

    @pl.when(pl.program_id(2) == pl.num_programs(2) - 1)
    def _():
        for bi in range(nbb):
            for hh in range(hpl):
                sfin_ref[bi, hh] = state[bi, hh].T[hh * B_KEY_DIM:(hh + 1) * B_KEY_DIM, :]


def _gla(qb, kb, lg, vb, s0, chunk):
    batch, t, _ = qb.shape
    tc = min(t, 1024)
    n_chunks = tc // chunk
    nbb = 8 if (t == tc and batch % 8 == 0) else 1
    hpl = LANE // B_KEY_DIM
    seq = pl.BlockSpec((nbb, tc, LANE), lambda b, h, c: (b, c, h))
    seq_v = pl.BlockSpec((nbb, tc, hpl * B_VAL_DIM), lambda b, h, c: (b, c, h))
    st = pl.BlockSpec((nbb, hpl, B_KEY_DIM, B_VAL_DIM), lambda b, h, c: (b, h, 0, 0))
    return pl.pallas_call(
        functools.partial(_gla_kernel, chunk=chunk, n_chunks=n_chunks, nbb=nbb),
        grid=(batch // nbb, B_HEADS // hpl, t // tc),
        in_specs=[seq, seq, seq, seq_v, st],
        out_specs=[seq_v, st],
        out_shape=[jax.ShapeDtypeStruct((batch, t, B_VAL_WIDTH), BF16 if chunk >= 16 else F32),
                   jax.ShapeDtypeStruct((batch, B_HEADS, B_KEY_DIM, B_VAL_DIM), F32)],
        scratch_shapes=[pltpu.VMEM((nbb, hpl, B_VAL_DIM, LANE), F32)],
        compiler_params=pltpu.CompilerParams(
            dimension_semantics=("arbitrary", "arbitrary", "arbitrary"),
            vmem_limit_bytes=VMEM_LIMIT),
        name="gla_c%d" % chunk,
    )(qb, kb, lg, vb, s0)


def _merge_kernel(x_ref, oa_ref, sza_ref, ob_ref, szb_ref, gg_ref, wo_ref, y_ref):
    ya = oa_ref[...].astype(F32) * sza_ref[...].astype(F32)
    ob = ob_ref[...].astype(F32)
    parts = []
    for h in range(B_HEADS):
        oh = ob[:, h * B_VAL_DIM:(h + 1) * B_VAL_DIM]
        ms = jnp.mean(oh * oh, axis=-1, keepdims=True)
        parts.append(oh * lax.rsqrt(ms + EPS))
    yb = (jnp.concatenate(parts, axis=1) * gg_ref[...]) * szb_ref[...].astype(F32)
    cat = jnp.concatenate([ya, yb], axis=1).astype(BF16)
    y_ref[...] = x_ref[...] + _dot(cat, wo_ref[...])


def _merge(x2d, oa, sza, ob, szb, w):
    n = x2d.shape[0]
    tm = min(n, 1024)
    row = lambda i: (i, 0)
    const = lambda i: (0, 0)
    return pl.pallas_call(
        _merge_kernel,
        grid=(n // tm,),
        in_specs=[pl.BlockSpec((tm, D_MODEL), row), pl.BlockSpec((tm, 512), row),
                  pl.BlockSpec((tm, 512), row), pl.BlockSpec((tm, 512), row),
                  pl.BlockSpec((tm, 512), row), pl.BlockSpec((1, 512), const),
                  pl.BlockSpec((D_MODEL, D_MODEL), const)],
        out_specs=pl.BlockSpec((tm, D_MODEL), row),
        out_shape=jax.ShapeDtypeStruct((n, D_MODEL), F32),
        compiler_params=pltpu.CompilerParams(dimension_semantics=("arbitrary",),
                                             vmem_limit_bytes=VMEM_LIMIT),
        name="merge_out",
    )(x2d, oa, sza, ob, szb, w["g_gla"], w["w_out"])


def _layer_weights(g_pre, w_in, g_q, g_k, w_a2, b_a, g_gla, w_out):
    o = _OFF
    w_t = w_in[:, o[0]:o[3]].T.astype(BF16)
    w_row = jnp.concatenate([
        w_in[:, o[3]:o[4]], w_in[:, o[7]:o[8]], w_in[:, o[4]:o[5]], w_in[:, o[5]:o[6]],
        w_in[:, o[6]:o[7]],
        jnp.pad(w_in[:, o[8]:o[9]], ((0, 0), (0, LANE - GATE_RANK)))], axis=1).astype(BF16)
    w_a2p = jnp.pad(w_a2, ((0, LANE - GATE_RANK), (0, 0))).astype(BF16)
    b_ap = b_a.reshape(1, B_KEY_WIDTH)
    gq_t = jnp.broadcast_to(jnp.tile(g_q, A_HEADS)[:, None], (A_WIDTH, _PROJ_ROWS))
    gk_t = jnp.broadcast_to(jnp.tile(g_k, A_HEADS)[:, None], (A_WIDTH, _PROJ_ROWS))
    slopes = jnp.asarray([2.0 ** (-8.0 * (h + 1) / A_HEADS) for h in range(A_HEADS)], F32)
    c = slopes * _LOG2E
    pieces = []
    rem = c
    for _ in range(3):
        pc = rem.astype(BF16).astype(F32)
        pieces.append(pc)
        rem = rem - pc
    qq = jnp.arange(MOBA_BLOCK, dtype=F32)
    qaug = jnp.zeros((A_HEADS, _QB, 8, MOBA_BLOCK), F32)
    for p_i, pc in enumerate(pieces):
        qaug = qaug.at[:, :, p_i, :].set(pc[:, None, None])
        qaug = qaug.at[:, :, 3 + p_i, :].set(pc[:, None, None])
    qpos = qq[None, None, :] + (MOBA_BLOCK * jnp.arange(_QB, dtype=F32))[None, :, None]
    qaug = qaug.at[:, :, 6, :].set(-c[:, None, None] * qpos)
    qaug = qaug.reshape(A_HEADS // 2, 2 * _QB, 8, MOBA_BLOCK)
    kaug = jnp.zeros((MOBA_BLOCK, LANE), F32)
    kaug = kaug.at[:, 0:3].set(qq[:, None]).at[:, 6].set(1.0).astype(BF16)
    return {
        "g_pre": g_pre.reshape(1, D_MODEL), "w_t": w_t, "w_row": w_row, "w_a2p": w_a2p, "b_ap": b_ap,
        "gq_t": gq_t, "gk_t": gk_t, "g_gla": g_gla.reshape(1, B_VAL_WIDTH),
        "w_out": w_out.astype(BF16), "qaug": qaug, "kaug": kaug, "slopes": slopes,
    }


def kernel(x_prompt, x_sample, cache_k, cache_v, state_gla, page_table, g_pre, w_in, g_q, g_k, w_a2, b_a, g_gla, w_out):
    depth = g_pre.shape[0]
    batch, t, _ = x_prompt.shape
    nb_s, t_new, _ = x_sample.shape
    cache_kt = jnp.transpose(cache_k, (0, 1, 3, 4, 2))
    cache_vt = jnp.transpose(cache_v, (0, 1, 3, 4, 2))
    yp = x_prompt.reshape(batch * t, D_MODEL)
    ys = x_sample.reshape(nb_s * t_new, D_MODEL)
    kp_l, vp_l, sp_l, ks_l, vs_l, ss_l = [], [], [], [], [], []
    for l in range(depth):
        w = _layer_weights(g_pre[l], w_in[l], g_q[l], g_k[l], w_a2[l], b_a[l], g_gla[l], w_out[l])
        qt, kt, vt, krow, kmean, sza, szb, qb, kb, vb, lg = _proj(yp, w, batch, sample=False)
        oa = _moba_prompt(qt, krow, vt, kmean, w)
        r3 = lambda a: a.reshape(batch, t, a.shape[-1])
        s0 = jnp.zeros((batch, B_HEADS, B_KEY_DIM, B_VAL_DIM), F32)
        ob, s_fin = _gla(r3(qb), r3(kb), r3(lg), r3(vb), s0, chunk=64)
        yp = _merge(yp, oa.reshape(batch * t, A_WIDTH), sza, ob.reshape(batch * t, 512), szb, w)
        kp_l.append(jnp.transpose(kt.reshape(batch, A_HEADS, A_HEAD_DIM, t), (0, 3, 1, 2)))
        vp_l.append(jnp.transpose(vt.reshape(batch, A_HEADS, A_HEAD_DIM, t), (0, 3, 1, 2)))
        sp_l.append(s_fin)
        qs, ks, vs, sza, szb, qb, kb, vb, lg = _proj(ys, w, nb_s, sample=True)
        h4 = lambda a: jnp.transpose(a.reshape(nb_s, t_new, A_HEADS, A_HEAD_DIM), (0, 2, 1, 3))
        slope_rows = jnp.broadcast_to(jnp.repeat(w["slopes"], t_new)[:, None], (A_HEADS * t_new, LANE))
        oa4 = _moba_sample(h4(qs), h4(ks), h4(vs), cache_kt[l:l + 1], cache_vt[l:l + 1],
                           page_table, slope_rows)
        oa = jnp.transpose(oa4, (0, 2, 1, 3)).reshape(nb_s * t_new, A_WIDTH)
        r3 = lambda a: a.reshape(nb_s, t_new, a.shape[-1])
        ob, s_new = _gla(r3(qb), r3(kb), r3(lg), r3(vb), state_gla[l], chunk=t_new)
        ys = _merge(ys, oa, sza, ob.reshape(nb_s * t_new, 512), szb, w)
        ks_l.append(ks.reshape(nb_s, t_new, A_HEADS, A_HEAD_DIM))
        vs_l.append(vs.reshape(nb_s, t_new, A_HEADS, A_HEAD_DIM))
        ss_l.append(s_new)
    return (yp.reshape(batch, t, D_MODEL), ys.reshape(nb_s, t_new, D_MODEL),
            jnp.stack(kp_l), jnp.stack(vp_l), jnp.stack(sp_l),
            jnp.stack(ks_l), jnp.stack(vs_l), jnp.stack(ss_l))
```

```python
import functools

import jax
import jax.numpy as jnp
import numpy as np
from jax import lax
from jax.experimental import pallas as pl
from jax.experimental.pallas import tpu as pltpu

F32 = jnp.float32
BF16 = jnp.bfloat16

D_MODEL = 1024
A_HEADS = 8
A_HEAD_DIM = 64
A_WIDTH = A_HEADS * A_HEAD_DIM
MOBA_BLOCK = 256
MOBA_TOPK = 3
B_HEADS = 4
B_KEY_DIM = 64
B_VAL_DIM = 128
B_KEY_WIDTH = B_HEADS * B_KEY_DIM
B_VAL_WIDTH = B_HEADS * B_VAL_DIM
GATE_RANK = 16
GATE_TAU = 16.0
PAGE_SIZE = 128
EPS = 1e-6
NEG = -1e30

LANE = 128
VMEM_LIMIT = 56 * 1024 * 1024

_OFF = np.cumsum([0, A_WIDTH, A_WIDTH, A_WIDTH, A_WIDTH, B_KEY_WIDTH, B_KEY_WIDTH,
                  B_VAL_WIDTH, B_VAL_WIDTH, GATE_RANK]).tolist()
_ROW_W = 3 * 512 + 2 * B_KEY_WIDTH + LANE
_AUG0 = 128
_MASK0 = 136
_LOG2E = 1.4426950408889634
_PROJ_ROWS = 512


def _dot(a, b, dims=(((1,), (0,)), ((), ())), precision=None):
    return lax.dot_general(a, b, dims, precision=precision, preferred_element_type=F32)


_NT = (((1,), (1,)), ((), ()))
_TN = (((0,), (0,)), ((), ()))


def _silu(x):
    return x / (1.0 + jnp.exp(-x))


def _log_sigmoid(x):
    return jnp.minimum(x, 0.0) - jnp.log1p(jnp.exp(-jnp.abs(x)))


def _proj_kernel(x_ref, gpre_ref, wt_ref, wrow_ref, wa2_ref, ba_ref, gq_ref, gk_ref, *out_refs,
                 tm, sample):
    x = x_ref[...]
    ms = jnp.mean(x * x, axis=-1, keepdims=True)
    h = ((x * lax.rsqrt(ms + EPS)) * gpre_ref[...]).astype(BF16)

    pt = _dot(wt_ref[...], h, _NT)

    def head_norm(t, g):
        t3 = t.reshape(A_HEADS, A_HEAD_DIM, tm)
        ss = jnp.mean(t3 * t3, axis=1, keepdims=True)
        return (t3 * lax.rsqrt(ss + EPS)).reshape(A_WIDTH, tm) * g

    q_t = head_norm(pt[0:A_WIDTH], gq_ref[...]) * (A_HEAD_DIM ** -0.5)
    k_t = head_norm(pt[A_WIDTH:2 * A_WIDTH], gk_ref[...])
    v_t = pt[2 * A_WIDTH:3 * A_WIDTH]

    def seg(i0, i1):
        return _dot(h, wrow_ref[:, i0:i1])

    sza = _silu(seg(0, 512)).astype(BF16)
    szb = _silu(seg(512, 1024)).astype(BF16)
    qk = seg(1024, 1536)
    qb = qk[:, 0:B_KEY_WIDTH] * (B_KEY_DIM ** -0.5)
    kb = qk[:, B_KEY_WIDTH:2 * B_KEY_WIDTH]
    vb = seg(1536, 2048)
    ab = seg(2048, 2176).astype(BF16)
    pre = _dot(ab, wa2_ref[...]) + ba_ref[...]
    lg = _log_sigmoid(pre) * (_LOG2E / GATE_TAU)

    if sample:
        (q_ref, k_ref, v_ref, sza_ref, szb_ref, qb_ref, kb_ref, vb_ref, lg_ref) = out_refs
        q_ref[...] = q_t.T
        k_ref[...] = k_t.T
        v_ref[...] = v_t.T
    else:
        (qt_ref, kt_ref, vt_ref, krow_ref, kmean_ref,
         sza_ref, szb_ref, qb_ref, kb_ref, vb_ref, lg_ref) = out_refs
        qt_ref[0] = q_t
        kt_ref[0] = k_t
        vt_ref[0] = v_t
        k_row = k_t.T
        krow_ref[0] = k_row.astype(BF16)
        kmean_ref[0] = jnp.mean(k_row.reshape(tm // MOBA_BLOCK, MOBA_BLOCK, A_WIDTH), axis=1)
    sza_ref[...] = sza
    szb_ref[...] = szb
    qb_ref[...] = qb
    kb_ref[...] = kb
    vb_ref[...] = vb
    lg_ref[...] = lg


def _proj(x2d, w, batch, sample):
    n = x2d.shape[0]
    tm = min(_PROJ_ROWS, n)
    nt = n // tm
    bpt = tm // MOBA_BLOCK
    const = lambda i: (0, 0)
    row = lambda i: (i, 0)
    in_specs = [
        pl.BlockSpec((tm, D_MODEL), row),
        pl.BlockSpec((1, D_MODEL), const),
        pl.BlockSpec((3 * A_WIDTH, D_MODEL), const),
        pl.BlockSpec((D_MODEL, _ROW_W), const),
        pl.BlockSpec((LANE, B_KEY_WIDTH), const),
        pl.BlockSpec((1, B_KEY_WIDTH), const),
        pl.BlockSpec((A_WIDTH, tm), const),
        pl.BlockSpec((A_WIDTH, tm), const),
    ]
    row_specs = [pl.BlockSpec((tm, wd), row) for wd in (512, 512, B_KEY_WIDTH, B_KEY_WIDTH, 512, B_KEY_WIDTH)]
    row_shapes = [jax.ShapeDtypeStruct((n, wd), dt) for wd, dt in
                  ((512, BF16), (512, BF16), (B_KEY_WIDTH, F32), (B_KEY_WIDTH, F32), (512, F32),
                   (B_KEY_WIDTH, F32))]
    if sample:
        out_specs = [pl.BlockSpec((tm, 512), row)] * 3 + row_specs
        out_shape = [jax.ShapeDtypeStruct((n, 512), F32)] * 3 + row_shapes
    else:
        t = n // batch
        tpb = t // tm
        feat = lambda i: (i // tpb, 0, i % tpb)
        out_specs = ([pl.BlockSpec((1, A_WIDTH, tm), feat)] * 3
                     + [pl.BlockSpec((1, tm, 512), lambda i: (i // tpb, i % tpb, 0)),
                        pl.BlockSpec((1, bpt, 512), lambda i: (i, 0, 0))]
                     + row_specs)
        out_shape = ([jax.ShapeDtypeStruct((batch, A_WIDTH, t), F32)] * 3
                     + [jax.ShapeDtypeStruct((batch, t, 512), BF16),
                        jax.ShapeDtypeStruct((nt, bpt, 512), F32)]
                     + row_shapes)
    return pl.pallas_call(
        functools.partial(_proj_kernel, tm=tm, sample=sample),
        grid=(nt,),
        in_specs=in_specs,
        out_specs=out_specs,
        out_shape=out_shape,
        compiler_params=pltpu.CompilerParams(dimension_semantics=("arbitrary",),
                                             vmem_limit_bytes=VMEM_LIMIT),
        name="proj_sample" if sample else "proj_prompt",
    )(x2d, w["g_pre"], w["w_t"], w["w_row"], w["w_a2p"], w["b_ap"], w["gq_t"], w["gk_t"])


def _top3_rows(g, n):
    idx = lax.broadcasted_iota(jnp.int32, g.shape, 0)
    sel = jnp.zeros(g.shape, F32)
    for _ in range(MOBA_TOPK):
        m = jnp.max(g, axis=0, keepdims=True)
        first = jnp.min(jnp.where(g == m, idx, n), axis=0, keepdims=True)
        pick = idx == first
        sel = jnp.where(pick, 1.0, sel)
        g = jnp.where(pick, -jnp.inf, g)
    return sel


_QB = 8


def _moba_prompt_kernel(qt_ref, krow_ref, vt_ref, kmean_ref, qaug_ref, kaug_ref, o_ref,
                        qop_ref, acc_ref, m_ref, s_ref, cm_ref, *, nb):
    blk = MOBA_BLOCK
    qw = _QB * blk
    i0 = pl.program_id(2) * _QB

    qt = qt_ref[0]
    row128 = lax.broadcasted_iota(jnp.int32, (2 * A_HEAD_DIM, qw), 0)
    lane128 = lax.broadcasted_iota(jnp.int32, (nb, LANE), 1)
    bidx = lax.broadcasted_iota(jnp.int32, (nb, qw), 0)
    iq = i0 + lax.broadcasted_iota(jnp.int32, (nb, qw), 1) // blk
    kmean = kmean_ref[0]
    qt_hi = qt.astype(BF16)
    qt_lo = (qt - qt_hi.astype(F32)).astype(BF16)

    for hh in range(2):
        in_head = (row128 >= hh * A_HEAD_DIM) & (row128 < (hh + 1) * A_HEAD_DIM)
        q_h = jnp.where(in_head, qt * _LOG2E, 0.0)
        km_h = jnp.where((lane128 >= hh * A_HEAD_DIM) & (lane128 < (hh + 1) * A_HEAD_DIM), kmean, 0.0)
        km_hi = km_h.astype(BF16)
        km_lo = (km_h - km_hi.astype(F32)).astype(BF16)
        gate = _dot(km_hi, qt_hi) + _dot(km_hi, qt_lo) + _dot(km_lo, qt_hi)
        past = bidx < iq
        sel = _top3_rows(jnp.where(past, gate, NEG), nb)
        keep = ((sel > 0.5) & past) | (bidx == iq)
        maskbias = jnp.where(keep, 0.0, NEG)
        for qb in range(_QB):
            c = hh * _QB + qb
            cols = slice(qb * blk, (qb + 1) * blk)
            qop = jnp.concatenate(
                [q_h[:, cols], qaug_ref[0, c], maskbias[:, cols],
                 jnp.zeros((2 * LANE - _MASK0 - nb, blk), F32)], axis=0)
            qop_ref[c] = qop.astype(BF16)
            acc_ref[c] = jnp.zeros(acc_ref.shape[1:], F32)
            m_ref[c] = jnp.full(m_ref.shape[1:], -jnp.inf, F32)

    lane_k = lax.broadcasted_iota(jnp.int32, (blk, LANE), 1)
    kaug_base = kaug_ref[...]
    ones_rows = jnp.ones((8, blk), BF16)
    kk = lax.broadcasted_iota(jnp.int32, (blk, blk), 0)
    qq = lax.broadcasted_iota(jnp.int32, (blk, blk), 1)

    def scores(j, slot, chains):
        off = pl.multiple_of(j * blk, blk)
        shift = ((j - i0) * blk).astype(F32)
        kaug = jnp.where((lane_k >= 3) & (lane_k < 6), shift.astype(BF16),
                         jnp.where(lane_k == 8 + j, jnp.ones((), BF16), kaug_base))
        kop = jnp.concatenate([krow_ref[0, pl.ds(off, blk), :], kaug], axis=1)
        for c in chains:
            s = _dot(kop, qop_ref[c])
            s_ref[slot, c] = s
            cm_ref[slot, c] = jnp.max(s.reshape(blk // 8, 8, blk), axis=0)

    def absorb(j, slot, chains):
        off = pl.multiple_of(j * blk, blk)
        vops = []
        for hh in range(2):
            v_t = vt_ref[0, hh * A_HEAD_DIM:(hh + 1) * A_HEAD_DIM, pl.ds(off, blk)].astype(BF16)
            vops.append(jnp.concatenate([v_t, ones_rows], axis=0))
        for c, causal in chains:
            s = s_ref[slot, c]
            if causal:
                s = jnp.where(kk <= qq, s, NEG)
            m_prev = m_ref[c][0:1]
            cmax = s if causal else cm_ref[slot, c]
            m_new = jnp.maximum(m_prev, jnp.max(cmax, axis=0, keepdims=True))
            alpha = jnp.exp2(m_prev - m_new)
            p = jnp.exp2(s - m_new).astype(BF16)
            acc_ref[c] = alpha * acc_ref[c] + _dot(vops[c // _QB], p)
            m_ref[c] = jnp.broadcast_to(m_new, m_ref.shape[1:])

    all_c = list(range(2 * _QB))
    plain = [(c, False) for c in all_c]
    scores(0, 0, all_c)

    def body(jj, carry):
        j = _QB * jj
        for d in range(_QB):
            scores(j + d + 1, (d + 1) % 2, all_c)
            absorb(j + d, d % 2, plain)
        return carry

    lax.fori_loop(0, i0 // _QB, body, 0)
    for d in range(_QB):
        if d + 1 < _QB:
            scores(i0 + d + 1, (d + 1) % 2, [c for c in all_c if c % _QB >= d + 1])
        absorb(i0 + d, d % 2, [(c, c % _QB == d) for c in all_c if c % _QB >= d])

    for qb in range(_QB):
        outs = []
        for hh in range(2):
            acc = acc_ref[hh * _QB + qb]
            outs.append(acc[0:A_HEAD_DIM] / acc[A_HEAD_DIM:A_HEAD_DIM + 1])
        o_ref[0, qb * blk:(qb + 1) * blk, :] = jnp.concatenate(outs, axis=0).T.astype(o_ref.dtype)


def _moba_prompt(qt, krow, vt, kmean, w):
    batch, _, t = qt.shape
    nb = t // MOBA_BLOCK
    blk = MOBA_BLOCK
    qw = _QB * blk
    return pl.pallas_call(
        functools.partial(_moba_prompt_kernel, nb=nb),
        grid=(batch, A_HEADS // 2, nb // _QB),
        in_specs=[
            pl.BlockSpec((1, 2 * A_HEAD_DIM, qw), lambda b, hp, i: (b, hp, i)),
            pl.BlockSpec((1, t, LANE), lambda b, hp, i: (b, 0, hp)),
            pl.BlockSpec((1, 2 * A_HEAD_DIM, t), lambda b, hp, i: (b, hp, 0)),
            pl.BlockSpec((1, nb, LANE), lambda b, hp, i: (b, 0, hp)),
            pl.BlockSpec((1, 2 * _QB, 8, blk), lambda b, hp, i: (hp, 0, 0, 0)),
            pl.BlockSpec((blk, LANE), lambda b, hp, i: (0, 0)),
        ],
        out_specs=pl.BlockSpec((1, qw, LANE), lambda b, hp, i: (b, i, hp)),
        out_shape=jax.ShapeDtypeStruct((batch, t, A_WIDTH), BF16),
        scratch_shapes=[
            pltpu.VMEM((2 * _QB, 2 * LANE, blk), BF16),
            pltpu.VMEM((2 * _QB, A_HEAD_DIM + 8, blk), F32),
            pltpu.VMEM((2 * _QB, 8, blk), F32),
            pltpu.VMEM((2, 2 * _QB, blk, blk), F32),
            pltpu.VMEM((2, 2 * _QB, 8, blk), F32),
        ],
        compiler_params=pltpu.CompilerParams(
            dimension_semantics=("arbitrary", "arbitrary", "arbitrary"),
            vmem_limit_bytes=VMEM_LIMIT),
        name="moba_prompt",
    )(qt, krow, vt, kmean.reshape(batch, nb, A_WIDTH), w["qaug"], w["kaug"])


def _moba_sample_kernel(pt_ref, q_ref, qn_ref, kn_ref, vn_ref, slope_ref, ck_hbm, cv_hbm, o_ref,
                        kbuf, sall, gate_ref, idx_v, idx_s, vbuf, sem_k, sem_v, sem_i,
                        *, n_pages, npg, ring):
    b = pl.program_id(0)
    nbatch = pl.num_programs(0)
    nxt = jnp.minimum(b + 1, nbatch - 1)
    n_chunks = n_pages // npg
    cw = npg * PAGE_SIZE
    bpc = cw // MOBA_BLOCK
    past = n_pages * PAGE_SIZE
    n_blocks = past // MOBA_BLOCK
    t_new = q_ref.shape[2]
    rows = A_HEADS * t_new
    ppb = MOBA_BLOCK // PAGE_SIZE
    slot_b = b % 2

    def k_copy(s, c, p):
        page = pt_ref[s, c * npg + p]
        return pltpu.make_async_copy(
            ck_hbm.at[0, page], kbuf.at[c % ring, p],
            sem_k.at[c % ring])

    def start_chunk(s, c):
        for p in range(npg):
            k_copy(s, c, p).start(priority=1)

    def start_ring(s):
        for c in range(ring):
            start_chunk(s, c)

    lane_g = lax.broadcasted_iota(jnp.int32, (t_new, LANE), 1)

    def split_q(qr):
        ops = []
        for h in range(A_HEADS):
            qh = qr[0, h]
            hi = qh.astype(BF16)
            ops.append(jnp.concatenate([hi, (qh - hi.astype(F32)).astype(BF16)], axis=0))
        return ops

    def sweep_chunk(s, q_ops, slot, c):
        for p in range(npg):
            k_copy(s, c, p).wait()
        for h in range(A_HEADS):
            kt = jnp.concatenate([kbuf[c % ring, p, h] for p in range(npg)], axis=1)
            s2 = _dot(q_ops[h], kt.astype(BF16))
            sc = s2[0:t_new] + s2[t_new:2 * t_new]
            sall[slot, h * t_new:(h + 1) * t_new, c * cw:(c + 1) * cw] = sc
            g = gate_ref[h * t_new:(h + 1) * t_new, :]
            for jb in range(bpc):
                t2 = (sc[:, jb * MOBA_BLOCK:jb * MOBA_BLOCK + LANE]
                      + sc[:, jb * MOBA_BLOCK + LANE:(jb + 1) * MOBA_BLOCK])
                g = jnp.where(lane_g == c * bpc + jb, jnp.sum(t2, axis=1, keepdims=True), g)
            gate_ref[h * t_new:(h + 1) * t_new, :] = g
        if c + ring < n_chunks:
            start_chunk(s, c + ring)

    picks = [(t, r) for t in range(t_new) for r in range(MOBA_TOPK)]

    def v_copy(s, h, t, r, pg):
        bsel = idx_s[h * t_new + t, r]
        page = pt_ref[s, ppb * bsel + pg]
        return pltpu.make_async_copy(
            cv_hbm.at[0, page, h], vbuf.at[h, (t * MOBA_TOPK + r) * ppb + pg], sem_v.at[h])

    def v_start(s, h):
        for t, r in picks:
            for pg in range(ppb):
                v_copy(s, h, t, r, pg).start()

    ahead = 2

    def select_blocks(s):
        lane_r = lax.broadcasted_iota(jnp.int32, (rows, LANE), 1)
        g = jnp.where(lane_r < n_blocks, gate_ref[...], -jnp.inf)
        idx = jnp.zeros((rows, LANE), jnp.int32)
        for r in range(MOBA_TOPK):
            m = jnp.max(g, axis=1, keepdims=True)
            first = jnp.min(jnp.where(g == m, lane_r, LANE), axis=1, keepdims=True)
            g = jnp.where(lane_r == first, -jnp.inf, g)
            idx = jnp.where(lane_r == r, first, idx)
        idx_v[...] = idx
        cp_i = pltpu.make_async_copy(idx_v, idx_s, sem_i)
        cp_i.start()
        cp_i.wait()
        for h0 in range(ahead):
            v_start(s, h0)

    @pl.when(b == 0)
    def _():
        start_ring(0)
        gate_ref[...] = jnp.zeros(gate_ref.shape, F32)
        q0 = split_q(q_ref)
        for c in range(n_chunks):
            sweep_chunk(0, q0, 0, c)
        start_ring(nxt)
        select_blocks(0)

    sub = lax.broadcasted_iota(jnp.int32, (t_new, MOBA_BLOCK), 0)
    klane = lax.broadcasted_iota(jnp.int32, (t_new, MOBA_BLOCK), 1)
    trow = lax.broadcasted_iota(jnp.int32, (t_new, t_new), 0)
    tcol = lax.broadcasted_iota(jnp.int32, (t_new, t_new), 1)

    def finish_head(h):
        if h + ahead < A_HEADS:
            v_start(b, h + ahead)
        r0 = h * t_new
        slope = slope_ref[r0:r0 + t_new, :][:, 0:1]
        pieces = []
        for t, r in picks:
            bsel = idx_s[r0 + t, r]
            off = pl.multiple_of(bsel * MOBA_BLOCK, MOBA_BLOCK)
            dist = ((past + t - bsel * MOBA_BLOCK) - klane).astype(F32)
            sc = sall[slot_b, r0:r0 + t_new, pl.ds(off, MOBA_BLOCK)] - slope * dist
            pieces.append(jnp.where(sub == t, sc, NEG))
        s_sel = jnp.concatenate(pieces, axis=1)
        s_new = _dot(q_ref[0, h], kn_ref[0, h], _NT) - slope * (trow - tcol).astype(F32)
        s_new = jnp.where(tcol <= trow, s_new, NEG)
        m = jnp.maximum(jnp.max(s_sel, axis=1, keepdims=True), jnp.max(s_new, axis=1, keepdims=True))
        p_sel = jnp.exp(s_sel - m)
        p_new = jnp.exp(s_new - m)
        l = jnp.sum(p_sel, axis=1, keepdims=True) + jnp.sum(p_new, axis=1, keepdims=True)
        for t, r in picks:
            for pg in range(ppb):
                v_copy(b, h, t, r, pg).wait()
        v_sel = jnp.concatenate([vbuf[h, j] for j in range(len(picks) * ppb)], axis=1)
        acc = _dot(p_sel.astype(BF16), v_sel.astype(BF16), _NT) + _dot(p_new, vn_ref[0, h])
        o_ref[0, h] = acc / l

    gate_ref[...] = jnp.zeros(gate_ref.shape, F32)
    qn = split_q(qn_ref)
    for i in range(max(n_chunks, A_HEADS)):
        if i < n_chunks:
            sweep_chunk(nxt, qn, 1 - slot_b, i)
        if i < A_HEADS:
            finish_head(i)

    @pl.when(b + 1 < nbatch)
    def _():
        start_ring(jnp.minimum(b + 2, nbatch - 1))
        select_blocks(nxt)


def _moba_sample(q4, kn4, vn4, cache_kt, cache_vt, page_table, slope_rows):
    nbatch, _, t_new, _ = q4.shape
    n_pages = page_table.shape[1]
    npg = 16
    ring = min(4, n_pages // npg)
    rows = A_HEADS * t_new
    past = n_pages * PAGE_SIZE
    shape4 = (1, A_HEADS, t_new, A_HEAD_DIM)
    blk4 = pl.BlockSpec(shape4, lambda b, pt: (b, 0, 0, 0))
    nxt4 = pl.BlockSpec(shape4, lambda b, pt: (jnp.minimum(b + 1, nbatch - 1), 0, 0, 0))
    return pl.pallas_call(
        functools.partial(_moba_sample_kernel, n_pages=n_pages, npg=npg, ring=ring),
        grid_spec=pltpu.PrefetchScalarGridSpec(
            num_scalar_prefetch=1,
            grid=(nbatch,),
            in_specs=[blk4, nxt4, blk4, blk4,
                      pl.BlockSpec((rows, LANE), lambda b, pt: (0, 0)),
                      pl.BlockSpec(memory_space=pl.ANY),
                      pl.BlockSpec(memory_space=pl.ANY)],
            out_specs=blk4,
            scratch_shapes=[
                pltpu.VMEM((ring, npg, A_HEADS, A_HEAD_DIM, PAGE_SIZE), F32),
                pltpu.VMEM((2, rows, past), F32),
                pltpu.VMEM((rows, LANE), F32),
                pltpu.VMEM((rows, LANE), jnp.int32),
                pltpu.SMEM((rows, LANE), jnp.int32),
                pltpu.VMEM((A_HEADS, t_new * MOBA_TOPK * (MOBA_BLOCK // PAGE_SIZE), A_HEAD_DIM, PAGE_SIZE), F32),
                pltpu.SemaphoreType.DMA((ring,)),
                pltpu.SemaphoreType.DMA((A_HEADS,)),
                pltpu.SemaphoreType.DMA,
            ]),
        out_shape=jax.ShapeDtypeStruct((nbatch, A_HEADS, t_new, A_HEAD_DIM), F32),
        compiler_params=pltpu.CompilerParams(dimension_semantics=("arbitrary",),
                                             vmem_limit_bytes=VMEM_LIMIT),
        name="moba_sample",
    )(page_table, q4, q4, kn4, vn4, slope_rows, cache_kt, cache_vt)


def _gla_kernel(q_ref, k_ref, g_ref, v_ref, s0_ref, o_ref, sfin_ref, state, *, chunk, n_chunks, nbb):
    c = chunk
    mx = BF16 if c >= 16 else F32
    hpl = LANE // B_KEY_DIM
    zpad = jnp.zeros((B_KEY_DIM, B_VAL_DIM), F32)

    @pl.when(pl.program_id(2) == 0)
    def _():
        for bi in range(nbb):
            for hh in range(hpl):
                parts = [s0_ref[bi, hh] if j == hh else zpad for j in range(hpl)]
                state[bi, hh] = jnp.concatenate(parts, axis=0).T

    ri = lax.broadcasted_iota(jnp.int32, (c, LANE), 0)
    lane_c = lax.broadcasted_iota(jnp.int32, (c, LANE), 1)
    head_lanes = [(lane_c // B_KEY_DIM) == hh for hh in range(hpl)]
    ti = lax.broadcasted_iota(jnp.int32, (c, c), 0)
    si = lax.broadcasted_iota(jnp.int32, (c, c), 1)
    tril = jnp.where(si <= ti, 1.0, 0.0).astype(mx)
    sub3 = lax.broadcasted_iota(jnp.int32, (c // 8, 8, LANE), 1)
    levels = []
    half = c // 2
    while half >= 1:
        width = 2 * half
        sb = int(np.log2(width))
        upper = (ri & (width - 1)) >= half
        pair = ((ti >> sb) == (si >> sb)) & ((ti & (width - 1)) >= half) & ((si & (width - 1)) < half)
        levels.append((half, width, upper, pair))
        half //= 2

    seq = [(bi, ci) for bi in range(nbb) for ci in range(n_chunks)]
    loaded = []
    for bi, ci in seq:
        g = g_ref[bi, pl.ds(ci * c, c), :]
        g_hi = g.astype(BF16)
        g_lo = (g - g_hi.astype(F32)).astype(BF16)
        loaded.append((g, _dot(tril, g_hi.astype(mx)) + _dot(tril, g_lo.astype(mx))))

    prepared = []
    for (bi, ci), (g, b) in zip(seq, loaded):
        rows = pl.ds(ci * c, c)
        q = q_ref[bi, rows, :]
        k = k_ref[bi, rows, :]
        v32 = v_ref[bi, rows, :]
        b_last = b[c - 1:c, :]
        b3 = b.reshape(c // 8, 8, LANE)
        zs = []
        for half, width, upper, _ in levels:
            if half == 1:
                x = jnp.where(upper, g, 0.0)
            else:
                if half >= 8:
                    mid = jnp.concatenate(
                        [jnp.broadcast_to(b[m * width + half - 1:m * width + half, :], (width, LANE))
                         for m in range(c // width)], axis=0)
                else:
                    mid3 = jnp.broadcast_to(b3[:, half - 1:half, :], b3.shape)
                    for m in range(1, 8 // width):
                        r = m * width + half - 1
                        mid3 = jnp.where(sub3 >= m * width,
                                         jnp.broadcast_to(b3[:, r:r + 1, :], b3.shape), mid3)
                    mid = mid3.reshape(c, LANE)
                x = jnp.where(upper, b - mid, mid - b)
            zs.append(jnp.where(upper, q, k) * jnp.exp2(x))
        qd = q * jnp.exp2(b)
        kd = k * jnp.exp2(b_last - b)
        qk = q * k
        heads = []
        for hh in range(hpl):
            own = head_lanes[hh]
            cut = lambda a: jnp.where(own, a, 0.0).astype(mx)
            v_h = v32[:, hh * B_VAL_DIM:(hh + 1) * B_VAL_DIM]
            o_same = jnp.sum(jnp.where(own, qk, 0.0), axis=1, keepdims=True) * v_h
            heads.append((v_h.astype(mx), [cut(z) for z in zs], cut(qd), cut(kd), o_same))
        prepared.append((heads, jnp.exp2(b_last)))

    grams, updates = [], []
    for heads, _ in prepared:
        grams.append([[_dot(z, z, _NT) for z in zh] for _, zh, _, _, _ in heads])
        updates.append([_dot(v, kd, _TN) for v, _, _, kd, _ in heads])

    attns = []
    for gr in grams:
        per_head = []
        for gh in gr:
            attn = jnp.where(levels[0][3], gh[0], 0.0)
            for (_, _, _, pair), gm in zip(levels[1:], gh[1:]):
                attn = attn + jnp.where(pair, gm, 0.0)
            per_head.append(attn.astype(mx))
        attns.append(per_head)

    states = []
    for n, (bi, ci) in enumerate(seq):
        a_last = prepared[n][1]
        cur = []
        for hh in range(hpl):
            st = state[bi, hh] if ci == 0 else states[-1][hh][1]
            cur.append((st, a_last * st + updates[n][hh]))
            if ci == n_chunks - 1:
                state[bi, hh] = cur[-1][1]
        states.append(cur)

    for n, (bi, ci) in enumerate(seq):
        outs = []
        for hh, (v, _, qd, _, o_same) in enumerate(prepared[n][0]):
            outs.append(o_same + _dot(attns[n][hh], v) + _dot(qd, states[n][hh][0].astype(mx), _NT))
        o_ref[bi, pl.ds(ci * c, c), :] = jnp.concatenate(outs, axis=1).astype(o_ref.dtype)

    @pl.when(pl.program_id(2) == pl.num_programs(2) - 1)
    def _():
        for bi in range(nbb):
            for hh in range(hpl):
                sfin_ref[bi, hh] = state[bi, hh].T[hh * B_KEY_DIM:(hh + 1) * B_KEY_DIM, :]


def _gla(qb, kb, lg, vb, s0, chunk):
    batch, t, _ = qb.shape
    tc = min(t, 1024)
    n_chunks = tc // chunk
    nbb = 8 if (t == tc and batch % 8 == 0) else 1
    hpl = LANE // B_KEY_DIM
    seq = pl.BlockSpec((nbb, tc, LANE), lambda b, h, c: (b, c, h))
    seq_v = pl.BlockSpec((nbb, tc, hpl * B_VAL_DIM), lambda b, h, c: (b, c, h))
    st = pl.BlockSpec((nbb, hpl, B_KEY_DIM, B_VAL_DIM), lambda b, h, c: (b, h, 0, 0))
    return pl.pallas_call(
        functools.partial(_gla_kernel, chunk=chunk, n_chunks=n_chunks, nbb=nbb),
        grid=(batch // nbb, B_HEADS // hpl, t // tc),
        in_specs=[seq, seq, seq, seq_v, st],
        out_specs=[seq_v, st],
        out_shape=[jax.ShapeDtypeStruct((batch, t, B_VAL_WIDTH), BF16 if chunk >= 16 else F32),
                   jax.ShapeDtypeStruct((batch, B_HEADS, B_KEY_DIM, B_VAL_DIM), F32)],
        scratch_shapes=[pltpu.VMEM((nbb, hpl, B_VAL_DIM, LANE), F32)],
        compiler_params=pltpu.CompilerParams(
            dimension_semantics=("arbitrary", "arbitrary", "arbitrary"),
            vmem_limit_bytes=VMEM_LIMIT),
        name="gla_c%d" % chunk,
    )(qb, kb, lg, vb, s0)


def _merge_kernel(x_ref, oa_ref, sza_ref, ob_ref, szb_ref, gg_ref, wo_ref, y_ref):
    ya = oa_ref[...].astype(F32) * sza_ref[...].astype(F32)
    ob = ob_ref[...].astype(F32)
    parts = []
    for h in range(B_HEADS):
        oh = ob[:, h * B_VAL_DIM:(h + 1) * B_VAL_DIM]
        ms = jnp.mean(oh * oh, axis=-1, keepdims=True)
        parts.append(oh * lax.rsqrt(ms + EPS))
    yb = (jnp.concatenate(parts, axis=1) * gg_ref[...]) * szb_ref[...].astype(F32)
    cat = jnp.concatenate([ya, yb], axis=1).astype(BF16)
    y_ref[...] = x_ref[...] + _dot(cat, wo_ref[...])


def _merge(x2d, oa, sza, ob, szb, w):
    n = x2d.shape[0]
    tm = min(n, 1024)
    row = lambda i: (i, 0)
    const = lambda i: (0, 0)
    return pl.pallas_call(
        _merge_kernel,
        grid=(n // tm,),
        in_specs=[pl.BlockSpec((tm, D_MODEL), row), pl.BlockSpec((tm, 512), row),
                  pl.BlockSpec((tm, 512), row), pl.BlockSpec((tm, 512), row),
                  pl.BlockSpec((tm, 512), row), pl.BlockSpec((1, 512), const),
                  pl.BlockSpec((D_MODEL, D_MODEL), const)],
        out_specs=pl.BlockSpec((tm, D_MODEL), row),
        out_shape=jax.ShapeDtypeStruct((n, D_MODEL), F32),
        compiler_params=pltpu.CompilerParams(dimension_semantics=("arbitrary",),
                                             vmem_limit_bytes=VMEM_LIMIT),
        name="merge_out",
    )(x2d, oa, sza, ob, szb, w["g_gla"], w["w_out"])


def _layer_weights(g_pre, w_in, g_q, g_k, w_a2, b_a, g_gla, w_out):
    o = _OFF
    w_t = w_in[:, o[0]:o[3]].T.astype(BF16)
    w_row = jnp.concatenate([
        w_in[:, o[3]:o[4]], w_in[:, o[7]:o[8]], w_in[:, o[4]:o[5]], w_in[:, o[5]:o[6]],
        w_in[:, o[6]:o[7]],
        jnp.pad(w_in[:, o[8]:o[9]], ((0, 0), (0, LANE - GATE_RANK)))], axis=1).astype(BF16)
    w_a2p = jnp.pad(w_a2, ((0, LANE - GATE_RANK), (0, 0))).astype(BF16)
    b_ap = b_a.reshape(1, B_KEY_WIDTH)
    gq_t = jnp.broadcast_to(jnp.tile(g_q, A_HEADS)[:, None], (A_WIDTH, _PROJ_ROWS))
    gk_t = jnp.broadcast_to(jnp.tile(g_k, A_HEADS)[:, None], (A_WIDTH, _PROJ_ROWS))
    slopes = jnp.asarray([2.0 ** (-8.0 * (h + 1) / A_HEADS) for h in range(A_HEADS)], F32)
    c = slopes * _LOG2E
    pieces = []
    rem = c
    for _ in range(3):
        pc = rem.astype(BF16).astype(F32)
        pieces.append(pc)
        rem = rem - pc
    qq = jnp.arange(MOBA_BLOCK, dtype=F32)
    qaug = jnp.zeros((A_HEADS, _QB, 8, MOBA_BLOCK), F32)
    for p_i, pc in enumerate(pieces):
        qaug = qaug.at[:, :, p_i, :].set(pc[:, None, None])
        qaug = qaug.at[:, :, 3 + p_i, :].set(pc[:, None, None])
    qpos = qq[None, None, :] + (MOBA_BLOCK * jnp.arange(_QB, dtype=F32))[None, :, None]
    qaug = qaug.at[:, :, 6, :].set(-c[:, None, None] * qpos)
    qaug = qaug.reshape(A_HEADS // 2, 2 * _QB, 8, MOBA_BLOCK)
    kaug = jnp.zeros((MOBA_BLOCK, LANE), F32)
    kaug = kaug.at[:, 0:3].set(qq[:, None]).at[:, 6].set(1.0).astype(BF16)
    return {
        "g_pre": g_pre.reshape(1, D_MODEL), "w_t": w_t, "w_row": w_row, "w_a2p": w_a2p, "b_ap": b_ap,
        "gq_t": gq_t, "gk_t": gk_t, "g_gla": g_gla.reshape(1, B_VAL_WIDTH),
        "w_out": w_out.astype(BF16), "qaug": qaug, "kaug": kaug, "slopes": slopes,
    }


def kernel(x_prompt, x_sample, cache_k, cache_v, state_gla, page_table, g_pre, w_in, g_q, g_k, w_a2, b_a, g_gla, w_out):
    depth = g_pre.shape[0]
    batch, t, _ = x_prompt.shape
    nb_s, t_new, _ = x_sample.shape
    cache_kt = jnp.transpose(cache_k, (0, 1, 3, 4, 2))
    cache_vt = jnp.transpose(cache_v, (0, 1, 3, 4, 2))
    yp = x_prompt.reshape(batch * t, D_MODEL)
    ys = x_sample.reshape(nb_s * t_new, D_MODEL)
    kp_l, vp_l, sp_l, ks_l, vs_l, ss_l = [], [], [], [], [], []
    for l in range(depth):
        w = _layer_weights(g_pre[l], w_in[l], g_q[l], g_k[l], w_a2[l], b_a[l], g_gla[l], w_out[l])
        qt, kt, vt, krow, kmean, sza, szb, qb, kb, vb, lg = _proj(yp, w, batch, sample=False)
        oa = _moba_prompt(qt, krow, vt, kmean, w)
        r3 = lambda a: a.reshape(batch, t, a.shape[-1])
        s0 = jnp.zeros((batch, B_HEADS, B_KEY_DIM, B_VAL_DIM), F32)
        ob, s_fin = _gla(r3(qb), r3(kb), r3(lg), r3(vb), s0, chunk=64)
        yp = _merge(yp, oa.reshape(batch * t, A_WIDTH), sza, ob.reshape(batch * t, 512), szb, w)
        kp_l.append(jnp.transpose(kt.reshape(batch, A_HEADS, A_HEAD_DIM, t), (0, 3, 1, 2)))
        vp_l.append(jnp.transpose(vt.reshape(batch, A_HEADS, A_HEAD_DIM, t), (0, 3, 1, 2)))
        sp_l.append(s_fin)
        qs, ks, vs, sza, szb, qb, kb, vb, lg = _proj(ys, w, nb_s, sample=True)
        h4 = lambda a: jnp.transpose(a.reshape(nb_s, t_new, A_HEADS, A_HEAD_DIM), (0, 2, 1, 3))
        slope_rows = jnp.broadcast_to(jnp.repeat(w["slopes"], t_new)[:, None], (A_HEADS * t_new, LANE))
        oa4 = _moba_sample(h4(qs), h4(ks), h4(vs), cache_kt[l:l + 1], cache_vt[l:l + 1],
                           page_table, slope_rows)
        oa = jnp.transpose(oa4, (0, 2, 1, 3)).reshape(nb_s * t_new, A_WIDTH)
        r3 = lambda a: a.reshape(nb_s, t_new, a.shape[-1])
        ob, s_new = _gla(r3(qb), r3(kb), r3(lg), r3(vb), state_gla[l], chunk=t_new)
        ys = _merge(ys, oa, sza, ob.reshape(nb_s * t_new, 512), szb, w)
        ks_l.append(ks.reshape(nb_s, t_new, A_HEADS, A_HEAD_DIM))
        vs_l.append(vs.reshape(nb_s, t_new, A_HEADS, A_HEAD_DIM))
        ss_l.append(s_new)
    return (yp.reshape(batch, t, D_MODEL), ys.reshape(nb_s, t_new, D_MODEL),
            jnp.stack(kp_l), jnp.stack(vp_l), jnp.stack(sp_l),
            jnp.stack(ks_l), jnp.stack(vs_l), jnp.stack(ss_l))
```

```python
import functools

import jax
import jax.numpy as jnp
import numpy as np
from jax import lax
from jax.experimental import pallas as pl
from jax.experimental.pallas import tpu as pltpu

F32 = jnp.float32
BF16 = jnp.bfloat16

D_MODEL = 1024
A_HEADS = 8
A_HEAD_DIM = 64
A_WIDTH = A_HEADS * A_HEAD_DIM
MOBA_BLOCK = 256
MOBA_TOPK = 3
B_HEADS = 4
B_KEY_DIM = 64
B_VAL_DIM = 128
B_KEY_WIDTH = B_HEADS * B_KEY_DIM
B_VAL_WIDTH = B_HEADS * B_VAL_DIM
GATE_RANK = 16
GATE_TAU = 16.0
PAGE_SIZE = 128
EPS = 1e-6
NEG = -1e30

LANE = 128
VMEM_LIMIT = 56 * 1024 * 1024

_OFF = np.cumsum([0, A_WIDTH, A_WIDTH, A_WIDTH, A_WIDTH, B_KEY_WIDTH, B_KEY_WIDTH,
                  B_VAL_WIDTH, B_VAL_WIDTH, GATE_RANK]).tolist()
_ROW_W = 3 * 512 + 2 * B_KEY_WIDTH + LANE
_AUG0 = 128
_MASK0 = 136
_LOG2E = 1.4426950408889634
_PROJ_ROWS = 512


def _dot(a, b, dims=(((1,), (0,)), ((), ())), precision=None):
    return lax.dot_general(a, b, dims, precision=precision, preferred_element_type=F32)


_NT = (((1,), (1,)), ((), ()))
_TN = (((0,), (0,)), ((), ()))


def _silu(x):
    return x / (1.0 + jnp.exp(-x))


def _log_sigmoid(x):
    return jnp.minimum(x, 0.0) - jnp.log1p(jnp.exp(-jnp.abs(x)))


def _proj_kernel(x_ref, gpre_ref, wt_ref, wrow_ref, wa2_ref, ba_ref, gq_ref, gk_ref, *out_refs,
                 tm, sample):
    x = x_ref[...]
    ms = jnp.mean(x * x, axis=-1, keepdims=True)
    h = ((x * lax.rsqrt(ms + EPS)) * gpre_ref[...]).astype(BF16)

    pt = _dot(wt_ref[...], h, _NT)

    def head_norm(t, g):
        t3 = t.reshape(A_HEADS, A_HEAD_DIM, tm)
        ss = jnp.mean(t3 * t3, axis=1, keepdims=True)
        return (t3 * lax.rsqrt(ss + EPS)).reshape(A_WIDTH, tm) * g

    q_t = head_norm(pt[0:A_WIDTH], gq_ref[...]) * (A_HEAD_DIM ** -0.5)
    k_t = head_norm(pt[A_WIDTH:2 * A_WIDTH], gk_ref[...])
    v_t = pt[2 * A_WIDTH:3 * A_WIDTH]

    def seg(i0, i1):
        return _dot(h, wrow_ref[:, i0:i1])

    sza = _silu(seg(0, 512)).astype(BF16)
    szb = _silu(seg(512, 1024)).astype(BF16)
    qk = seg(1024, 1536)
    qb = qk[:, 0:B_KEY_WIDTH] * (B_KEY_DIM ** -0.5)
    kb = qk[:, B_KEY_WIDTH:2 * B_KEY_WIDTH]
    vb = seg(1536, 2048)
    ab = seg(2048, 2176).astype(BF16)
    pre = _dot(ab, wa2_ref[...]) + ba_ref[...]
    lg = _log_sigmoid(pre) * (_LOG2E / GATE_TAU)

    if sample:
        (q_ref, k_ref, v_ref, sza_ref, szb_ref, qb_ref, kb_ref, vb_ref, lg_ref) = out_refs
        q_ref[...] = q_t.T
        k_ref[...] = k_t.T
        v_ref[...] = v_t.T
    else:
        (qt_ref, kt_ref, vt_ref, krow_ref, kmean_ref,
         sza_ref, szb_ref, qb_ref, kb_ref, vb_ref, lg_ref) = out_refs
        qt_ref[0] = q_t
        kt_ref[0] = k_t
        vt_ref[0] = v_t
        k_row = k_t.T
        krow_ref[0] = k_row.astype(BF16)
        kmean_ref[0] = jnp.mean(k_row.reshape(tm // MOBA_BLOCK, MOBA_BLOCK, A_WIDTH), axis=1)
    sza_ref[...] = sza
    szb_ref[...] = szb
    qb_ref[...] = qb
    kb_ref[...] = kb
    vb_ref[...] = vb
    lg_ref[...] = lg


def _proj(x2d, w, batch, sample):
    n = x2d.shape[0]
    tm = min(_PROJ_ROWS, n)
    nt = n // tm
    bpt = tm // MOBA_BLOCK
    const = lambda i: (0, 0)
    row = lambda i: (i, 0)
    in_specs = [
        pl.BlockSpec((tm, D_MODEL), row),
        pl.BlockSpec((1, D_MODEL), const),
        pl.BlockSpec((3 * A_WIDTH, D_MODEL), const),
        pl.BlockSpec((D_MODEL, _ROW_W), const),
        pl.BlockSpec((LANE, B_KEY_WIDTH), const),
        pl.BlockSpec((1, B_KEY_WIDTH), const),
        pl.BlockSpec((A_WIDTH, tm), const),
        pl.BlockSpec((A_WIDTH, tm), const),
    ]
    row_specs = [pl.BlockSpec((tm, wd), row) for wd in (512, 512, B_KEY_WIDTH, B_KEY_WIDTH, 512, B_KEY_WIDTH)]
    row_shapes = [jax.ShapeDtypeStruct((n, wd), dt) for wd, dt in
                  ((512, BF16), (512, BF16), (B_KEY_WIDTH, F32), (B_KEY_WIDTH, F32), (512, F32),
                   (B_KEY_WIDTH, F32))]
    if sample:
        out_specs = [pl.BlockSpec((tm, 512), row)] * 3 + row_specs
        out_shape = [jax.ShapeDtypeStruct((n, 512), F32)] * 3 + row_shapes
    else:
        t = n // batch
        tpb = t // tm
        feat = lambda i: (i // tpb, 0, i % tpb)
        out_specs = ([pl.BlockSpec((1, A_WIDTH, tm), feat)] * 3
                     + [pl.BlockSpec((1, tm, 512), lambda i: (i // tpb, i % tpb, 0)),
                        pl.BlockSpec((1, bpt, 512), lambda i: (i, 0, 0))]
                     + row_specs)
        out_shape = ([jax.ShapeDtypeStruct((batch, A_WIDTH, t), F32)] * 3
                     + [jax.ShapeDtypeStruct((batch, t, 512), BF16),
                        jax.ShapeDtypeStruct((nt, bpt, 512), F32)]
                     + row_shapes)
    return pl.pallas_call(
        functools.partial(_proj_kernel, tm=tm, sample=sample),
        grid=(nt,),
        in_specs=in_specs,
        out_specs=out_specs,
        out_shape=out_shape,
        compiler_params=pltpu.CompilerParams(dimension_semantics=("arbitrary",),
                                             vmem_limit_bytes=VMEM_LIMIT),
        name="proj_sample" if sample else "proj_prompt",
    )(x2d, w["g_pre"], w["w_t"], w["w_row"], w["w_a2p"], w["b_ap"], w["gq_t"], w["gk_t"])


def _top3_rows(g, n):
    idx = lax.broadcasted_iota(jnp.int32, g.shape, 0)
    sel = jnp.zeros(g.shape, F32)
    for _ in range(MOBA_TOPK):
        m = jnp.max(g, axis=0, keepdims=True)
        first = jnp.min(jnp.where(g == m, idx, n), axis=0, keepdims=True)
        pick = idx == first
        sel = jnp.where(pick, 1.0, sel)
        g = jnp.where(pick, -jnp.inf, g)
    return sel


_QB = 8


def _moba_prompt_kernel(qt_ref, krow_ref, vt_ref, kmean_ref, qaug_ref, kaug_ref, o_ref,
                        qop_ref, acc_ref, m_ref, s_ref, cm_ref, *, nb):
    blk = MOBA_BLOCK
    qw = _QB * blk
    i0 = pl.program_id(2) * _QB

    qt = qt_ref[0]
    row128 = lax.broadcasted_iota(jnp.int32, (2 * A_HEAD_DIM, qw), 0)
    lane128 = lax.broadcasted_iota(jnp.int32, (nb, LANE), 1)
    bidx = lax.broadcasted_iota(jnp.int32, (nb, qw), 0)
    iq = i0 + lax.broadcasted_iota(jnp.int32, (nb, qw), 1) // blk
    kmean = kmean_ref[0]
    qt_hi = qt.astype(BF16)
    qt_lo = (qt - qt_hi.astype(F32)).astype(BF16)

    for hh in range(2):
        in_head = (row128 >= hh * A_HEAD_DIM) & (row128 < (hh + 1) * A_HEAD_DIM)
        q_h = jnp.where(in_head, qt * _LOG2E, 0.0)
        km_h = jnp.where((lane128 >= hh * A_HEAD_DIM) & (lane128 < (hh + 1) * A_HEAD_DIM), kmean, 0.0)
        km_hi = km_h.astype(BF16)
        km_lo = (km_h - km_hi.astype(F32)).astype(BF16)
        gate = _dot(km_hi, qt_hi) + _dot(km_hi, qt_lo) + _dot(km_lo, qt_hi)
        past = bidx < iq
        sel = _top3_rows(jnp.where(past, gate, NEG), nb)
        keep = ((sel > 0.5) & past) | (bidx == iq)
        maskbias = jnp.where(keep, 0.0, NEG)
        for qb in range(_QB):
            c = hh * _QB + qb
            cols = slice(qb * blk, (qb + 1) * blk)
            qop = jnp.concatenate(
                [q_h[:, cols], qaug_ref[0, c], maskbias[:, cols],
                 jnp.zeros((2 * LANE - _MASK0 - nb, blk), F32)], axis=0)
            qop_ref[c] = qop.astype(BF16)
            acc_ref[c] = jnp.zeros(acc_ref.shape[1:], F32)
            m_ref[c] = jnp.full(m_ref.shape[1:], -jnp.inf, F32)

    lane_k = lax.broadcasted_iota(jnp.int32, (blk, LANE), 1)
    kaug_base = kaug_ref[...]
    ones_rows = jnp.ones((8, blk), BF16)
    kk = lax.broadcasted_iota(jnp.int32, (blk, blk), 0)
    qq = lax.broadcasted_iota(jnp.int32, (blk, blk), 1)

    def scores(j, slot, chains):
        off = pl.multiple_of(j * blk, blk)
        shift = ((j - i0) * blk).astype(F32)
        kaug = jnp.where((lane_k >= 3) & (lane_k < 6), shift.astype(BF16),
                         jnp.where(lane_k == 8 + j, jnp.ones((), BF16), kaug_base))
        kop = jnp.concatenate([krow_ref[0, pl.ds(off, blk), :], kaug], axis=1)
        for c in chains:
            s = _dot(kop, qop_ref[c])
            s_ref[slot, c] = s
            cm_ref[slot, c] = jnp.max(s.reshape(blk // 8, 8, blk), axis=0)

    def absorb(j, slot, chains):
        off = pl.multiple_of(j * blk, blk)
        vops = []
        for hh in range(2):
            v_t = vt_ref[0, hh * A_HEAD_DIM:(hh + 1) * A_HEAD_DIM, pl.ds(off, blk)].astype(BF16)
            vops.append(jnp.concatenate([v_t, ones_rows], axis=0))
        for c, causal in chains:
            s = s_ref[slot, c]
            if causal:
                s = jnp.where(kk <= qq, s, NEG)
            m_prev = m_ref[c][0:1]
            cmax = s if causal else cm_ref[slot, c]
            m_new = jnp.maximum(m_prev, jnp.max(cmax, axis=0, keepdims=True))
            alpha = jnp.exp2(m_prev - m_new)
            p = jnp.exp2(s - m_new).astype(BF16)
            acc_ref[c] = alpha * acc_ref[c] + _dot(vops[c // _QB], p)
            m_ref[c] = jnp.broadcast_to(m_new, m_ref.shape[1:])

    all_c = list(range(2 * _QB))
    plain = [(c, False) for c in all_c]
    scores(0, 0, all_c)

    def body(jj, carry):
        j = _QB * jj
        for d in range(_QB):
            scores(j + d + 1, (d + 1) % 2, all_c)
            absorb(j + d, d % 2, plain)
        return carry

    lax.fori_loop(0, i0 // _QB, body, 0)
    for d in range(_QB):
        if d + 1 < _QB:
            scores(i0 + d + 1, (d + 1) % 2, [c for c in all_c if c % _QB >= d + 1])
        absorb(i0 + d, d % 2, [(c, c % _QB == d) for c in all_c if c % _QB >= d])

    for qb in range(_QB):
        outs = []
        for hh in range(2):
            acc = acc_ref[hh * _QB + qb]
            outs.append(acc[0:A_HEAD_DIM] / acc[A_HEAD_DIM:A_HEAD_DIM + 1])
        o_ref[0, qb * blk:(qb + 1) * blk, :] = jnp.concatenate(outs, axis=0).T.astype(o_ref.dtype)


def _moba_prompt(qt, krow, vt, kmean, w):
    batch, _, t = qt.shape
    nb = t // MOBA_BLOCK
    blk = MOBA_BLOCK
    qw = _QB * blk
    return pl.pallas_call(
        functools.partial(_moba_prompt_kernel, nb=nb),
        grid=(batch, A_HEADS // 2, nb // _QB),
        in_specs=[
            pl.BlockSpec((1, 2 * A_HEAD_DIM, qw), lambda b, hp, i: (b, hp, i)),
            pl.BlockSpec((1, t, LANE), lambda b, hp, i: (b, 0, hp)),
            pl.BlockSpec((1, 2 * A_HEAD_DIM, t), lambda b, hp, i: (b, hp, 0)),
            pl.BlockSpec((1, nb, LANE), lambda b, hp, i: (b, 0, hp)),
            pl.BlockSpec((1, 2 * _QB, 8, blk), lambda b, hp, i: (hp, 0, 0, 0)),
            pl.BlockSpec((blk, LANE), lambda b, hp, i: (0, 0)),
        ],
        out_specs=pl.BlockSpec((1, qw, LANE), lambda b, hp, i: (b, i, hp)),
        out_shape=jax.ShapeDtypeStruct((batch, t, A_WIDTH), BF16),
        scratch_shapes=[
            pltpu.VMEM((2 * _QB, 2 * LANE, blk), BF16),
            pltpu.VMEM((2 * _QB, A_HEAD_DIM + 8, blk), F32),
            pltpu.VMEM((2 * _QB, 8, blk), F32),
            pltpu.VMEM((2, 2 * _QB, blk, blk), F32),
            pltpu.VMEM((2, 2 * _QB, 8, blk), F32),
        ],
        compiler_params=pltpu.CompilerParams(
            dimension_semantics=("arbitrary", "arbitrary", "arbitrary"),
            vmem_limit_bytes=VMEM_LIMIT),
        name="moba_prompt",
    )(qt, krow, vt, kmean.reshape(batch, nb, A_WIDTH), w["qaug"], w["kaug"])


def _moba_sample_kernel(pt_ref, q_ref, qn_ref, kn_ref, vn_ref, slope_ref, ck_hbm, cv_hbm, o_ref,
                        kbuf, sall, gate_ref, idx_v, idx_s, vbuf, sem_k, sem_v, sem_i,
                        *, n_pages, npg, ring):
    b = pl.program_id(0)
    nbatch = pl.num_programs(0)
    nxt = jnp.minimum(b + 1, nbatch - 1)
    n_chunks = n_pages // npg
    cw = npg * PAGE_SIZE
    bpc = cw // MOBA_BLOCK
    past = n_pages * PAGE_SIZE
    n_blocks = past // MOBA_BLOCK
    t_new = q_ref.shape[2]
    rows = A_HEADS * t_new
    ppb = MOBA_BLOCK // PAGE_SIZE
    slot_b = b % 2

    def k_copy(s, c, p):
        page = pt_ref[s, c * npg + p]
        return pltpu.make_async_copy(
            ck_hbm.at[0, page], kbuf.at[c % ring, p],
            sem_k.at[c % ring])

    def start_chunk(s, c):
        for p in range(npg):
            k_copy(s, c, p).start(priority=1)

    def start_ring(s):
        for c in range(ring):
            start_chunk(s, c)

    lane_g = lax.broadcasted_iota(jnp.int32, (t_new, LANE), 1)

    def split_q(qr):
        ops = []
        for h in range(A_HEADS):
            qh = qr[0, h]
            hi = qh.astype(BF16)
            ops.append(jnp.concatenate([hi, (qh - hi.astype(F32)).astype(BF16)], axis=0))
        return ops

    def sweep_chunk(s, q_ops, slot, c):
        for p in range(npg):
            k_copy(s, c, p).wait()
        for h in range(A_HEADS):
            kt = jnp.concatenate([kbuf[c % ring, p, h] for p in range(npg)], axis=1)
            s2 = _dot(q_ops[h], kt.astype(BF16))
            sc = s2[0:t_new] + s2[t_new:2 * t_new]
            sall[slot, h * t_new:(h + 1) * t_new, c * cw:(c + 1) * cw] = sc
            g = gate_ref[h * t_new:(h + 1) * t_new, :]
            for jb in range(bpc):
                t2 = (sc[:, jb * MOBA_BLOCK:jb * MOBA_BLOCK + LANE]
                      + sc[:, jb * MOBA_BLOCK + LANE:(jb + 1) * MOBA_BLOCK])
                g = jnp.where(lane_g == c * bpc + jb, jnp.sum(t2, axis=1, keepdims=True), g)
            gate_ref[h * t_new:(h + 1) * t_new, :] = g
        if c + ring < n_chunks:
            start_chunk(s, c + ring)

    picks = [(t, r) for t in range(t_new) for r in range(MOBA_TOPK)]

    def v_copy(s, h, t, r, pg):
        bsel = idx_s[h * t_new + t, r]
        page = pt_ref[s, ppb * bsel + pg]
        return pltpu.make_async_copy(
            cv_hbm.at[0, page, h], vbuf.at[h, (t * MOBA_TOPK + r) * ppb + pg], sem_v.at[h])

    def v_start(s, h):
        for t, r in picks:
            for pg in range(ppb):
                v_copy(s, h, t, r, pg).start()

    ahead = 2

    def select_blocks(s):
        lane_r = lax.broadcasted_iota(jnp.int32, (rows, LANE), 1)
        g = jnp.where(lane_r < n_blocks, gate_ref[...], -jnp.inf)
        idx = jnp.zeros((rows, LANE), jnp.int32)
        for r in range(MOBA_TOPK):
            m = jnp.max(g, axis=1, keepdims=True)
            first = jnp.min(jnp.where(g == m, lane_r, LANE), axis=1, keepdims=True)
            g = jnp.where(lane_r == first, -jnp.inf, g)
            idx = jnp.where(lane_r == r, first, idx)
        idx_v[...] = idx
        cp_i = pltpu.make_async_copy(idx_v, idx_s, sem_i)
        cp_i.start()
        cp_i.wait()
        for h0 in range(ahead):
            v_start(s, h0)

    @pl.when(b == 0)
    def _():
        start_ring(0)
        gate_ref[...] = jnp.zeros(gate_ref.shape, F32)
        q0 = split_q(q_ref)
        for c in range(n_chunks):
            sweep_chunk(0, q0, 0, c)
        start_ring(nxt)
        select_blocks(0)

    sub = lax.broadcasted_iota(jnp.int32, (t_new, MOBA_BLOCK), 0)
    klane = lax.broadcasted_iota(jnp.int32, (t_new, MOBA_BLOCK), 1)
    trow = lax.broadcasted_iota(jnp.int32, (t_new, t_new), 0)
    tcol = lax.broadcasted_iota(jnp.int32, (t_new, t_new), 1)

    def finish_head(h):
        if h + ahead < A_HEADS:
            v_start(b, h + ahead)
        r0 = h * t_new
        slope = slope_ref[r0:r0 + t_new, :][:, 0:1]
        pieces = []
        for t, r in picks:
            bsel = idx_s[r0 + t, r]
            off = pl.multiple_of(bsel * MOBA_BLOCK, MOBA_BLOCK)
            dist = ((past + t - bsel * MOBA_BLOCK) - klane).astype(F32)
            sc = sall[slot_b, r0:r0 + t_new, pl.ds(off, MOBA_BLOCK)] - slope * dist
            pieces.append(jnp.where(sub == t, sc, NEG))
        s_sel = jnp.concatenate(pieces, axis=1)
        s_new = _dot(q_ref[0, h], kn_ref[0, h], _NT) - slope * (trow - tcol).astype(F32)
        s_new = jnp.where(tcol <= trow, s_new, NEG)
        m = jnp.maximum(jnp.max(s_sel, axis=1, keepdims=True), jnp.max(s_new, axis=1, keepdims=True))
        p_sel = jnp.exp(s_sel - m)
        p_new = jnp.exp(s_new - m)
        l = jnp.sum(p_sel, axis=1, keepdims=True) + jnp.sum(p_new, axis=1, keepdims=True)
        for t, r in picks:
            for pg in range(ppb):
                v_copy(b, h, t, r, pg).wait()
        v_sel = jnp.concatenate([vbuf[h, j] for j in range(len(picks) * ppb)], axis=1)
        acc = _dot(p_sel.astype(BF16), v_sel.astype(BF16), _NT) + _dot(p_new, vn_ref[0, h])
        o_ref[0, h] = acc / l

    gate_ref[...] = jnp.zeros(gate_ref.shape, F32)
    qn = split_q(qn_ref)
    for i in range(max(n_chunks, A_HEADS)):
        if i < n_chunks:
            sweep_chunk(nxt, qn, 1 - slot_b, i)
        if i < A_HEADS:
            finish_head(i)

    @pl.when(b + 1 < nbatch)
    def _():
        start_ring(jnp.minimum(b + 2, nbatch - 1))
        select_blocks(nxt)


def _moba_sample(q4, kn4, vn4, cache_kt, cache_vt, page_table, slope_rows):
    nbatch, _, t_new, _ = q4.shape
    n_pages = page_table.shape[1]
    npg = 16
    ring = min(4, n_pages // npg)
    rows = A_HEADS * t_new
    past = n_pages * PAGE_SIZE
    shape4 = (1, A_HEADS, t_new, A_HEAD_DIM)
    blk4 = pl.BlockSpec(shape4, lambda b, pt: (b, 0, 0, 0))
    nxt4 = pl.BlockSpec(shape4, lambda b, pt: (jnp.minimum(b + 1, nbatch - 1), 0, 0, 0))
    return pl.pallas_call(
        functools.partial(_moba_sample_kernel, n_pages=n_pages, npg=npg, ring=ring),
        grid_spec=pltpu.PrefetchScalarGridSpec(
            num_scalar_prefetch=1,
            grid=(nbatch,),
            in_specs=[blk4, nxt4, blk4, blk4,
                      pl.BlockSpec((rows, LANE), lambda b, pt: (0, 0)),
                      pl.BlockSpec(memory_space=pl.ANY),
                      pl.BlockSpec(memory_space=pl.ANY)],
            out_specs=blk4,
            scratch_shapes=[
                pltpu.VMEM((ring, npg, A_HEADS, A_HEAD_DIM, PAGE_SIZE), F32),
                pltpu.VMEM((2, rows, past), F32),
                pltpu.VMEM((rows, LANE), F32),
                pltpu.VMEM((rows, LANE), jnp.int32),
                pltpu.SMEM((rows, LANE), jnp.int32),
                pltpu.VMEM((A_HEADS, t_new * MOBA_TOPK * (MOBA_BLOCK // PAGE_SIZE), A_HEAD_DIM, PAGE_SIZE), F32),
                pltpu.SemaphoreType.DMA((ring,)),
                pltpu.SemaphoreType.DMA((A_HEADS,)),
                pltpu.SemaphoreType.DMA,
            ]),
        out_shape=jax.ShapeDtypeStruct((nbatch, A_HEADS, t_new, A_HEAD_DIM), F32),
        compiler_params=pltpu.CompilerParams(dimension_semantics=("arbitrary",),
                                             vmem_limit_bytes=VMEM_LIMIT),
        name="moba_sample",
    )(page_table, q4, q4, kn4, vn4, slope_rows, cache_kt, cache_vt)


def _gla_kernel(q_ref, k_ref, g_ref, v_ref, s0_ref, o_ref, sfin_ref, state, *, chunk, n_chunks, nbb):
    c = chunk
    mx = BF16 if c >= 16 else F32
    hpl = LANE // B_KEY_DIM
    zpad = jnp.zeros((B_KEY_DIM, B_VAL_DIM), F32)

    @pl.when(pl.program_id(2) == 0)
    def _():
        for bi in range(nbb):
            for hh in range(hpl):
                parts = [s0_ref[bi, hh] if j == hh else zpad for j in range(hpl)]
                state[bi, hh] = jnp.concatenate(parts, axis=0).T

    ri = lax.broadcasted_iota(jnp.int32, (c, LANE), 0)
    lane_c = lax.broadcasted_iota(jnp.int32, (c, LANE), 1)
    head_lanes = [(lane_c // B_KEY_DIM) == hh for hh in range(hpl)]
    ti = lax.broadcasted_iota(jnp.int32, (c, c), 0)
    si = lax.broadcasted_iota(jnp.int32, (c, c), 1)
    tril = jnp.where(si <= ti, 1.0, 0.0).astype(mx)
    sub3 = lax.broadcasted_iota(jnp.int32, (c // 8, 8, LANE), 1)
    levels = []
    half = c // 2
    while half >= 1:
        width = 2 * half
        sb = int(np.log2(width))
        upper = (ri & (width - 1)) >= half
        pair = ((ti >> sb) == (si >> sb)) & ((ti & (width - 1)) >= half) & ((si & (width - 1)) < half)
        levels.append((half, width, upper, pair))
        half //= 2

    seq = [(bi, ci) for bi in range(nbb) for ci in range(n_chunks)]
    loaded = []
    for bi, ci in seq:
        g = g_ref[bi, pl.ds(ci * c, c), :]
        g_hi = g.astype(BF16)
        g_lo = (g - g_hi.astype(F32)).astype(BF16)
        loaded.append((g, _dot(tril, g_hi.astype(mx)) + _dot(tril, g_lo.astype(mx))))

    prepared = []
    for (bi, ci), (g, b) in zip(seq, loaded):
        rows = pl.ds(ci * c, c)
        q = q_ref[bi, rows, :]
        k = k_ref[bi, rows, :]
        v32 = v_ref[bi, rows, :]
        b_last = b[c - 1:c, :]
        b3 = b.reshape(c // 8, 8, LANE)
        zs = []
        for half, width, upper, _ in levels:
            if half == 1:
                x = jnp.where(upper, g, 0.0)
            else:
                if half >= 8:
                    mid = jnp.concatenate(
                        [jnp.broadcast_to(b[m * width + half - 1:m * width + half, :], (width, LANE))
                         for m in range(c // width)], axis=0)
                else:
                    mid3 = jnp.broadcast_to(b3[:, half - 1:half, :], b3.shape)
                    for m in range(1, 8 // width):
                        r = m * width + half - 1
                        mid3 = jnp.where(sub3 >= m * width,
                                         jnp.broadcast_to(b3[:, r:r + 1, :], b3.shape), mid3)
                    mid = mid3.reshape(c, LANE)
                x = jnp.where(upper, b - mid, mid - b)
            zs.append(jnp.where(upper, q, k) * jnp.exp2(x))
        qd = q * jnp.exp2(b)
        kd = k * jnp.exp2(b_last - b)
        qk = q * k
        heads = []
        for hh in range(hpl):
            own = head_lanes[hh]
            cut = lambda a: jnp.where(own, a, 0.0).astype(mx)
            v_h = v32[:, hh * B_VAL_DIM:(hh + 1) * B_VAL_DIM]
            o_same = jnp.sum(jnp.where(own, qk, 0.0), axis=1, keepdims=True) * v_h
            heads.append((v_h.astype(mx), [cut(z) for z in zs], cut(qd), cut(kd), o_same))
        prepared.append((heads, jnp.exp2(b_last)))

    grams, updates = [], []
    for heads, _ in prepared:
        grams.append([[_dot(z, z, _NT) for z in zh] for _, zh, _, _, _ in heads])
        updates.append([_dot(v, kd, _TN) for v, _, _, kd, _ in heads])

    attns = []
    for gr in grams:
        per_head = []
        for gh in gr:
            attn = jnp.where(levels[0][3], gh[0], 0.0)
            for (_, _, _, pair), gm in zip(levels[1:], gh[1:]):
                attn = attn + jnp.where(pair, gm, 0.0)
            per_head.append(attn.astype(mx))
        attns.append(per_head)

    states = []
    for n, (bi, ci) in enumerate(seq):
        a_last = prepared[n][1]
        cur = []
        for hh in range(hpl):
            st = state[bi, hh] if ci == 0 else states[-1][hh][1]
            cur.append((st, a_last * st + updates[n][hh]))
            if ci == n_chunks - 1:
                state[bi, hh] = cur[-1][1]
        states.append(cur)

    for n, (bi, ci) in enumerate(seq):
        outs = []
        for hh, (v, _, qd, _, o_same) in enumerate(prepared[n][0]):
            outs.append(o_same + _dot(attns[n][hh], v) + _dot(qd, states[n][hh][0].astype(mx), _NT))
        o_ref[bi, pl.ds(ci * c, c), :] = jnp.concatenate(outs, axis=1).astype(o_ref.dtype)

    @pl.when(pl.program_id(2) == pl.num_programs(2) - 1)
    def _():
        for bi in range(nbb):
            for hh in range(hpl):
                sfin_ref[bi, hh] = state[bi, hh].T[hh * B_KEY_DIM:(hh + 1) * B_KEY_DIM, :]


def _gla(qb, kb, lg, vb, s0, chunk):
    batch, t, _ = qb.shape
    tc = min(t, 2048)
    n_chunks = tc // chunk
    nbb = 8 if (t == tc and batch % 8 == 0) else 1
    hpl = LANE // B_KEY_DIM
    seq = pl.BlockSpec((nbb, tc, LANE), lambda b, h, c: (b, c, h))
    seq_v = pl.BlockSpec((nbb, tc, hpl * B_VAL_DIM), lambda b, h, c: (b, c, h))
    st = pl.BlockSpec((nbb, hpl, B_KEY_DIM, B_VAL_DIM), lambda b, h, c: (b, h, 0, 0))
    return pl.pallas_call(
        functools.partial(_gla_kernel, chunk=chunk, n_chunks=n_chunks, nbb=nbb),
        grid=(batch // nbb, B_HEADS // hpl, t // tc),
        in_specs=[seq, seq, seq, seq_v, st],
        out_specs=[seq_v, st],
        out_shape=[jax.ShapeDtypeStruct((batch, t, B_VAL_WIDTH), BF16 if chunk >= 16 else F32),
                   jax.ShapeDtypeStruct((batch, B_HEADS, B_KEY_DIM, B_VAL_DIM), F32)],
        scratch_shapes=[pltpu.VMEM((nbb, hpl, B_VAL_DIM, LANE), F32)],
        compiler_params=pltpu.CompilerParams(
            dimension_semantics=("arbitrary", "arbitrary", "arbitrary"),
            vmem_limit_bytes=VMEM_LIMIT),
        name="gla_c%d" % chunk,
    )(qb, kb, lg, vb, s0)


def _merge_kernel(x_ref, oa_ref, sza_ref, ob_ref, szb_ref, gg_ref, wo_ref, y_ref):
    ya = oa_ref[...].astype(F32) * sza_ref[...].astype(F32)
    ob = ob_ref[...].astype(F32)
    parts = []
    for h in range(B_HEADS):
        oh = ob[:, h * B_VAL_DIM:(h + 1) * B_VAL_DIM]
        ms = jnp.mean(oh * oh, axis=-1, keepdims=True)
        parts.append(oh * lax.rsqrt(ms + EPS))
    yb = (jnp.concatenate(parts, axis=1) * gg_ref[...]) * szb_ref[...].astype(F32)
    cat = jnp.concatenate([ya, yb], axis=1).astype(BF16)
    y_ref[...] = x_ref[...] + _dot(cat, wo_ref[...])


def _merge(x2d, oa, sza, ob, szb, w):
    n = x2d.shape[0]
    tm = min(n, 1024)
    row = lambda i: (i, 0)
    const = lambda i: (0, 0)
    return pl.pallas_call(
        _merge_kernel,
        grid=(n // tm,),
        in_specs=[pl.BlockSpec((tm, D_MODEL), row), pl.BlockSpec((tm, 512), row),
                  pl.BlockSpec((tm, 512), row), pl.BlockSpec((tm, 512), row),
                  pl.BlockSpec((tm, 512), row), pl.BlockSpec((1, 512), const),
                  pl.BlockSpec((D_MODEL, D_MODEL), const)],
        out_specs=pl.BlockSpec((tm, D_MODEL), row),
        out_shape=jax.ShapeDtypeStruct((n, D_MODEL), F32),
        compiler_params=pltpu.CompilerParams(dimension_semantics=("arbitrary",),
                                             vmem_limit_bytes=VMEM_LIMIT),
        name="merge_out",
    )(x2d, oa, sza, ob, szb, w["g_gla"], w["w_out"])


def _layer_weights(g_pre, w_in, g_q, g_k, w_a2, b_a, g_gla, w_out):
    o = _OFF
    w_t = w_in[:, o[0]:o[3]].T.astype(BF16)
    w_row = jnp.concatenate([
        w_in[:, o[3]:o[4]], w_in[:, o[7]:o[8]], w_in[:, o[4]:o[5]], w_in[:, o[5]:o[6]],
        w_in[:, o[6]:o[7]],
        jnp.pad(w_in[:, o[8]:o[9]], ((0, 0), (0, LANE - GATE_RANK)))], axis=1).astype(BF16)
    w_a2p = jnp.pad(w_a2, ((0, LANE - GATE_RANK), (0, 0))).astype(BF16)
    b_ap = b_a.reshape(1, B_KEY_WIDTH)
    gq_t = jnp.broadcast_to(jnp.tile(g_q, A_HEADS)[:, None], (A_WIDTH, _PROJ_ROWS))
    gk_t = jnp.broadcast_to(jnp.tile(g_k, A_HEADS)[:, None], (A_WIDTH, _PROJ_ROWS))
    slopes = jnp.asarray([2.0 ** (-8.0 * (h + 1) / A_HEADS) for h in range(A_HEADS)], F32)
    c = slopes * _LOG2E
    pieces = []
    rem = c
    for _ in range(3):
        pc = rem.astype(BF16).astype(F32)
        pieces.append(pc)
        rem = rem - pc
    qq = jnp.arange(MOBA_BLOCK, dtype=F32)
    qaug = jnp.zeros((A_HEADS, _QB, 8, MOBA_BLOCK), F32)
    for p_i, pc in enumerate(pieces):
        qaug = qaug.at[:, :, p_i, :].set(pc[:, None, None])
        qaug = qaug.at[:, :, 3 + p_i, :].set(pc[:, None, None])
    qpos = qq[None, None, :] + (MOBA_BLOCK * jnp.arange(_QB, dtype=F32))[None, :, None]
    qaug = qaug.at[:, :, 6, :].set(-c[:, None, None] * qpos)
    qaug = qaug.reshape(A_HEADS // 2, 2 * _QB, 8, MOBA_BLOCK)
    kaug = jnp.zeros((MOBA_BLOCK, LANE), F32)
    kaug = kaug.at[:, 0:3].set(qq[:, None]).at[:, 6].set(1.0).astype(BF16)
    return {
        "g_pre": g_pre.reshape(1, D_MODEL), "w_t": w_t, "w_row": w_row, "w_a2p": w_a2p, "b_ap": b_ap,
        "gq_t": gq_t, "gk_t": gk_t, "g_gla": g_gla.reshape(1, B_VAL_WIDTH),
        "w_out": w_out.astype(BF16), "qaug": qaug, "kaug": kaug, "slopes": slopes,
    }


def kernel(x_prompt, x_sample, cache_k, cache_v, state_gla, page_table, g_pre, w_in, g_q, g_k, w_a2, b_a, g_gla, w_out):
    depth = g_pre.shape[0]
    batch, t, _ = x_prompt.shape
    nb_s, t_new, _ = x_sample.shape
    cache_kt = jnp.transpose(cache_k, (0, 1, 3, 4, 2))
    cache_vt = jnp.transpose(cache_v, (0, 1, 3, 4, 2))
    yp = x_prompt.reshape(batch * t, D_MODEL)
    ys = x_sample.reshape(nb_s * t_new, D_MODEL)
    kp_l, vp_l, sp_l, ks_l, vs_l, ss_l = [], [], [], [], [], []
    for l in range(depth):
        w = _layer_weights(g_pre[l], w_in[l], g_q[l], g_k[l], w_a2[l], b_a[l], g_gla[l], w_out[l])
        qt, kt, vt, krow, kmean, sza, szb, qb, kb, vb, lg = _proj(yp, w, batch, sample=False)
        oa = _moba_prompt(qt, krow, vt, kmean, w)
        r3 = lambda a: a.reshape(batch, t, a.shape[-1])
        s0 = jnp.zeros((batch, B_HEADS, B_KEY_DIM, B_VAL_DIM), F32)
        ob, s_fin = _gla(r3(qb), r3(kb), r3(lg), r3(vb), s0, chunk=64)
        yp = _merge(yp, oa.reshape(batch * t, A_WIDTH), sza, ob.reshape(batch * t, 512), szb, w)
        kp_l.append(jnp.transpose(kt.reshape(batch, A_HEADS, A_HEAD_DIM, t), (0, 3, 1, 2)))
        vp_l.append(jnp.transpose(vt.reshape(batch, A_HEADS, A_HEAD_DIM, t), (0, 3, 1, 2)))
        sp_l.append(s_fin)
        qs, ks, vs, sza, szb, qb, kb, vb, lg = _proj(ys, w, nb_s, sample=True)
        h4 = lambda a: jnp.transpose(a.reshape(nb_s, t_new, A_HEADS, A_HEAD_DIM), (0, 2, 1, 3))
        slope_rows = jnp.broadcast_to(jnp.repeat(w["slopes"], t_new)[:, None], (A_HEADS * t_new, LANE))
        oa4 = _moba_sample(h4(qs), h4(ks), h4(vs), cache_kt[l:l + 1], cache_vt[l:l + 1],
                           page_table, slope_rows)
        oa = jnp.transpose(oa4, (0, 2, 1, 3)).reshape(nb_s * t_new, A_WIDTH)
        r3 = lambda a: a.reshape(nb_s, t_new, a.shape[-1])
        ob, s_new = _gla(r3(qb), r3(kb), r3(lg), r3(vb), state_gla[l], chunk=t_new)
        ys = _merge(ys, oa, sza, ob.reshape(nb_s * t_new, 512), szb, w)
        ks_l.append(ks.reshape(nb_s, t_new, A_HEADS, A_HEAD_DIM))
        vs_l.append(vs.reshape(nb_s, t_new, A_HEADS, A_HEAD_DIM))
        ss_l.append(s_new)
    return (yp.reshape(batch, t, D_MODEL), ys.reshape(nb_s, t_new, D_MODEL),
            jnp.stack(kp_l), jnp.stack(vp_l), jnp.stack(sp_l),
            jnp.stack(ks_l), jnp.stack(vs_l), jnp.stack(ss_l))
```
